```python
import math
import jax, jax.numpy as jnp
from jax import lax
import numpy as np

D_MODEL = 1024
BATCH = 2
SEQ = 8192
DEPTH = 2
DEC_BATCH = 8
DEC_SEQ = 32
PAST_LEN = 1024

CHUNK = 64
MLA_HEADS = 4
MLA_NOPE = 128
MLA_ROPE = 64
MLA_V = 128
Q_LORA = 256
KV_LORA = 128
MLA_WIDTH = MLA_HEADS * MLA_V
MLA_SCALE = (MLA_NOPE + MLA_ROPE) ** -0.5
ROPE_THETA = 10000.0
ATTN_BLOCK = 128
S5_GROUP = 16
S5_GROUPS = 16
S5_WIDTH = S5_GROUP * S5_GROUPS
S5_STATE = 64
DT_MIN = 0.001
DT_MAX = 0.1
GMLP_HEADS = 4
GMLP_HEAD_DIM = 64
GMLP_WIDTH = GMLP_HEADS * GMLP_HEAD_DIM
GMLP_CHUNK = 128
D_MIX = MLA_WIDTH + S5_WIDTH + GMLP_WIDTH
IN_COLS = Q_LORA + KV_LORA + MLA_ROPE + S5_WIDTH + 2 * GMLP_WIDTH
N_EXPERTS = 32
TOP_K = 4
D_FF = D_MODEL
SWIGLU_LIMIT = 7.0
SWIGLU_ALPHA = 1.702
MOE_BLOCK = 128
DN_ALPHA = (2 * DEPTH) ** 0.25
DN_BETA = (8 * DEPTH) ** -0.25
NORM_EPS = 1e-5

kernel_name = 'hybrid_s5_mla_gmlp_moe_stream_step'


def _rms_norm(x, g):
    xf = x.astype(jnp.float32)
    y = xf * lax.rsqrt(jnp.mean(xf * xf, axis=-1, keepdims=True) + NORM_EPS)
    return (y * g.astype(jnp.float32)).astype(x.dtype)


def _layer_norm(x, g, b):
    xf = x.astype(jnp.float32)
    xc = xf - jnp.mean(xf, axis=-1, keepdims=True)
    var = jnp.mean(xc * xc, axis=-1, keepdims=True)
    y = xc * lax.rsqrt(var + NORM_EPS) * g.astype(jnp.float32) + b.astype(jnp.float32)
    return y.astype(x.dtype)


def _rope(x, pos):
    half = x.shape[-1] // 2
    inv_freq = ROPE_THETA ** (-jnp.arange(half, dtype=jnp.float32) / half)
    ang = pos.astype(jnp.float32)[:, None] * inv_freq[None, :]
    cos = jnp.cos(ang)[None, :, None, :]
    sin = jnp.sin(ang)[None, :, None, :]
    xf = x.astype(jnp.float32)
    x1, x2 = xf[..., :half], xf[..., half:]
    return jnp.concatenate([x1 * cos - x2 * sin, x1 * sin + x2 * cos], axis=-1).astype(x.dtype)


def _chunk_mask(q_pos, k_pos):
    return (k_pos[None, :] // CHUNK) <= (q_pos[:, None] // CHUNK)


def _mla_attend(q_nope, q_rope, k_nope, k_rope, v, q_pos, k_pos):
    s = (jnp.einsum('bqhd,bkhd->bhqk', q_nope, k_nope)
         + jnp.einsum('bqhr,bkr->bhqk', q_rope, k_rope))
    s = s.astype(jnp.float32) * MLA_SCALE
    s = jnp.where(_chunk_mask(q_pos, k_pos)[None, None], s, -jnp.inf)
    p = jax.nn.softmax(s, axis=-1).astype(v.dtype)
    return jnp.einsum('bhqk,bkhd->bqhd', p, v)


def _mla_prompt_attend(q_nope, q_rope, k_nope, k_rope, v, pos):
    b, s = q_nope.shape[:2]
    nb = s // ATTN_BLOCK
    qn = q_nope.reshape(b, nb, ATTN_BLOCK, MLA_HEADS, MLA_NOPE).transpose(1, 0, 2, 3, 4)
    qr = q_rope.reshape(b, nb, ATTN_BLOCK, MLA_HEADS, MLA_ROPE).transpose(1, 0, 2, 3, 4)
    qp = pos.reshape(nb, ATTN_BLOCK)
    out = lax.map(lambda blk: _mla_attend(blk[0], blk[1], k_nope, k_rope, v, blk[2], pos), (qn, qr, qp))
    return out.transpose(1, 0, 2, 3, 4).reshape(b, s, MLA_HEADS, MLA_V)


def _mla_mixer(proj_q, proj_kv, proj_kr, pos, past_latent, past_krope, lp):
    b, t = proj_q.shape[:2]
    q = (_rms_norm(proj_q, lp['q_a_norm_g']) @ lp['w_q_b']).reshape(b, t, MLA_HEADS, MLA_NOPE + MLA_ROPE)
    q_nope = q[..., :MLA_NOPE]
    q_rope = _rope(q[..., MLA_NOPE:], pos)
    latent = _rms_norm(proj_kv, lp['kv_a_norm_g'])
    k_rope = _rope(proj_kr[:, :, None, :], pos)[:, :, 0, :]
    if past_latent is None:
        lat_all, kr_all = latent, k_rope
    else:
        lat_all = jnp.concatenate([past_latent.astype(latent.dtype), latent], axis=1)
        kr_all = jnp.concatenate([past_krope.astype(k_rope.dtype), k_rope], axis=1)
    n_keys = lat_all.shape[1]
    kv = (lat_all @ lp['w_kv_b']).reshape(b, n_keys, MLA_HEADS, MLA_NOPE + MLA_V)
    k_nope, v = kv[..., :MLA_NOPE], kv[..., MLA_NOPE:]
    if past_latent is None:
        out = _mla_prompt_attend(q_nope, q_rope, k_nope, kr_all, v, pos)
    else:
        k_pos = jnp.arange(n_keys, dtype=jnp.int32)
        out = _mla_attend(q_nope, q_rope, k_nope, kr_all, v, pos, k_pos)
    return out.reshape(b, t, MLA_WIDTH), latent, k_rope


def _cmul_scan_op(e1, e2):
    a1r, a1i, b1r, b1i = e1
    a2r, a2i, b2r, b2i = e2
    return (a2r * a1r - a2i * a1i,
            a2r * a1i + a2i * a1r,
            a2r * b1r - a2i * b1i + b2r,
            a2r * b1i + a2i * b1r + b2i)


def _s5_mixer(u, s0_re, s0_im, lp):
    b, t = u.shape[:2]
    f32 = jnp.float32
    lam_re = lp['s5_lam_re'].astype(f32)
    lam_im = lp['s5_lam_im'].astype(f32)
    dt = jnp.exp(lp['s5_log_dt'].astype(f32))[:, None]
    mag = jnp.exp(lam_re * dt)
    ab_re = mag * jnp.cos(lam_im * dt)
    ab_im = mag * jnp.sin(lam_im * dt)
    den = lam_re * lam_re + lam_im * lam_im
    nr, ni = ab_re - 1.0, ab_im
    f_re = (nr * lam_re + ni * lam_im) / den
    f_im = (ni * lam_re - nr * lam_im) / den
    b_re = lp['s5_b_re'].astype(f32)
    b_im = lp['s5_b_im'].astype(f32)
    bb_re = f_re[..., None] * b_re - f_im[..., None] * b_im
    bb_im = f_re[..., None] * b_im + f_im[..., None] * b_re
    uf = u.astype(f32).reshape(b, t, S5_GROUPS, S5_GROUP)
    bu_re = jnp.einsum('btgc,gnc->btgn', uf, bb_re)
    bu_im = jnp.einsum('btgc,gnc->btgn', uf, bb_im)
    a_re = jnp.broadcast_to(ab_re, bu_re.shape)
    a_im = jnp.broadcast_to(ab_im, bu_im.shape)
    cum_re, cum_im, sc_re, sc_im = lax.associative_scan(_cmul_scan_op, (a_re, a_im, bu_re, bu_im), axis=1)
    st_re = cum_re * s0_re[:, None] - cum_im * s0_im[:, None] + sc_re
    st_im = cum_re * s0_im[:, None] + cum_im * s0_re[:, None] + sc_im
    y = (jnp.einsum('btgn,gcn->btgc', st_re, lp['s5_c_re'].astype(f32))
         - jnp.einsum('btgn,gcn->btgc', st_im, lp['s5_c_im'].astype(f32))
         + lp['s5_d'].astype(f32) * uf).reshape(b, t, S5_WIDTH)
    z = jax.nn.gelu(y)
    out = z * jax.nn.sigmoid(z @ lp['s5_w_glu'].astype(f32) + lp['s5_b_glu'].astype(f32))
    return out.astype(u.dtype), st_re[:, -1], st_im[:, -1]


def _gmlp_mixer(zu, chunk_len, lp):
    b, t = zu.shape[:2]
    z = jax.nn.gelu(zu)
    u, v = z[..., :GMLP_WIDTH], z[..., GMLP_WIDTH:]
    v = _layer_norm(v.reshape(b, t, GMLP_HEADS, GMLP_HEAD_DIM),
                    lp['gmlp_norm_g'].reshape(GMLP_HEADS, GMLP_HEAD_DIM),
                    lp['gmlp_norm_b'].reshape(GMLP_HEADS, GMLP_HEAD_DIM))
    i = jnp.arange(chunk_len)
    mask = (i[None, :] // CHUNK) <= (i[:, None] // CHUNK)
    w_s = jnp.where(mask[None], lp['gmlp_w_s'][:, :chunk_len, :chunk_len], 0.0).astype(v.dtype)
    vc = v.reshape(b, t // chunk_len, chunk_len, GMLP_HEADS, GMLP_HEAD_DIM)
    bias = lp['gmlp_b_s'][:, :chunk_len].T[None, None, :, :, None].astype(v.dtype)
    mix = jnp.einsum('hij,bcjhd->bcihd', w_s, vc) + bias
    out = u * mix.reshape(b, t, GMLP_WIDTH)
    return out, v.reshape(b, t, GMLP_WIDTH)


def _moe(x, lp):
    b, t, d = x.shape
    xf = x.reshape(-1, d)
    n = xf.shape[0]
    logits = (xf @ lp['router_w'] + lp['router_b']).astype(jnp.float32)
    top_v, top_i = lax.top_k(logits, TOP_K)
    gates = jax.nn.softmax(top_v, axis=-1)
    flat_e = top_i.reshape(-1)
    flat_tok = jnp.repeat(jnp.arange(n, dtype=jnp.int32), TOP_K)
    flat_g = gates.reshape(-1)
    order = jnp.argsort(flat_e)
    e_sorted = flat_e[order]
    counts = jnp.bincount(flat_e, length=N_EXPERTS)
    padded = (counts + MOE_BLOCK - 1) // MOE_BLOCK * MOE_BLOCK
    pad_end = jnp.cumsum(padded)
    pad_start = pad_end - padded
    grp_start = jnp.cumsum(counts) - counts
    dest = pad_start[e_sorted] + jnp.arange(n * TOP_K, dtype=jnp.int32) - grp_start[e_sorted]
    n_rows = (-(-(n * TOP_K) // MOE_BLOCK) + N_EXPERTS) * MOE_BLOCK
    row_tok = jnp.zeros((n_rows,), jnp.int32).at[dest].set(flat_tok[order])
    row_gate = jnp.zeros((n_rows,), jnp.float32).at[dest].set(flat_g[order])
    n_blk = n_rows // MOE_BLOCK
    blk_start = jnp.arange(n_blk, dtype=pad_end.dtype) * MOE_BLOCK
    blk_e = jnp.minimum(jnp.searchsorted(pad_end, blk_start, side='right'), N_EXPERTS - 1)
    xb = xf[row_tok].reshape(n_blk, MOE_BLOCK, d)
    w_gu, b_gu, w_dn, b_dn = lp['moe_w_gu'], lp['moe_b_gu'], lp['moe_w_down'], lp['moe_b_down']

    def expert_block(args):
        xe, e = args
        gu = xe @ w_gu[e] + b_gu[e]
        gate = jnp.minimum(gu[..., :D_FF], SWIGLU_LIMIT)
        up = jnp.clip(gu[..., D_FF:], -SWIGLU_LIMIT, SWIGLU_LIMIT)
        h = (up + 1.0) * (gate * jax.nn.sigmoid(SWIGLU_ALPHA * gate))
        return h @ w_dn[e] + b_dn[e]

    yb = lax.map(expert_block, (xb, blk_e)).reshape(n_rows, d)
    y = jnp.zeros_like(xf).at[row_tok].add(yb * row_gate[:, None].astype(yb.dtype))
    return y.reshape(b, t, d)


def _layer(x, pos, past_latent, past_krope, s0_re, s0_im, gmlp_len, lp):
    proj = x @ lp['w_in']
    o1 = Q_LORA
    o2 = o1 + KV_LORA
    o3 = o2 + MLA_ROPE
    o4 = o3 + S5_WIDTH
    attn, latent, k_rope = _mla_mixer(proj[..., :o1], proj[..., o1:o2], proj[..., o2:o3],
                                      pos, past_latent, past_krope, lp)
    ssm, s_re, s_im = _s5_mixer(proj[..., o3:o4], s0_re, s0_im, lp)
    gm, v_rows = _gmlp_mixer(proj[..., o4:], gmlp_len, lp)
    g = lp['mix_norm_g']
    mixed = jnp.concatenate([_rms_norm(attn, g[:MLA_WIDTH]),
                             _rms_norm(ssm, g[MLA_WIDTH:MLA_WIDTH + S5_WIDTH]),
                             _rms_norm(gm, g[MLA_WIDTH + S5_WIDTH:])], axis=-1)
    x = _layer_norm(DN_ALPHA * x + mixed @ lp['w_out'], lp['ln1_g'], lp['ln1_b'])
    x = _layer_norm(DN_ALPHA * x + _moe(x, lp), lp['ln2_g'], lp['ln2_b'])
    return x, latent, k_rope, s_re, s_im, v_rows


def setup_inputs(seed: int = 0) -> dict:
    key = jax.random.key(seed)
    ks = iter(jax.random.split(key, 48))
    f32 = jnp.float32

    def nrm(shape, scale):
        return scale * jax.random.normal(next(ks), shape, f32)

    L = DEPTH
    x_prompt = nrm((BATCH, SEQ, D_MODEL), 1.0)
    x_sample = nrm((DEC_BATCH, DEC_SEQ, D_MODEL), 1.0)
    cache_mla_latent = nrm((L, DEC_BATCH, PAST_LEN, KV_LORA), 1.0)
    cache_mla_krope = nrm((L, DEC_BATCH, PAST_LEN, MLA_ROPE), 1.0)
    state_s5_re = nrm((L, DEC_BATCH, S5_GROUPS, S5_STATE), 0.1)
    state_s5_im = nrm((L, DEC_BATCH, S5_GROUPS, S5_STATE), 0.1)
    w_in = nrm((L, D_MODEL, IN_COLS), D_MODEL ** -0.5)
    q_a_norm_g = 1.0 + nrm((L, Q_LORA), 0.02)
    w_q_b = nrm((L, Q_LORA, MLA_HEADS * (MLA_NOPE + MLA_ROPE)), Q_LORA ** -0.5)
    kv_a_norm_g = 1.0 + nrm((L, KV_LORA), 0.02)
    w_kv_b = nrm((L, KV_LORA, MLA_HEADS * (MLA_NOPE + MLA_V)), KV_LORA ** -0.5)
    s5_lam_re = -0.5 + nrm((L, S5_GROUPS, S5_STATE), 0.01)
    s5_lam_im = jnp.pi * jnp.arange(S5_STATE, dtype=f32) + nrm((L, S5_GROUPS, S5_STATE), 0.01)
    s5_log_dt = math.log(DT_MIN) + jax.random.uniform(next(ks), (L, S5_GROUPS), f32) * (math.log(DT_MAX) - math.log(DT_MIN))
    s5_b_re = nrm((L, S5_GROUPS, S5_STATE, S5_GROUP), (2 * S5_GROUP) ** -0.5)
    s5_b_im = nrm((L, S5_GROUPS, S5_STATE, S5_GROUP), (2 * S5_GROUP) ** -0.5)
    s5_c_re = nrm((L, S5_GROUPS, S5_GROUP, S5_STATE), (2 * S5_STATE) ** -0.5)
    s5_c_im = nrm((L, S5_GROUPS, S5_GROUP, S5_STATE), (2 * S5_STATE) ** -0.5)
    s5_d = nrm((L, S5_GROUPS, S5_GROUP), 1.0)
    s5_w_glu = nrm((L, S5_WIDTH, S5_WIDTH), S5_WIDTH ** -0.5)
    s5_b_glu = nrm((L, S5_WIDTH), 0.01)
    gmlp_norm_g = 1.0 + nrm((L, GMLP_WIDTH), 0.02)
    gmlp_norm_b = nrm((L, GMLP_WIDTH), 0.02)
    gmlp_w_s = nrm((L, GMLP_HEADS, GMLP_CHUNK, GMLP_CHUNK), GMLP_CHUNK ** -0.5)
    gmlp_b_s = 1.0 + nrm((L, GMLP_HEADS, GMLP_CHUNK), 0.02)
    mix_norm_g = 1.0 + nrm((L, D_MIX), 0.02)
    w_out = nrm((L, D_MIX, D_MODEL), DN_BETA * D_MIX ** -0.5)
    ln1_g = 1.0 + nrm((L, D_MODEL), 0.02)
    ln1_b = nrm((L, D_MODEL), 0.02)
    router_w = nrm((L, D_MODEL, N_EXPERTS), D_MODEL ** -0.5)
    router_b = nrm((L, N_EXPERTS), 0.01)
    moe_w_gu = nrm((L, N_EXPERTS, D_MODEL, 2 * D_FF), D_MODEL ** -0.5)
    moe_b_gu = nrm((L, N_EXPERTS, 2 * D_FF), 0.01)
    moe_w_down = nrm((L, N_EXPERTS, D_FF, D_MODEL), DN_BETA * D_FF ** -0.5)
    moe_b_down = nrm((L, N_EXPERTS, D_MODEL), 0.01)
    ln2_g = 1.0 + nrm((L, D_MODEL), 0.02)
    ln2_b = nrm((L, D_MODEL), 0.02)
    return {'x_prompt': x_prompt, 'x_sample': x_sample,
            'cache_mla_latent': cache_mla_latent, 'cache_mla_krope': cache_mla_krope,
            'state_s5_re': state_s5_re, 'state_s5_im': state_s5_im,
            'w_in': w_in, 'q_a_norm_g': q_a_norm_g, 'w_q_b': w_q_b, 'kv_a_norm_g': kv_a_norm_g, 'w_kv_b': w_kv_b,
            's5_lam_re': s5_lam_re, 's5_lam_im': s5_lam_im, 's5_log_dt': s5_log_dt,
            's5_b_re': s5_b_re, 's5_b_im': s5_b_im, 's5_c_re': s5_c_re, 's5_c_im': s5_c_im, 's5_d': s5_d,
            's5_w_glu': s5_w_glu, 's5_b_glu': s5_b_glu,
            'gmlp_norm_g': gmlp_norm_g, 'gmlp_norm_b': gmlp_norm_b, 'gmlp_w_s': gmlp_w_s, 'gmlp_b_s': gmlp_b_s,
            'mix_norm_g': mix_norm_g, 'w_out': w_out, 'ln1_g': ln1_g, 'ln1_b': ln1_b,
            'router_w': router_w, 'router_b': router_b, 'moe_w_gu': moe_w_gu, 'moe_b_gu': moe_b_gu,
            'moe_w_down': moe_w_down, 'moe_b_down': moe_b_down, 'ln2_g': ln2_g, 'ln2_b': ln2_b}


def reference(x_prompt, x_sample, cache_mla_latent, cache_mla_krope, state_s5_re, state_s5_im,
              w_in, q_a_norm_g, w_q_b, kv_a_norm_g, w_kv_b,
              s5_lam_re, s5_lam_im, s5_log_dt, s5_b_re, s5_b_im, s5_c_re, s5_c_im, s5_d,
              s5_w_glu, s5_b_glu, gmlp_norm_g, gmlp_norm_b, gmlp_w_s, gmlp_b_s,
              mix_norm_g, w_out, ln1_g, ln1_b, router_w, router_b,
              moe_w_gu, moe_b_gu, moe_w_down, moe_b_down, ln2_g, ln2_b):
    b_p, s_p = x_prompt.shape[:2]
    b_s, s_s = x_sample.shape[:2]
    past = cache_mla_latent.shape[2]
    pos_p = jnp.arange(s_p, dtype=jnp.int32)
    pos_s = past + jnp.arange(s_s, dtype=jnp.int32)
    zeros_state = jnp.zeros((b_p, S5_GROUPS, S5_STATE), jnp.float32)
    xp, xs = x_prompt, x_sample
    lat_p, kr_p, sre_p, sim_p = [], [], [], []
    lat_s, kr_s, sre_s, sim_s, gv_s = [], [], [], [], []
    for l in range(DEPTH):
        lp = {'w_in': w_in[l], 'q_a_norm_g': q_a_norm_g[l], 'w_q_b': w_q_b[l],
              'kv_a_norm_g': kv_a_norm_g[l], 'w_kv_b': w_kv_b[l],
              's5_lam_re': s5_lam_re[l], 's5_lam_im': s5_lam_im[l], 's5_log_dt': s5_log_dt[l],
              's5_b_re': s5_b_re[l], 's5_b_im': s5_b_im[l], 's5_c_re': s5_c_re[l], 's5_c_im': s5_c_im[l],
              's5_d': s5_d[l], 's5_w_glu': s5_w_glu[l], 's5_b_glu': s5_b_glu[l],
              'gmlp_norm_g': gmlp_norm_g[l], 'gmlp_norm_b': gmlp_norm_b[l],
              'gmlp_w_s': gmlp_w_s[l], 'gmlp_b_s': gmlp_b_s[l],
              'mix_norm_g': mix_norm_g[l], 'w_out': w_out[l], 'ln1_g': ln1_g[l], 'ln1_b': ln1_b[l],
              'router_w': router_w[l], 'router_b': router_b[l],
              'moe_w_gu': moe_w_gu[l], 'moe_b_gu': moe_b_gu[l],
              'moe_w_down': moe_w_down[l], 'moe_b_down': moe_b_down[l],
              'ln2_g': ln2_g[l], 'ln2_b': ln2_b[l]}
        xp, lt, kr, sr, si, _ = _layer(xp, pos_p, None, None, zeros_state, zeros_state, GMLP_CHUNK, lp)
        lat_p.append(lt)
        kr_p.append(kr)
        sre_p.append(sr)
        sim_p.append(si)
        xs, lt, kr, sr, si, gv = _layer(xs, pos_s, cache_mla_latent[l], cache_mla_krope[l],
                                        state_s5_re[l].astype(jnp.float32), state_s5_im[l].astype(jnp.float32),
                                        s_s, lp)
        lat_s.append(lt)
        kr_s.append(kr)
        sre_s.append(sr)
        sim_s.append(si)
        gv_s.append(gv)
    return (xp, xs,
            jnp.stack(lat_p), jnp.stack(kr_p), jnp.stack(sre_p), jnp.stack(sim_p),
            jnp.stack(lat_s), jnp.stack(kr_s), jnp.stack(sre_s), jnp.stack(sim_s), jnp.stack(gv_s))
```

```python
import functools
import math

import jax
import jax.numpy as jnp
import numpy as np
from jax import lax
from jax.experimental import pallas as pl
from jax.experimental.pallas import tpu as pltpu

F32 = jnp.float32
BF16 = jnp.bfloat16

D_MODEL = 1024
CHUNK = 64
MLA_HEADS = 4
MLA_NOPE = 128
MLA_ROPE = 64
MLA_V = 128
Q_LORA = 256
KV_LORA = 128
MLA_WIDTH = MLA_HEADS * MLA_V
MLA_SCALE = (MLA_NOPE + MLA_ROPE) ** -0.5
ROPE_THETA = 10000.0
S5_GROUP = 16
S5_GROUPS = 16
S5_WIDTH = S5_GROUP * S5_GROUPS
S5_STATE = 64
S5_COLS = S5_GROUPS * S5_STATE
GMLP_HEADS = 4
GMLP_HEAD_DIM = 64
GMLP_WIDTH = GMLP_HEADS * GMLP_HEAD_DIM
GMLP_CHUNK = 128
N_EXPERTS = 32
TOP_K = 4
D_FF = D_MODEL
SWIGLU_LIMIT = 7.0
SWIGLU_ALPHA = 1.702
NORM_EPS = 1e-5

LANES = 128
SUBLANES = 8
TOKEN_BLOCK = 256
ATTN_BLOCK = 256
EXPERT_BLOCK = 256
S5_BLOCK = 256
VMEM_LIMIT = 56 * 1024 * 1024

_C_Q = 0
_C_KV = _C_Q + Q_LORA
_C_KR = _C_KV + KV_LORA
_C_KRS = _C_KR + LANES
_C_S5 = _C_KRS + LANES
_C_G = _C_S5 + S5_WIDTH
IN_EXT = _C_G + 2 * GMLP_WIDTH
HEAD_SLAB = MLA_NOPE + LANES
QK_WIDTH = MLA_HEADS * HEAD_SLAB


def _cparams(sem, vmem=VMEM_LIMIT):
    return pltpu.CompilerParams(dimension_semantics=sem, vmem_limit_bytes=vmem)


def _dot(a, b):
    return jnp.dot(a, b, preferred_element_type=F32)


def _dot_nt(a, b):
    return lax.dot_general(a, b, (((1,), (1,)), ((), ())), preferred_element_type=F32)


def _split_bf16(x):
    hi = x.astype(BF16)
    lo = (x - hi.astype(F32)).astype(BF16)
    return hi, lo


def _rms(x, g):
    return x * lax.rsqrt(jnp.mean(x * x, axis=-1, keepdims=True) + NORM_EPS) * g


def _ln(x, g, b):
    xc = x - jnp.mean(x, axis=-1, keepdims=True)
    var = jnp.mean(xc * xc, axis=-1, keepdims=True)
    return xc * lax.rsqrt(var + NORM_EPS) * g + b


def _pre_kernel(x_ref, win_ref, qg_ref, wq_ref, kvg_ref, wkv_ref, cos_ref, sin_ref,
                gg_ref, gb_ref, mavg_ref, wsp_ref, bsp_ref,
                q_ref, k_ref, v_ref, lat_ref, kr_ref, u_ref, gm_ref, gv_ref):
    xb = x_ref[...].astype(BF16)
    proj = _dot(xb, win_ref[...])
    cos = cos_ref[...]
    sin = sin_ref[...]

    qa = _rms(proj[:, _C_Q:_C_Q + Q_LORA], qg_ref[...]).astype(BF16)
    qq = _dot(qa, wq_ref[...])
    for h in range(MLA_HEADS):
        c0 = h * HEAD_SLAB
        nope = qq[:, c0:c0 + MLA_NOPE]
        rope = (qq[:, c0 + MLA_NOPE:c0 + HEAD_SLAB] * cos
                + qq[:, QK_WIDTH + h * LANES:QK_WIDTH + (h + 1) * LANES] * sin)
        q_ref[:, c0:c0 + MLA_NOPE] = (nope * MLA_SCALE).astype(BF16)
        q_ref[:, c0 + MLA_NOPE:c0 + HEAD_SLAB] = (rope * MLA_SCALE).astype(BF16)

    lat = _rms(proj[:, _C_KV:_C_KV + KV_LORA], kvg_ref[...])
    lat_ref[...] = lat
    kv = _dot(lat.astype(BF16), wkv_ref[...])
    kr = proj[:, _C_KR:_C_KR + LANES] * cos + proj[:, _C_KRS:_C_KRS + LANES] * sin
    kr_ref[...] = kr[:, :MLA_ROPE]
    krb = kr.astype(BF16)
    for h in range(MLA_HEADS):
        c0 = h * HEAD_SLAB
        k_ref[:, c0:c0 + MLA_NOPE] = kv[:, h * 256:h * 256 + MLA_NOPE].astype(BF16)
        k_ref[:, c0 + MLA_NOPE:c0 + HEAD_SLAB] = krb
        v_ref[:, h * MLA_V:(h + 1) * MLA_V] = kv[:, h * 256 + MLA_NOPE:(h + 1) * 256].astype(BF16)

    u_ref[...] = proj[:, _C_S5:_C_S5 + S5_WIDTH]

    z = jax.nn.gelu(proj[:, _C_G:_C_G + 2 * GMLP_WIDTH])
    ug = z[:, :GMLP_WIDTH]
    vg = z[:, GMLP_WIDTH:]
    mavg = mavg_ref[...]
    hi, lo = _split_bf16(vg)
    xc = vg - (_dot(hi, mavg) + _dot(lo, mavg))
    hi, lo = _split_bf16(xc * xc)
    var = _dot(hi, mavg) + _dot(lo, mavg)
    vn = xc * lax.rsqrt(var + NORM_EPS) * gg_ref[...] + gb_ref[...]
    gv_ref[...] = vn
    lane = lax.broadcasted_iota(jnp.int32, (GMLP_CHUNK, GMLP_WIDTH), 1)
    wsp = wsp_ref[0]
    bsp = bsp_ref[0]
    for c in range(x_ref.shape[0] // GMLP_CHUNK):
        r0 = c * GMLP_CHUNK
        vc = vn[r0:r0 + GMLP_CHUNK, :].astype(BF16)
        stack = jnp.concatenate(
            [jnp.where(lane // GMLP_HEAD_DIM == h, vc, jnp.zeros_like(vc))
             for h in range(GMLP_HEADS)], axis=0)
        mix = _dot(wsp, stack) + bsp
        gm_ref[r0:r0 + GMLP_CHUNK, :] = ug[r0:r0 + GMLP_CHUNK, :] * mix


def _pre_call(x, lw, n_prompt_blocks):
    n = x.shape[0]
    nb = n // TOKEN_BLOCK
    tb = TOKEN_BLOCK
    row = lambda w: pl.BlockSpec((tb, w), lambda i: (i, 0))
    full = lambda a: pl.BlockSpec(a.shape, lambda i: (0,) * a.ndim)
    variant = lambda i: (jnp.where(i < n_prompt_blocks, 0, 1), 0, 0)
    in_specs = [row(D_MODEL), full(lw['w_in']), full(lw['q_g']), full(lw['wq']), full(lw['kv_g']),
                full(lw['wkv']), row(LANES), row(LANES), full(lw['g_g']), full(lw['g_b']),
                full(lw['mavg']),
                pl.BlockSpec((1, GMLP_CHUNK, GMLP_HEADS * GMLP_CHUNK), variant),
                pl.BlockSpec((1, GMLP_CHUNK, GMLP_WIDTH), variant)]
    widths = [(QK_WIDTH, BF16), (QK_WIDTH, BF16), (MLA_WIDTH, BF16), (KV_LORA, F32),
              (MLA_ROPE, F32), (S5_WIDTH, F32), (GMLP_WIDTH, F32), (GMLP_WIDTH, F32)]
    return pl.pallas_call(
        _pre_kernel,
        grid=(nb,),
        in_specs=in_specs,
        out_specs=[row(w) for w, _ in widths],
        out_shape=[jax.ShapeDtypeStruct((n, w), dt) for w, dt in widths],
        compiler_params=_cparams(("parallel",)),
        name="pre",
    )(x, lw['w_in'], lw['q_g'], lw['wq'], lw['kv_g'], lw['wkv'], lw['cos'], lw['sin'],
      lw['g_g'], lw['g_b'], lw['mavg'], lw['wsp'], lw['bsp'])


def _attn_kernel(q_ref, k_ref, v_ref, o_ref):
    i = pl.program_id(2)
    bq = q_ref.shape[0]
    q = q_ref[...]

    def step(k, v, carry, mask):
        m, l, acc = carry
        s = _dot_nt(q, k)
        if mask is not None:
            s = jnp.where(mask, s, -jnp.inf)
        m_new = jnp.maximum(m, jnp.max(s, axis=-1, keepdims=True))
        p = jnp.exp(s - m_new)
        alpha = jnp.exp(m - m_new)
        l = alpha * l + jnp.sum(p, axis=-1, keepdims=True)
        acc = alpha * acc + _dot(p.astype(BF16), v)
        return m_new, l, acc

    def body(j, carry):
        r0 = pl.multiple_of(j * bq, bq)
        return step(k_ref[pl.ds(r0, bq), :], v_ref[pl.ds(r0, bq), :], carry, None)

    init = (jnp.full((bq, 1), -jnp.inf, F32), jnp.zeros((bq, 1), F32), jnp.zeros((bq, MLA_V), F32))
    carry = lax.fori_loop(0, i, body, init)
    r0 = pl.multiple_of(i * bq, bq)
    qc = lax.broadcasted_iota(jnp.int32, (bq, bq), 0) // CHUNK
    kc = lax.broadcasted_iota(jnp.int32, (bq, bq), 1) // CHUNK
    m, l, acc = step(k_ref[pl.ds(r0, bq), :], v_ref[pl.ds(r0, bq), :], carry, kc <= qc)
    o_ref[...] = acc / l


def _attn_call(q, k, v, batch, seq):
    bq = ATTN_BLOCK
    nq = seq // bq
    return pl.pallas_call(
        _attn_kernel,
        grid=(batch, MLA_HEADS, nq),
        in_specs=[pl.BlockSpec((bq, HEAD_SLAB), lambda b, h, i: (b * nq + i, h)),
                  pl.BlockSpec((seq, HEAD_SLAB), lambda b, h, i: (b, h)),
                  pl.BlockSpec((seq, MLA_V), lambda b, h, i: (b, h))],
        out_specs=pl.BlockSpec((bq, MLA_V), lambda b, h, i: (b * nq + i, h)),
        out_shape=jax.ShapeDtypeStruct((batch * seq, MLA_WIDTH), F32),
        compiler_params=_cparams(("parallel", "parallel", "arbitrary")),
        name="attn_prompt",
    )(q, k, v)


def _attn_sample_kernel(q_ref, k_ref, v_ref, plat_ref, pkr_ref, wkv_ref, o_ref, *, past):
    t = q_ref.shape[0]
    q = q_ref[...]
    kn = k_ref[...]
    vn = v_ref[...]
    kvp = _dot(plat_ref[0].astype(BF16), wkv_ref[...]).astype(BF16)
    krp = pkr_ref[0].astype(BF16)
    q_chunk = (past + lax.broadcasted_iota(jnp.int32, (t, 1), 0)) // CHUNK
    mask_p = lax.broadcasted_iota(jnp.int32, (t, past), 1) // CHUNK <= q_chunk
    mask_n = (past + lax.broadcasted_iota(jnp.int32, (t, t), 1)) // CHUNK <= q_chunk
    for h in range(MLA_HEADS):
        c0 = h * HEAD_SLAB
        qh = q[:, c0:c0 + HEAD_SLAB]
        s_p = (_dot_nt(qh[:, :MLA_NOPE], kvp[:, h * 256:h * 256 + MLA_NOPE])
               + _dot_nt(qh[:, MLA_NOPE:MLA_NOPE + MLA_ROPE], krp))
        s_n = _dot_nt(qh, kn[:, c0:c0 + HEAD_SLAB])
        s_p = jnp.where(mask_p, s_p, -jnp.inf)
        s_n = jnp.where(mask_n, s_n, -jnp.inf)
        m = jnp.maximum(jnp.max(s_p, axis=-1, keepdims=True), jnp.max(s_n, axis=-1, keepdims=True))
        p_p = jnp.exp(s_p - m)
        p_n = jnp.exp(s_n - m)
        l = jnp.sum(p_p, axis=-1, keepdims=True) + jnp.sum(p_n, axis=-1, keepdims=True)
        o = (_dot(p_p.astype(BF16), kvp[:, h * 256 + MLA_NOPE:(h + 1) * 256])
             + _dot(p_n.astype(BF16), vn[:, h * MLA_V:(h + 1) * MLA_V]))
        o_ref[:, h * MLA_V:(h + 1) * MLA_V] = o / l


def _attn_sample_call(q, k, v, past_lat, past_kr, wkv, row0, batch, t):
    past = past_lat.shape[1]
    blk0 = row0 // t
    return pl.pallas_call(
        functools.partial(_attn_sample_kernel, past=past),
        grid=(batch,),
        in_specs=[pl.BlockSpec((t, QK_WIDTH), lambda b: (blk0 + b, 0)),
                  pl.BlockSpec((t, QK_WIDTH), lambda b: (blk0 + b, 0)),
                  pl.BlockSpec((t, MLA_WIDTH), lambda b: (blk0 + b, 0)),
                  pl.BlockSpec((1, past, KV_LORA), lambda b: (b, 0, 0)),
                  pl.BlockSpec((1, past, MLA_ROPE), lambda b: (b, 0, 0)),
                  pl.BlockSpec(wkv.shape, lambda b: (0, 0))],
        out_specs=pl.BlockSpec((t, MLA_WIDTH), lambda b: (b, 0)),
        out_shape=jax.ShapeDtypeStruct((batch * t, MLA_WIDTH), F32),
        compiler_params=_cparams(("parallel",)),
        name="attn_sample",
    )(q, k, v, past_lat, past_kr, wkv)


def _s5_kernel(u_ref, s0r_ref, s0i_ref, bb_ref, tab_ref, cc_ref, d_ref, wg_ref, bg_ref,
               y_ref, sr_ref, si_ref, st_ref, cr_ref, ci_ref):
    tb = u_ref.shape[0]
    nc = S5_COLS

    @pl.when(pl.program_id(1) == 0)
    def _():
        cr_ref[...] = jnp.broadcast_to(s0r_ref[0], (SUBLANES, nc))
        ci_ref[...] = jnp.broadcast_to(s0i_ref[0], (SUBLANES, nc))

    u = u_ref[...]
    st_ref[...] = _dot(u.astype(BF16), bb_ref[...])

    def tile(r, carry):
        car, cai = carry
        r0 = pl.multiple_of(r * SUBLANES, SUBLANES)
        xr = st_ref[pl.ds(r0, SUBLANES), :nc]
        xi = st_ref[pl.ds(r0, SUBLANES), nc:]
        for si, d in enumerate((1, 2, 4)):
            pr = tab_ref[si, 0]
            pi = tab_ref[si, 1]
            sr = pltpu.roll(xr, d, 0)
            sim = pltpu.roll(xi, d, 0)
            xr, xi = xr + (pr * sr - pi * sim), xi + (pr * sim + pi * sr)
        pr = tab_ref[3, 0]
        pi = tab_ref[3, 1]
        xr, xi = xr + (pr * car - pi * cai), xi + (pr * cai + pi * car)
        st_ref[pl.ds(r0, SUBLANES), :nc] = xr
        st_ref[pl.ds(r0, SUBLANES), nc:] = xi
        return (jnp.broadcast_to(xr[SUBLANES - 1:SUBLANES, :], (SUBLANES, nc)),
                jnp.broadcast_to(xi[SUBLANES - 1:SUBLANES, :], (SUBLANES, nc)))

    car, cai = lax.fori_loop(0, tb // SUBLANES, tile, (cr_ref[...], ci_ref[...]))
    cr_ref[...] = car
    ci_ref[...] = cai
    sr_ref[0] = car[0:1, :]
    si_ref[0] = cai[0:1, :]

    y = _dot(st_ref[...].astype(BF16), cc_ref[...]) + d_ref[...] * u
    z = jax.nn.gelu(y)
    y_ref[...] = z * jax.nn.sigmoid(_dot(z.astype(BF16), wg_ref[...]) + bg_ref[...])


def _s5_call(u, s0r, s0i, lw, row0, batch, seq, tblk, name):
    nt = seq // tblk
    blk0 = row0 // tblk
    full = lambda a: pl.BlockSpec(a.shape, lambda b, t: (0,) * a.ndim)
    st_spec = pl.BlockSpec((1, 1, S5_COLS), lambda b, t: (b, 0, 0))
    return pl.pallas_call(
        _s5_kernel,
        grid=(batch, nt),
        in_specs=[pl.BlockSpec((tblk, S5_WIDTH), lambda b, t: (blk0 + b * nt + t, 0)),
                  st_spec, st_spec,
                  full(lw['s5_bb']), full(lw['s5_tab']), full(lw['s5_cc']), full(lw['s5_d']),
                  full(lw['s5_wg']), full(lw['s5_bg'])],
        out_specs=[pl.BlockSpec((tblk, S5_WIDTH), lambda b, t: (b * nt + t, 0)), st_spec, st_spec],
        out_shape=[jax.ShapeDtypeStruct((batch * seq, S5_WIDTH), F32),
                   jax.ShapeDtypeStruct((batch, 1, S5_COLS), F32),
                   jax.ShapeDtypeStruct((batch, 1, S5_COLS), F32)],
        scratch_shapes=[pltpu.VMEM((tblk, 2 * S5_COLS), F32),
                        pltpu.VMEM((SUBLANES, S5_COLS), F32),
                        pltpu.VMEM((SUBLANES, S5_COLS), F32)],
        compiler_params=_cparams(("parallel", "arbitrary")),
        name=name,
    )(u, s0r, s0i, lw['s5_bb'], lw['s5_tab'], lw['s5_cc'], lw['s5_d'], lw['s5_wg'], lw['s5_bg'])


def _post_kernel(a_ref, s_ref, g_ref, x_ref, mg_ref, wo_ref, l1g_ref, l1b_ref,
                 rwh_ref, rwl_ref, rb_ref, tri_ref,
                 x1_ref, info_ref, cnt_ref, run_ref, *, alpha):
    tb = x_ref.shape[0]

    @pl.when(pl.program_id(0) == 0)
    def _():
        run_ref[...] = jnp.zeros_like(run_ref)

    mg = mg_ref[...]
    na = _rms(a_ref[...], mg[:, :MLA_WIDTH]).astype(BF16)
    ns = _rms(s_ref[...], mg[:, MLA_WIDTH:MLA_WIDTH + S5_WIDTH]).astype(BF16)
    ng = _rms(g_ref[...], mg[:, MLA_WIDTH + S5_WIDTH:]).astype(BF16)
    mixed = (_dot(na, wo_ref[:MLA_WIDTH, :]) + _dot(ns, wo_ref[MLA_WIDTH:MLA_WIDTH + S5_WIDTH, :])
             + _dot(ng, wo_ref[MLA_WIDTH + S5_WIDTH:, :]))
    x1 = _ln(alpha * x_ref[...] + mixed, l1g_ref[...], l1b_ref[...])
    x1_ref[...] = x1

    hi, lo = _split_bf16(x1)
    rwh = rwh_ref[...]
    logits = _dot(hi, rwh) + (_dot(lo, rwh) + _dot(hi, rwl_ref[...])) + rb_ref[...]

    eidx = lax.broadcasted_iota(jnp.int32, (tb, N_EXPERTS), 1).astype(F32)
    work = logits
    tops, sels, hots = [], [], []
    for _ in range(TOP_K):
        m = jnp.max(work, axis=-1, keepdims=True)
        sel = jnp.min(jnp.where(work == m, eidx, float(N_EXPERTS)), axis=-1, keepdims=True)
        hot = eidx == sel
        tops.append(m)
        sels.append(sel)
        hots.append(hot)
        work = jnp.where(hot, -jnp.inf, work)
    exps = [jnp.exp(t - tops[0]) for t in tops]
    den = exps[0] + exps[1] + exps[2] + exps[3]

    cnt = jnp.zeros((tb, N_EXPERTS), F32)
    for hot in hots:
        cnt = cnt + hot.astype(F32)
    before = _dot(tri_ref[...], cnt.astype(BF16)) + run_ref[...]
    lane = lax.broadcasted_iota(jnp.int32, (tb, LANES), 1)
    info = jnp.zeros((tb, LANES), F32)
    for k in range(TOP_K):
        rank = jnp.sum(jnp.where(hots[k], before, 0.0), axis=-1, keepdims=True)
        info = jnp.where(lane == k, exps[k] / den, info)
        info = jnp.where(lane == TOP_K + k, sels[k], info)
        info = jnp.where(lane == 2 * TOP_K + k, rank, info)
    info_ref[...] = info
    run = run_ref[...] + jnp.sum(cnt, axis=0, keepdims=True)
    run_ref[...] = run
    cnt_ref[...] = run


def _post_call(attn, ssm, gm, x, lw, alpha):
    n = x.shape[0]
    tb = TOKEN_BLOCK
    row = lambda w: pl.BlockSpec((tb, w), lambda i: (i, 0))
    full = lambda a: pl.BlockSpec(a.shape, lambda i: (0,) * a.ndim)
    names = ['mix_g', 'w_out', 'ln1_g', 'ln1_b', 'rw_hi', 'rw_lo', 'router_b', 'tri']
    return pl.pallas_call(
        functools.partial(_post_kernel, alpha=alpha),
        grid=(n // tb,),
        in_specs=[row(MLA_WIDTH), row(S5_WIDTH), row(GMLP_WIDTH), row(D_MODEL)]
                 + [full(lw[k]) for k in names],
        out_specs=[row(D_MODEL), row(LANES), pl.BlockSpec((1, N_EXPERTS), lambda i: (0, 0))],
        out_shape=[jax.ShapeDtypeStruct((n, D_MODEL), F32),
                   jax.ShapeDtypeStruct((n, LANES), F32),
                   jax.ShapeDtypeStruct((1, N_EXPERTS), F32)],
        scratch_shapes=[pltpu.VMEM((1, N_EXPERTS), F32)],
        compiler_params=_cparams(("arbitrary",)),
        name="post",
    )(attn, ssm, gm, x, *[lw[k] for k in names])


_DMA_UNROLL = 8


def _dispatch_kernel(dest_ref, x_ref, zero_ref, xb_ref, sem):
    del zero_ref
    tb = x_ref.shape[0]

    def row_copy(r, k):
        return pltpu.make_async_copy(x_ref.at[pl.ds(r, 1), :],
                                     xb_ref.at[pl.ds(dest_ref[r * TOP_K + k], 1), :], sem)

    def issue(c, _):
        for j in range(_DMA_UNROLL):
            for k in range(TOP_K):
                row_copy(c * _DMA_UNROLL + j, k).start()
        return 0

    def drain(c, _):
        for j in range(_DMA_UNROLL):
            for k in range(TOP_K):
                row_copy(c * _DMA_UNROLL + j, k).wait()
        return 0

    lax.fori_loop(0, tb // _DMA_UNROLL, issue, 0)
    lax.fori_loop(0, tb // _DMA_UNROLL, drain, 0)


def _dispatch_call(dest_flat, x1, n_rows):
    n = x1.shape[0]
    tb = TOKEN_BLOCK
    zeros = jnp.zeros((n_rows, D_MODEL), F32)
    return pl.pallas_call(
        _dispatch_kernel,
        grid=(n // tb,),
        in_specs=[pl.BlockSpec((tb * TOP_K,), lambda i: (i,), memory_space=pltpu.SMEM),
                  pl.BlockSpec((tb, D_MODEL), lambda i: (i, 0)),
                  pl.BlockSpec(memory_space=pl.ANY)],
        out_specs=pl.BlockSpec(memory_space=pl.ANY),
        out_shape=jax.ShapeDtypeStruct((n_rows, D_MODEL), F32),
        scratch_shapes=[pltpu.SemaphoreType.DMA],
        input_output_aliases={2: 0},
        compiler_params=_cparams(("arbitrary",)),
        name="dispatch",
    )(dest_flat, x1, zeros)


def _expert_kernel(be_ref, nu_ref, x_ref, wgu_ref, bgu_ref, wdn_ref, bdn_ref, y_ref,
                   wgu_bf, wdn_bf):
    i = pl.program_id(0)
    changed = jnp.logical_or(i == 0, be_ref[jnp.maximum(i - 1, 0)] != be_ref[i])

    @pl.when(changed)
    def _():
        wgu_bf[...] = wgu_ref[0].astype(BF16)
        wdn_bf[...] = wdn_ref[0].astype(BF16)

    @pl.when(i < nu_ref[0])
    def _():
        gu = _dot(x_ref[...].astype(BF16), wgu_bf[...]) + bgu_ref[0]
        gate = jnp.minimum(gu[:, :D_FF], SWIGLU_LIMIT)
        up = jnp.clip(gu[:, D_FF:], -SWIGLU_LIMIT, SWIGLU_LIMIT)
        h = (up + 1.0) * (gate * jax.nn.sigmoid(SWIGLU_ALPHA * gate))
        y_ref[...] = _dot(h.astype(BF16), wdn_bf[...]) + bdn_ref[0]

    @pl.when(i >= nu_ref[0])
    def _():
        y_ref[...] = jnp.zeros_like(y_ref)


def _expert_call(blk_e, n_used, xb, wgu, bgu, wdn, bdn):
    n_rows = xb.shape[0]
    eb = EXPERT_BLOCK
    grid_spec = pltpu.PrefetchScalarGridSpec(
        num_scalar_prefetch=2,
        grid=(n_rows // eb,),
        in_specs=[pl.BlockSpec((eb, D_MODEL), lambda i, be, nu: (i, 0)),
                  pl.BlockSpec((1, D_MODEL, 2 * D_FF), lambda i, be, nu: (be[i], 0, 0)),
                  pl.BlockSpec((1, 1, 2 * D_FF), lambda i, be, nu: (be[i], 0, 0)),
                  pl.BlockSpec((1, D_FF, D_MODEL), lambda i, be, nu: (be[i], 0, 0)),
                  pl.BlockSpec((1, 1, D_MODEL), lambda i, be, nu: (be[i], 0, 0))],
        out_specs=pl.BlockSpec((eb, D_MODEL), lambda i, be, nu: (i, 0)),
        scratch_shapes=[pltpu.VMEM((D_MODEL, 2 * D_FF), BF16), pltpu.VMEM((D_FF, D_MODEL), BF16)],
    )
    return pl.pallas_call(
        _expert_kernel,
        grid_spec=grid_spec,
        out_shape=jax.ShapeDtypeStruct((n_rows, D_MODEL), F32),
        compiler_params=_cparams(("arbitrary",)),
        name="experts",
    )(blk_e, n_used, xb, wgu, bgu, wdn, bdn)


def _combine_kernel(dest_ref, info_ref, x1_ref, g_ref, b_ref, yb_ref, o_ref, buf, sem, *, alpha):
    tb = x1_ref.shape[0]

    def row_copy(r, k):
        return pltpu.make_async_copy(yb_ref.at[pl.ds(dest_ref[r * TOP_K + k], 1), :],
                                     buf.at[k, pl.ds(r, 1), :], sem)

    def issue(c, _):
        for j in range(_DMA_UNROLL):
            for k in range(TOP_K):
                row_copy(c * _DMA_UNROLL + j, k).start()
        return 0

    def drain(c, _):
        for j in range(_DMA_UNROLL):
            for k in range(TOP_K):
                row_copy(c * _DMA_UNROLL + j, k).wait()
        return 0

    lax.fori_loop(0, tb // _DMA_UNROLL, issue, 0)
    lax.fori_loop(0, tb // _DMA_UNROLL, drain, 0)
    info = info_ref[...]
    moe = info[:, 0:1] * buf[0]
    for k in range(1, TOP_K):
        moe = moe + info[:, k:k + 1] * buf[k]
    o_ref[...] = _ln(alpha * x1_ref[...] + moe, g_ref[...], b_ref[...])


def _combine_call(dest_flat, info, x1, ln_g, ln_b, yb, alpha):
    n = x1.shape[0]
    tb = TOKEN_BLOCK
    return pl.pallas_call(
        functools.partial(_combine_kernel, alpha=alpha),
        grid=(n // tb,),
        in_specs=[pl.BlockSpec((tb * TOP_K,), lambda i: (i,), memory_space=pltpu.SMEM),
                  pl.BlockSpec((tb, LANES), lambda i: (i, 0)),
                  pl.BlockSpec((tb, D_MODEL), lambda i: (i, 0)),
                  pl.BlockSpec((1, D_MODEL), lambda i: (0, 0)),
                  pl.BlockSpec((1, D_MODEL), lambda i: (0, 0)),
                  pl.BlockSpec(memory_space=pl.ANY)],
        out_specs=pl.BlockSpec((tb, D_MODEL), lambda i: (i, 0)),
        out_shape=jax.ShapeDtypeStruct((n, D_MODEL), F32),
        scratch_shapes=[pltpu.VMEM((TOP_K, tb, D_MODEL), F32), pltpu.SemaphoreType.DMA],
        compiler_params=_cparams(("arbitrary",)),
        name="combine",
    )(dest_flat, info, x1, ln_g, ln_b, yb)


def _rope_tables(pos):
    half = MLA_ROPE // 2
    inv_freq = ROPE_THETA ** (-jnp.arange(half, dtype=F32) / half)
    ang = pos.astype(F32)[:, None] * inv_freq[None, :]
    cos, sin = jnp.cos(ang), jnp.sin(ang)
    zero = jnp.zeros((pos.shape[0], LANES - MLA_ROPE), F32)
    return (jnp.concatenate([cos, cos, zero], axis=1), jnp.concatenate([-sin, sin, zero], axis=1))


def _s5_params(lam_re, lam_im, log_dt, b_re, b_im, c_re, c_im):
    dt = jnp.exp(log_dt)[:, None]
    mag = jnp.exp(lam_re * dt)
    ab_re = mag * jnp.cos(lam_im * dt)
    ab_im = mag * jnp.sin(lam_im * dt)
    den = lam_re * lam_re + lam_im * lam_im
    nr, ni = ab_re - 1.0, ab_im
    f_re = (nr * lam_re + ni * lam_im) / den
    f_im = (ni * lam_re - nr * lam_im) / den
    bb_re = f_re[..., None] * b_re - f_im[..., None] * b_im
    bb_im = f_re[..., None] * b_im + f_im[..., None] * b_re
    eye = jnp.eye(S5_GROUPS, dtype=F32)

    def in_blockdiag(w):
        return jnp.einsum('gnc,gh->gchn', w, eye).reshape(S5_WIDTH, S5_COLS)

    def out_blockdiag(w):
        return jnp.einsum('gcn,gh->gnhc', w, eye).reshape(S5_COLS, S5_WIDTH)

    bb = jnp.concatenate([in_blockdiag(bb_re), in_blockdiag(bb_im)], axis=1).astype(BF16)
    cc = jnp.concatenate([out_blockdiag(c_re), -out_blockdiag(c_im)], axis=0).astype(BF16)
    ar, ai = ab_re.reshape(1, S5_COLS), ab_im.reshape(1, S5_COLS)
    pr, pi = [ar], [ai]
    for _ in range(SUBLANES - 1):
        pr, pi = pr + [pr[-1] * ar - pi[-1] * ai], pi + [pr[-1] * ai + pi[-1] * ar]
    rows = jnp.arange(SUBLANES)[:, None]
    tabs = []
    for d in (1, 2, 4):
        keep = rows >= d
        tabs.append(jnp.stack([jnp.where(keep, pr[d - 1], 0.0), jnp.where(keep, pi[d - 1], 0.0)]))
    tabs.append(jnp.stack([jnp.concatenate(pr, axis=0), jnp.concatenate(pi, axis=0)]))
    return bb, cc, jnp.stack(tabs)


def _gmlp_spatial(w_s, b_s, chunk_len):
    reps = GMLP_CHUNK // chunk_len
    i = jnp.arange(chunk_len)
    mask = (i[None, :] // CHUNK) <= (i[:, None] // CHUNK)
    w = jnp.where(mask[None], w_s[:, :chunk_len, :chunk_len], 0.0)
    eye = jnp.eye(reps, dtype=F32)
    wblk = jnp.einsum('hij,rs->hrisj', w, eye).reshape(GMLP_HEADS, GMLP_CHUNK, GMLP_CHUNK)
    wcat = jnp.transpose(wblk, (1, 0, 2)).reshape(GMLP_CHUNK, GMLP_HEADS * GMLP_CHUNK)
    bias = jnp.tile(b_s[:, :chunk_len], (1, reps))
    bias = jnp.repeat(bias.T, GMLP_HEAD_DIM, axis=1)
    return wcat.astype(BF16), bias


def _layer_weights(p, l, cos, sin, dec_seq):
    lw = {'cos': cos, 'sin': sin}
    w_in = p['w_in'][l]
    kr = w_in[:, _C_KR:_C_KR + MLA_ROPE]
    half = MLA_ROPE // 2
    zero = jnp.zeros((D_MODEL, LANES - MLA_ROPE), F32)
    rest = w_in[:, _C_KR + MLA_ROPE:]
    lw['w_in'] = jnp.concatenate(
        [w_in[:, :_C_KR], kr, zero, kr[:, half:], kr[:, :half], zero, rest], axis=1).astype(BF16)
    wq = p['w_q_b'][l].reshape(Q_LORA, MLA_HEADS, MLA_NOPE + MLA_ROPE)
    zq = jnp.zeros((Q_LORA, MLA_HEADS, LANES - MLA_ROPE), F32)
    rope = wq[:, :, MLA_NOPE:]
    plain = jnp.concatenate([wq, zq], axis=2).reshape(Q_LORA, QK_WIDTH)
    swapped = jnp.concatenate([rope[:, :, half:], rope[:, :, :half], zq], axis=2)
    lw['wq'] = jnp.concatenate([plain, swapped.reshape(Q_LORA, MLA_HEADS * LANES)], axis=1).astype(BF16)
    lw['q_g'] = p['q_a_norm_g'][l].reshape(1, Q_LORA)
    lw['kv_g'] = p['kv_a_norm_g'][l].reshape(1, KV_LORA)
    lw['wkv'] = p['w_kv_b'][l].astype(BF16)
    lw['s5_bb'], lw['s5_cc'], lw['s5_tab'] = _s5_params(
        p['s5_lam_re'][l], p['s5_lam_im'][l], p['s5_log_dt'][l], p['s5_b_re'][l], p['s5_b_im'][l],
        p['s5_c_re'][l], p['s5_c_im'][l])
    lw['s5_d'] = p['s5_d'][l].reshape(1, S5_WIDTH)
    lw['s5_wg'] = p['s5_w_glu'][l].astype(BF16)
    lw['s5_bg'] = p['s5_b_glu'][l].reshape(1, S5_WIDTH)
    lw['g_g'] = p['gmlp_norm_g'][l].reshape(1, GMLP_WIDTH)
    lw['g_b'] = p['gmlp_norm_b'][l].reshape(1, GMLP_WIDTH)
    grp = jnp.arange(GMLP_WIDTH) // GMLP_HEAD_DIM
    lw['mavg'] = jnp.where(grp[:, None] == grp[None, :], 1.0 / GMLP_HEAD_DIM, 0.0).astype(BF16)
    wp, bp = _gmlp_spatial(p['gmlp_w_s'][l], p['gmlp_b_s'][l], GMLP_CHUNK)
    ws, bs = _gmlp_spatial(p['gmlp_w_s'][l], p['gmlp_b_s'][l], dec_seq)
    lw['wsp'] = jnp.stack([wp, ws])
    lw['bsp'] = jnp.stack([bp, bs])
    lw['mix_g'] = p['mix_norm_g'][l].reshape(1, -1)
    lw['w_out'] = p['w_out'][l].astype(BF16)
    lw['ln1_g'] = p['ln1_g'][l].reshape(1, D_MODEL)
    lw['ln1_b'] = p['ln1_b'][l].reshape(1, D_MODEL)
    rw = p['router_w'][l]
    lw['rw_hi'] = rw.astype(BF16)
    lw['rw_lo'] = (rw - lw['rw_hi'].astype(F32)).astype(BF16)
    lw['router_b'] = p['router_b'][l].reshape(1, N_EXPERTS)
    t = jnp.arange(TOKEN_BLOCK)
    lw['tri'] = (t[None, :] < t[:, None]).astype(BF16)
    lw['ln2_g'] = p['ln2_g'][l].reshape(1, D_MODEL)
    lw['ln2_b'] = p['ln2_b'][l].reshape(1, D_MODEL)
    return lw


def _moe(x1, info, counts, p, l, lw, alpha):
    n = x1.shape[0]
    eb = EXPERT_BLOCK
    top_i = info[:, TOP_K:2 * TOP_K].astype(jnp.int32)
    rank = info[:, 2 * TOP_K:3 * TOP_K].astype(jnp.int32)
    cnt = counts[0].astype(jnp.int32)
    padded = (cnt + eb - 1) // eb * eb
    pad_end = jnp.cumsum(padded)
    pad_start = pad_end - padded
    dest = (pad_start[top_i] + rank).reshape(-1)
    n_blk = -(-(n * TOP_K) // eb) + N_EXPERTS
    n_used = (pad_end[-1] // eb).astype(jnp.int32)
    blk_start = jnp.arange(n_blk, dtype=jnp.int32) * eb
    blk_e = jnp.minimum(jnp.searchsorted(pad_end, blk_start, side='right'), N_EXPERTS - 1)
    last_e = blk_e[jnp.maximum(n_used - 1, 0)]
    blk_e = jnp.where(jnp.arange(n_blk) < n_used, blk_e, last_e).astype(jnp.int32)
    xb = _dispatch_call(dest, x1, n_blk * eb)
    yb = _expert_call(blk_e, n_used.reshape(1), xb,
                      p['moe_w_gu'][l], p['moe_b_gu'][l].reshape(N_EXPERTS, 1, 2 * D_FF),
                      p['moe_w_down'][l], p['moe_b_down'][l].reshape(N_EXPERTS, 1, D_MODEL))
    return _combine_call(dest, info, x1, lw['ln2_g'], lw['ln2_b'], yb, alpha)


def kernel(x_prompt, x_sample, cache_mla_latent, cache_mla_krope, state_s5_re, state_s5_im, w_in, q_a_norm_g, w_q_b, kv_a_norm_g, w_kv_b, s5_lam_re, s5_lam_im, s5_log_dt, s5_b_re, s5_b_im, s5_c_re, s5_c_im, s5_d, s5_w_glu, s5_b_glu, gmlp_norm_g, gmlp_norm_b, gmlp_w_s, gmlp_b_s, mix_norm_g, w_out, ln1_g, ln1_b, router_w, router_b, moe_w_gu, moe_b_gu, moe_w_down, moe_b_down, ln2_g, ln2_b):
    p = dict(w_in=w_in, q_a_norm_g=q_a_norm_g, w_q_b=w_q_b, kv_a_norm_g=kv_a_norm_g, w_kv_b=w_kv_b,
             s5_lam_re=s5_lam_re, s5_lam_im=s5_lam_im, s5_log_dt=s5_log_dt, s5_b_re=s5_b_re,
             s5_b_im=s5_b_im, s5_c_re=s5_c_re, s5_c_im=s5_c_im, s5_d=s5_d, s5_w_glu=s5_w_glu,
             s5_b_glu=s5_b_glu, gmlp_norm_g=gmlp_norm_g, gmlp_norm_b=gmlp_norm_b, gmlp_w_s=gmlp_w_s,
             gmlp_b_s=gmlp_b_s, mix_norm_g=mix_norm_g, w_out=w_out, ln1_g=ln1_g, ln1_b=ln1_b,
             router_w=router_w, router_b=router_b, moe_w_gu=moe_w_gu, moe_b_gu=moe_b_gu,
             moe_w_down=moe_w_down, moe_b_down=moe_b_down, ln2_g=ln2_g, ln2_b=ln2_b)
    depth = w_in.shape[0]
    bp, sp, _ = x_prompt.shape
    bs, ss, _ = x_sample.shape
    past = cache_mla_latent.shape[2]
    n_p, n_s = bp * sp, bs * ss
    assert sp % ATTN_BLOCK == 0 and sp % S5_BLOCK == 0 and sp % TOKEN_BLOCK == 0
    assert n_s % TOKEN_BLOCK == 0 and GMLP_CHUNK % ss == 0 and ss % SUBLANES == 0
    alpha = float((2 * depth) ** 0.25)

    pos = jnp.concatenate([jnp.tile(jnp.arange(sp, dtype=jnp.int32), bp),
                           jnp.tile(past + jnp.arange(ss, dtype=jnp.int32), bs)])
    cos, sin = _rope_tables(pos)
    x = jnp.concatenate([x_prompt.reshape(n_p, D_MODEL), x_sample.reshape(n_s, D_MODEL)], axis=0)
    zero_state = jnp.zeros((bp, 1, S5_COLS), F32)

    outs = {k: [] for k in ('lat_p', 'kr_p', 'sre_p', 'sim_p', 'lat_s', 'kr_s', 'sre_s', 'sim_s', 'gv_s')}
    for l in range(depth):
        lw = _layer_weights(p, l, cos, sin, ss)
        q, k, v, lat, kr, u, gm, gv = _pre_call(x, lw, n_p // TOKEN_BLOCK)
        attn_p = _attn_call(q, k, v, bp, sp)
        attn_s = _attn_sample_call(q, k, v, cache_mla_latent[l], cache_mla_krope[l], lw['wkv'],
                                   n_p, bs, ss)
        ssm_p, sre_p, sim_p = _s5_call(u, zero_state, zero_state, lw, 0, bp, sp, S5_BLOCK, "s5_prompt")
        ssm_s, sre_s, sim_s = _s5_call(u, state_s5_re[l].reshape(bs, 1, S5_COLS).astype(F32),
                                       state_s5_im[l].reshape(bs, 1, S5_COLS).astype(F32),
                                       lw, n_p, bs, ss, ss, "s5_sample")
        attn = jnp.concatenate([attn_p, attn_s], axis=0)
        ssm = jnp.concatenate([ssm_p, ssm_s], axis=0)
        x1, info, counts = _post_call(attn, ssm, gm, x, lw, alpha)
        x = _moe(x1, info, counts, p, l, lw, alpha)

        outs['lat_p'].append(lat[:n_p].reshape(bp, sp, KV_LORA))
        outs['kr_p'].append(kr[:n_p].reshape(bp, sp, MLA_ROPE))
        outs['sre_p'].append(sre_p.reshape(bp, S5_GROUPS, S5_STATE))
        outs['sim_p'].append(sim_p.reshape(bp, S5_GROUPS, S5_STATE))
        outs['lat_s'].append(lat[n_p:].reshape(bs, ss, KV_LORA))
        outs['kr_s'].append(kr[n_p:].reshape(bs, ss, MLA_ROPE))
        outs['sre_s'].append(sre_s.reshape(bs, S5_GROUPS, S5_STATE))
        outs['sim_s'].append(sim_s.reshape(bs, S5_GROUPS, S5_STATE))
        outs['gv_s'].append(gv[n_p:].reshape(bs, ss, GMLP_WIDTH))

    st = lambda name: jnp.stack(outs[name])
    return (x[:n_p].reshape(bp, sp, D_MODEL), x[n_p:].reshape(bs, ss, D_MODEL),
            st('lat_p'), st('kr_p'), st('sre_p'), st('sim_p'),
            st('lat_s'), st('kr_s'), st('sre_s'), st('sim_s'), st('gv_s'))
```

```python
import functools
import math

import jax
import jax.numpy as jnp
import numpy as np
from jax import lax
from jax.experimental import pallas as pl
from jax.experimental.pallas import tpu as pltpu

F32 = jnp.float32
BF16 = jnp.bfloat16

D_MODEL = 1024
CHUNK = 64
MLA_HEADS = 4
MLA_NOPE = 128
MLA_ROPE = 64
MLA_V = 128
Q_LORA = 256
KV_LORA = 128
MLA_WIDTH = MLA_HEADS * MLA_V
MLA_SCALE = (MLA_NOPE + MLA_ROPE) ** -0.5
Q_SCALE = MLA_SCALE * math.log2(math.e)
ROPE_THETA = 10000.0
S5_GROUP = 16
S5_GROUPS = 16
S5_WIDTH = S5_GROUP * S5_GROUPS
S5_STATE = 64
S5_COLS = S5_GROUPS * S5_STATE
GMLP_HEADS = 4
GMLP_HEAD_DIM = 64
GMLP_WIDTH = GMLP_HEADS * GMLP_HEAD_DIM
GMLP_CHUNK = 128
N_EXPERTS = 32
TOP_K = 4
D_FF = D_MODEL
SWIGLU_LIMIT = 7.0
SWIGLU_ALPHA = 1.702
NORM_EPS = 1e-5

LANES = 128
SUBLANES = 8
TOKEN_BLOCK = 256
ATTN_BLOCK = 1024
ATTN_HEADS = 2
EXPERT_BLOCK = 256
S5_BLOCK = 256
SORT_ROWS = TOKEN_BLOCK * TOP_K + N_EXPERTS * SUBLANES
GROUPED_WIDTH = D_MODEL + LANES
VMEM_LIMIT = 56 * 1024 * 1024

_C_Q = 0
_C_KV = _C_Q + Q_LORA
_C_KR = _C_KV + KV_LORA
_C_KRS = _C_KR + LANES
_C_S5 = _C_KRS + LANES
_C_G = _C_S5 + S5_WIDTH
IN_EXT = _C_G + 2 * GMLP_WIDTH
HEAD_SLAB = MLA_NOPE + LANES
QK_WIDTH = MLA_HEADS * HEAD_SLAB


def _cparams(sem, vmem=VMEM_LIMIT):
    return pltpu.CompilerParams(dimension_semantics=sem, vmem_limit_bytes=vmem)


def _dot(a, b):
    return jnp.dot(a, b, preferred_element_type=F32)


def _dot_nt(a, b):
    return lax.dot_general(a, b, (((1,), (1,)), ((), ())), preferred_element_type=F32)


def _split_bf16(x):
    hi = x.astype(BF16)
    lo = (x - hi.astype(F32)).astype(BF16)
    return hi, lo


def _rms(x, g):
    return x * lax.rsqrt(jnp.mean(x * x, axis=-1, keepdims=True) + NORM_EPS) * g


def _ln(x, g, b):
    xc = x - jnp.mean(x, axis=-1, keepdims=True)
    var = jnp.mean(xc * xc, axis=-1, keepdims=True)
    return xc * lax.rsqrt(var + NORM_EPS) * g + b


def _pre_kernel(x_ref, win_ref, qg_ref, wq_ref, kvg_ref, wkv_ref, cos_ref, sin_ref,
                gg_ref, gb_ref, mavg_ref, wsp_ref, bsp_ref,
                q_ref, k_ref, v_ref, lat_ref, kr_ref, u_ref, gm_ref, gv_ref):
    xb = x_ref[...].astype(BF16)
    proj = _dot(xb, win_ref[...])
    cos = cos_ref[...]
    sin = sin_ref[...]

    qa = _rms(proj[:, _C_Q:_C_Q + Q_LORA], qg_ref[...]).astype(BF16)
    qq = _dot(qa, wq_ref[...])
    for h in range(MLA_HEADS):
        c0 = h * HEAD_SLAB
        nope = qq[:, c0:c0 + MLA_NOPE]
        rope = (qq[:, c0 + MLA_NOPE:c0 + HEAD_SLAB] * cos
                + qq[:, QK_WIDTH + h * LANES:QK_WIDTH + (h + 1) * LANES] * sin)
        q_ref[:, c0:c0 + MLA_NOPE] = (nope * Q_SCALE).astype(BF16)
        q_ref[:, c0 + MLA_NOPE:c0 + HEAD_SLAB] = (rope * Q_SCALE).astype(BF16)

    lat = _rms(proj[:, _C_KV:_C_KV + KV_LORA], kvg_ref[...])
    lat_ref[...] = lat
    kv = _dot(lat.astype(BF16), wkv_ref[...])
    kr = proj[:, _C_KR:_C_KR + LANES] * cos + proj[:, _C_KRS:_C_KRS + LANES] * sin
    kr_ref[...] = kr[:, :MLA_ROPE]
    krb = kr.astype(BF16)
    for h in range(MLA_HEADS):
        c0 = h * HEAD_SLAB
        k_ref[:, c0:c0 + MLA_NOPE] = kv[:, h * 256:h * 256 + MLA_NOPE].astype(BF16)
        k_ref[:, c0 + MLA_NOPE:c0 + HEAD_SLAB] = krb
        v_ref[:, h * MLA_V:(h + 1) * MLA_V] = kv[:, h * 256 + MLA_NOPE:(h + 1) * 256].astype(BF16)

    u_ref[...] = proj[:, _C_S5:_C_S5 + S5_WIDTH]

    z = jax.nn.gelu(proj[:, _C_G:_C_G + 2 * GMLP_WIDTH])
    ug = z[:, :GMLP_WIDTH]
    vg = z[:, GMLP_WIDTH:]
    mavg = mavg_ref[...]
    hi, lo = _split_bf16(vg)
    xc = vg - (_dot(hi, mavg) + _dot(lo, mavg))
    hi, lo = _split_bf16(xc * xc)
    var = _dot(hi, mavg) + _dot(lo, mavg)
    vn = xc * lax.rsqrt(var + NORM_EPS) * gg_ref[...] + gb_ref[...]
    gv_ref[...] = vn
    lane = lax.broadcasted_iota(jnp.int32, (GMLP_CHUNK, GMLP_WIDTH), 1)
    wsp = wsp_ref[0]
    bsp = bsp_ref[0]
    for c in range(x_ref.shape[0] // GMLP_CHUNK):
        r0 = c * GMLP_CHUNK
        vc = vn[r0:r0 + GMLP_CHUNK, :].astype(BF16)
        stack = jnp.concatenate(
            [jnp.where(lane // GMLP_HEAD_DIM == h, vc, jnp.zeros_like(vc))
             for h in range(GMLP_HEADS)], axis=0)
        mix = _dot(wsp, stack) + bsp
        gm_ref[r0:r0 + GMLP_CHUNK, :] = ug[r0:r0 + GMLP_CHUNK, :] * mix


def _pre_call(x, lw, n_prompt_blocks):
    n = x.shape[0]
    nb = n // TOKEN_BLOCK
    tb = TOKEN_BLOCK
    row = lambda w: pl.BlockSpec((tb, w), lambda i: (i, 0))
    full = lambda a: pl.BlockSpec(a.shape, lambda i: (0,) * a.ndim)
    variant = lambda i: (jnp.where(i < n_prompt_blocks, 0, 1), 0, 0)
    in_specs = [row(D_MODEL), full(lw['w_in']), full(lw['q_g']), full(lw['wq']), full(lw['kv_g']),
                full(lw['wkv']), row(LANES), row(LANES), full(lw['g_g']), full(lw['g_b']),
                full(lw['mavg']),
                pl.BlockSpec((1, GMLP_CHUNK, GMLP_HEADS * GMLP_CHUNK), variant),
                pl.BlockSpec((1, GMLP_CHUNK, GMLP_WIDTH), variant)]
    widths = [(QK_WIDTH, BF16), (QK_WIDTH, BF16), (MLA_WIDTH, BF16), (KV_LORA, F32),
              (MLA_ROPE, F32), (S5_WIDTH, F32), (GMLP_WIDTH, F32), (GMLP_WIDTH, F32)]
    return pl.pallas_call(
        _pre_kernel,
        grid=(nb,),
        in_specs=in_specs,
        out_specs=[row(w) for w, _ in widths],
        out_shape=[jax.ShapeDtypeStruct((n, w), dt) for w, dt in widths],
        compiler_params=_cparams(("parallel",)),
        name="pre",
    )(x, lw['w_in'], lw['q_g'], lw['wq'], lw['kv_g'], lw['wkv'], lw['cos'], lw['sin'],
      lw['g_g'], lw['g_b'], lw['mavg'], lw['wsp'], lw['bsp'])


def _attn_kernel(q_ref, k_ref, v_ref, o_ref):
    i = pl.program_id(2)
    bq = q_ref.shape[0]
    qs = [q_ref[:, h * HEAD_SLAB:(h + 1) * HEAD_SLAB] for h in range(ATTN_HEADS)]

    def step(h, r0, carry, mask):
        m, l, acc = carry
        s = _dot_nt(qs[h], k_ref[pl.ds(r0, bq), h * HEAD_SLAB:(h + 1) * HEAD_SLAB])
        if mask is not None:
            s = jnp.where(mask, s, -jnp.inf)
        m_new = jnp.maximum(m, jnp.max(s, axis=-1, keepdims=True))
        p = jnp.exp2(s - m_new)
        alpha = jnp.exp2(m - m_new)
        l = alpha * l + jnp.sum(p, axis=-1, keepdims=True)
        acc = alpha * acc + _dot(p.astype(BF16), v_ref[pl.ds(r0, bq), h * MLA_V:(h + 1) * MLA_V])
        return m_new, l, acc

    def body(j, carries):
        r0 = pl.multiple_of(j * bq, bq)
        return tuple(step(h, r0, carries[h], None) for h in range(ATTN_HEADS))

    init = (jnp.full((bq, 1), -jnp.inf, F32), jnp.zeros((bq, 1), F32), jnp.zeros((bq, MLA_V), F32))
    carries = lax.fori_loop(0, i, body, (init,) * ATTN_HEADS)
    r0 = pl.multiple_of(i * bq, bq)
    qc = lax.broadcasted_iota(jnp.int32, (bq, bq), 0) // CHUNK
    kc = lax.broadcasted_iota(jnp.int32, (bq, bq), 1) // CHUNK
    for h in range(ATTN_HEADS):
        m, l, acc = step(h, r0, carries[h], kc <= qc)
        o_ref[:, h * MLA_V:(h + 1) * MLA_V] = acc / l


def _attn_call(q, k, v, batch, seq):
    bq = ATTN_BLOCK
    nq = seq // bq
    nh = ATTN_HEADS
    return pl.pallas_call(
        _attn_kernel,
        grid=(batch, MLA_HEADS // nh, nq),
        in_specs=[pl.BlockSpec((bq, nh * HEAD_SLAB), lambda b, h, i: (b * nq + i, h)),
                  pl.BlockSpec((seq, nh * HEAD_SLAB), lambda b, h, i: (b, h)),
                  pl.BlockSpec((seq, nh * MLA_V), lambda b, h, i: (b, h))],
        out_specs=pl.BlockSpec((bq, nh * MLA_V), lambda b, h, i: (b * nq + i, h)),
        out_shape=jax.ShapeDtypeStruct((batch * seq, MLA_WIDTH), F32),
        compiler_params=_cparams(("parallel", "parallel", "arbitrary")),
        name="attn_prompt",
    )(q, k, v)


def _attn_sample_kernel(q_ref, k_ref, v_ref, plat_ref, pkr_ref, wkv_ref, o_ref, *, past):
    t = q_ref.shape[0]
    q = q_ref[...]
    kn = k_ref[...]
    vn = v_ref[...]
    kvp = _dot(plat_ref[0].astype(BF16), wkv_ref[...]).astype(BF16)
    krp = pkr_ref[0].astype(BF16)
    q_chunk = (past + lax.broadcasted_iota(jnp.int32, (t, 1), 0)) // CHUNK
    mask_p = lax.broadcasted_iota(jnp.int32, (t, past), 1) // CHUNK <= q_chunk
    mask_n = (past + lax.broadcasted_iota(jnp.int32, (t, t), 1)) // CHUNK <= q_chunk
    for h in range(MLA_HEADS):
        c0 = h * HEAD_SLAB
        qh = q[:, c0:c0 + HEAD_SLAB]
        s_p = (_dot_nt(qh[:, :MLA_NOPE], kvp[:, h * 256:h * 256 + MLA_NOPE])
               + _dot_nt(qh[:, MLA_NOPE:MLA_NOPE + MLA_ROPE], krp))
        s_n = _dot_nt(qh, kn[:, c0:c0 + HEAD_SLAB])
        s_p = jnp.where(mask_p, s_p, -jnp.inf)
        s_n = jnp.where(mask_n, s_n, -jnp.inf)
        m = jnp.maximum(jnp.max(s_p, axis=-1, keepdims=True), jnp.max(s_n, axis=-1, keepdims=True))
        p_p = jnp.exp2(s_p - m)
        p_n = jnp.exp2(s_n - m)
        l = jnp.sum(p_p, axis=-1, keepdims=True) + jnp.sum(p_n, axis=-1, keepdims=True)
        o = (_dot(p_p.astype(BF16), kvp[:, h * 256 + MLA_NOPE:(h + 1) * 256])
             + _dot(p_n.astype(BF16), vn[:, h * MLA_V:(h + 1) * MLA_V]))
        o_ref[:, h * MLA_V:(h + 1) * MLA_V] = o / l


def _attn_sample_call(q, k, v, past_lat, past_kr, wkv, row0, batch, t):
    past = past_lat.shape[1]
    blk0 = row0 // t
    return pl.pallas_call(
        functools.partial(_attn_sample_kernel, past=past),
        grid=(batch,),
        in_specs=[pl.BlockSpec((t, QK_WIDTH), lambda b: (blk0 + b, 0)),
                  pl.BlockSpec((t, QK_WIDTH), lambda b: (blk0 + b, 0)),
                  pl.BlockSpec((t, MLA_WIDTH), lambda b: (blk0 + b, 0)),
                  pl.BlockSpec((1, past, KV_LORA), lambda b: (b, 0, 0)),
                  pl.BlockSpec((1, past, MLA_ROPE), lambda b: (b, 0, 0)),
                  pl.BlockSpec(wkv.shape, lambda b: (0, 0))],
        out_specs=pl.BlockSpec((t, MLA_WIDTH), lambda b: (b, 0)),
        out_shape=jax.ShapeDtypeStruct((batch * t, MLA_WIDTH), F32),
        compiler_params=_cparams(("parallel",)),
        name="attn_sample",
    )(q, k, v, past_lat, past_kr, wkv)


def _s5_kernel(u_ref, s0r_ref, s0i_ref, bb_ref, tab_ref, cc_ref, d_ref, wg_ref, bg_ref,
               y_ref, sr_ref, si_ref, st_ref, cr_ref, ci_ref):
    tb = u_ref.shape[0]
    nc = S5_COLS

    @pl.when(pl.program_id(1) == 0)
    def _():
        cr_ref[...] = jnp.broadcast_to(s0r_ref[0], (SUBLANES, nc))
        ci_ref[...] = jnp.broadcast_to(s0i_ref[0], (SUBLANES, nc))

    u = u_ref[...]
    st_ref[...] = _dot(u.astype(BF16), bb_ref[...])

    def tile(r, carry):
        car, cai = carry
        r0 = pl.multiple_of(r * SUBLANES, SUBLANES)
        xr = st_ref[pl.ds(r0, SUBLANES), :nc]
        xi = st_ref[pl.ds(r0, SUBLANES), nc:]
        for si, d in enumerate((1, 2, 4)):
            pr = tab_ref[si, 0]
            pi = tab_ref[si, 1]
            sr = pltpu.roll(xr, d, 0)
            sim = pltpu.roll(xi, d, 0)
            xr, xi = xr + (pr * sr - pi * sim), xi + (pr * sim + pi * sr)
        pr = tab_ref[3, 0]
        pi = tab_ref[3, 1]
        xr, xi = xr + (pr * car - pi * cai), xi + (pr * cai + pi * car)
        st_ref[pl.ds(r0, SUBLANES), :nc] = xr
        st_ref[pl.ds(r0, SUBLANES), nc:] = xi
        return (jnp.broadcast_to(xr[SUBLANES - 1:SUBLANES, :], (SUBLANES, nc)),
                jnp.broadcast_to(xi[SUBLANES - 1:SUBLANES, :], (SUBLANES, nc)))

    car, cai = lax.fori_loop(0, tb // SUBLANES, tile, (cr_ref[...], ci_ref[...]))
    cr_ref[...] = car
    ci_ref[...] = cai
    sr_ref[0] = car[0:1, :]
    si_ref[0] = cai[0:1, :]

    y = _dot(st_ref[...].astype(BF16), cc_ref[...]) + d_ref[...] * u
    z = jax.nn.gelu(y)
    y_ref[...] = z * jax.nn.sigmoid(_dot(z.astype(BF16), wg_ref[...]) + bg_ref[...])


def _s5_call(u, s0r, s0i, lw, row0, batch, seq, tblk, name):
    nt = seq // tblk
    blk0 = row0 // tblk
    full = lambda a: pl.BlockSpec(a.shape, lambda b, t: (0,) * a.ndim)
    st_spec = pl.BlockSpec((1, 1, S5_COLS), lambda b, t: (b, 0, 0))
    return pl.pallas_call(
        _s5_kernel,
        grid=(batch, nt),
        in_specs=[pl.BlockSpec((tblk, S5_WIDTH), lambda b, t: (blk0 + b * nt + t, 0)),
                  st_spec, st_spec,
                  full(lw['s5_bb']), full(lw['s5_tab']), full(lw['s5_cc']), full(lw['s5_d']),
                  full(lw['s5_wg']), full(lw['s5_bg'])],
        out_specs=[pl.BlockSpec((tblk, S5_WIDTH), lambda b, t: (b * nt + t, 0)), st_spec, st_spec],
        out_shape=[jax.ShapeDtypeStruct((batch * seq, S5_WIDTH), F32),
                   jax.ShapeDtypeStruct((batch, 1, S5_COLS), F32),
                   jax.ShapeDtypeStruct((batch, 1, S5_COLS), F32)],
        scratch_shapes=[pltpu.VMEM((tblk, 2 * S5_COLS), F32),
                        pltpu.VMEM((SUBLANES, S5_COLS), F32),
                        pltpu.VMEM((SUBLANES, S5_COLS), F32)],
        compiler_params=_cparams(("parallel", "arbitrary")),
        name=name,
    )(u, s0r, s0i, lw['s5_bb'], lw['s5_tab'], lw['s5_cc'], lw['s5_d'], lw['s5_wg'], lw['s5_bg'])


def _post_kernel(a_ref, s_ref, g_ref, x_ref, mg_ref, wo_ref, l1g_ref, l1b_ref,
                 rwh_ref, rwl_ref, rb_ref, tri_ref, upper_ref,
                 x1_ref, info_ref, cnt_ref, *, alpha):
    tb = x_ref.shape[0]
    mg = mg_ref[...]
    na = _rms(a_ref[...], mg[:, :MLA_WIDTH]).astype(BF16)
    ns = _rms(s_ref[...], mg[:, MLA_WIDTH:MLA_WIDTH + S5_WIDTH]).astype(BF16)
    ng = _rms(g_ref[...], mg[:, MLA_WIDTH + S5_WIDTH:]).astype(BF16)
    mixed = (_dot(na, wo_ref[:MLA_WIDTH, :]) + _dot(ns, wo_ref[MLA_WIDTH:MLA_WIDTH + S5_WIDTH, :])
             + _dot(ng, wo_ref[MLA_WIDTH + S5_WIDTH:, :]))
    x1 = _ln(alpha * x_ref[...] + mixed, l1g_ref[...], l1b_ref[...])
    x1_ref[...] = x1

    hi, lo = _split_bf16(x1)
    rwh = rwh_ref[...]
    logits = _dot(hi, rwh) + (_dot(lo, rwh) + _dot(hi, rwl_ref[...])) + rb_ref[...]

    eidx = lax.broadcasted_iota(jnp.int32, (tb, N_EXPERTS), 1).astype(F32)
    work = logits
    tops, sels, hots = [], [], []
    for _ in range(TOP_K):
        m = jnp.max(work, axis=-1, keepdims=True)
        sel = jnp.min(jnp.where(work == m, eidx, float(N_EXPERTS)), axis=-1, keepdims=True)
        hot = eidx == sel
        tops.append(m)
        sels.append(sel)
        hots.append(hot)
        work = jnp.where(hot, -jnp.inf, work)
    exps = [jnp.exp(t - tops[0]) for t in tops]
    den = exps[0] + exps[1] + exps[2] + exps[3]

    cnt = jnp.zeros((tb, N_EXPERTS), F32)
    for hot in hots:
        cnt = cnt + hot.astype(F32)
    tot = jnp.sum(cnt, axis=0, keepdims=True)
    seg = jnp.ceil(tot * (1.0 / SUBLANES))
    seg8 = jnp.broadcast_to(seg, (SUBLANES, N_EXPERTS)).astype(BF16)
    start = _dot(seg8, upper_ref[...])[0:1, :] * float(SUBLANES)
    before = _dot(tri_ref[...], cnt.astype(BF16)) + start
    lane = lax.broadcasted_iota(jnp.int32, (tb, LANES), 1)
    info = jnp.zeros((tb, LANES), F32)
    for k in range(TOP_K):
        pos = jnp.sum(jnp.where(hots[k], before, 0.0), axis=-1, keepdims=True)
        info = jnp.where(lane == k, exps[k] / den, info)
        info = jnp.where(lane == TOP_K + k, sels[k], info)
        info = jnp.where(lane == 2 * TOP_K + k, pos, info)
    info_ref[...] = info
    cnt_ref[0] = tot


def _post_call(attn, ssm, gm, x, lw, alpha):
    n = x.shape[0]
    tb = TOKEN_BLOCK
    row = lambda w: pl.BlockSpec((tb, w), lambda i: (i, 0))
    full = lambda a: pl.BlockSpec(a.shape, lambda i: (0,) * a.ndim)
    names = ['mix_g', 'w_out', 'ln1_g', 'ln1_b', 'rw_hi', 'rw_lo', 'router_b', 'tri', 'upper']
    return pl.pallas_call(
        functools.partial(_post_kernel, alpha=alpha),
        grid=(n // tb,),
        in_specs=[row(MLA_WIDTH), row(S5_WIDTH), row(GMLP_WIDTH), row(D_MODEL)]
                 + [full(lw[k]) for k in names],
        out_specs=[row(D_MODEL), row(LANES), pl.BlockSpec((1, 1, N_EXPERTS), lambda i: (i, 0, 0))],
        out_shape=[jax.ShapeDtypeStruct((n, D_MODEL), F32),
                   jax.ShapeDtypeStruct((n, LANES), F32),
                   jax.ShapeDtypeStruct((n // tb, 1, N_EXPERTS), F32)],
        compiler_params=_cparams(("parallel",)),
        name="post",
    )(attn, ssm, gm, x, *[lw[k] for k in names])


def _segment_copies(tab_ref, step, src, dst, src_is_local, sem):
    copies = []
    for e in range(N_EXPERTS):
        base = (step * N_EXPERTS + e) * 3
        loc = pl.multiple_of(tab_ref[base], SUBLANES)
        glo = pl.multiple_of(tab_ref[base + 1], SUBLANES)
        n = pl.multiple_of(tab_ref[base + 2], SUBLANES)
        s_at, d_at = (loc, glo) if src_is_local else (glo, loc)
        copies.append((n, pltpu.make_async_copy(src.at[pl.ds(s_at, n), :], dst.at[pl.ds(d_at, n), :], sem)))
    return copies


def _run_segment_copies(copies):
    for n, cp in copies:
        @pl.when(n > 0)
        def _(cp=cp):
            cp.start()
    for n, cp in copies:
        @pl.when(n > 0)
        def _(cp=cp):
            cp.wait()


def _dispatch_kernel(tab_ref, x_ref, pos_ref, gate_ref, zero_ref, xb_ref, sbuf, sem):
    del zero_ref
    tb = x_ref.shape[0]
    rows = lax.broadcasted_iota(jnp.int32, (SORT_ROWS, tb), 0)
    hit = None
    gates = jnp.zeros((SORT_ROWS, tb), F32)
    for k in range(TOP_K):
        eq = rows == pos_ref[0, k:k + 1, :]
        hit = eq if hit is None else jnp.logical_or(hit, eq)
        gates = jnp.where(eq, gate_ref[0, k:k + 1, :], gates)
    perm = jnp.where(hit, 1.0, 0.0).astype(BF16)
    sbuf[:, :D_MODEL] = _dot(perm, x_ref[...].astype(BF16))
    row_gate = jnp.sum(gates, axis=-1, keepdims=True)
    sbuf[:, D_MODEL:] = jnp.broadcast_to(row_gate, (SORT_ROWS, LANES))
    _run_segment_copies(_segment_copies(tab_ref, pl.program_id(0), sbuf, xb_ref, True, sem))


def _dispatch_call(seg_tab, x1, pos_t, gate_t, n_rows):
    n = x1.shape[0]
    tb = TOKEN_BLOCK
    zeros = jnp.zeros((n_rows, GROUPED_WIDTH), F32)
    grid_spec = pltpu.PrefetchScalarGridSpec(
        num_scalar_prefetch=1,
        grid=(n // tb,),
        in_specs=[pl.BlockSpec((tb, D_MODEL), lambda i, t: (i, 0)),
                  pl.BlockSpec((1, TOP_K, tb), lambda i, t: (i, 0, 0)),
                  pl.BlockSpec((1, TOP_K, tb), lambda i, t: (i, 0, 0)),
                  pl.BlockSpec(memory_space=pl.ANY)],
        out_specs=pl.BlockSpec(memory_space=pl.ANY),
        scratch_shapes=[pltpu.VMEM((SORT_ROWS, GROUPED_WIDTH), F32), pltpu.SemaphoreType.DMA],
    )
    return pl.pallas_call(
        _dispatch_kernel,
        grid_spec=grid_spec,
        out_shape=jax.ShapeDtypeStruct((n_rows, GROUPED_WIDTH), F32),
        input_output_aliases={4: 0},
        compiler_params=_cparams(("arbitrary",)),
        name="dispatch",
    )(seg_tab, x1, pos_t, gate_t, zeros)


def _expert_kernel(be_ref, nu_ref, x_ref, wgu_ref, bgu_ref, wdn_ref, bdn_ref, y_ref,
                   wgu_bf, wdn_bf):
    i = pl.program_id(0)
    changed = jnp.logical_or(i == 0, be_ref[jnp.maximum(i - 1, 0)] != be_ref[i])

    @pl.when(changed)
    def _():
        wgu_bf[...] = wgu_ref[0].astype(BF16)
        wdn_bf[...] = wdn_ref[0].astype(BF16)

    @pl.when(i < nu_ref[0])
    def _():
        row_gate = x_ref[:, D_MODEL:D_MODEL + 1]
        gu = _dot(x_ref[:, :D_MODEL].astype(BF16), wgu_bf[...]) + bgu_ref[0]
        gate = jnp.minimum(gu[:, :D_FF], SWIGLU_LIMIT)
        up = jnp.clip(gu[:, D_FF:], -SWIGLU_LIMIT, SWIGLU_LIMIT)
        h = (up + 1.0) * (gate * jax.nn.sigmoid(SWIGLU_ALPHA * gate))
        y_ref[...] = (_dot(h.astype(BF16), wdn_bf[...]) + bdn_ref[0]) * row_gate

    @pl.when(i >= nu_ref[0])
    def _():
        y_ref[...] = jnp.zeros_like(y_ref)


def _expert_call(blk_e, n_used, xb, wgu, bgu, wdn, bdn):
    n_rows = xb.shape[0]
    eb = EXPERT_BLOCK
    grid_spec = pltpu.PrefetchScalarGridSpec(
        num_scalar_prefetch=2,
        grid=(n_rows // eb,),
        in_specs=[pl.BlockSpec((eb, GROUPED_WIDTH), lambda i, be, nu: (i, 0)),
                  pl.BlockSpec((1, D_MODEL, 2 * D_FF), lambda i, be, nu: (be[i], 0, 0)),
                  pl.BlockSpec((1, 1, 2 * D_FF), lambda i, be, nu: (be[i], 0, 0)),
                  pl.BlockSpec((1, D_FF, D_MODEL), lambda i, be, nu: (be[i], 0, 0)),
                  pl.BlockSpec((1, 1, D_MODEL), lambda i, be, nu: (be[i], 0, 0))],
        out_specs=pl.BlockSpec((eb, D_MODEL), lambda i, be, nu: (i, 0)),
        scratch_shapes=[pltpu.VMEM((D_MODEL, 2 * D_FF), BF16), pltpu.VMEM((D_FF, D_MODEL), BF16)],
    )
    return pl.pallas_call(
        _expert_kernel,
        grid_spec=grid_spec,
        out_shape=jax.ShapeDtypeStruct((n_rows, D_MODEL), F32),
        compiler_params=_cparams(("arbitrary",)),
        name="experts",
    )(blk_e, n_used, xb, wgu, bgu, wdn, bdn)


def _combine_kernel(tab_ref, info_ref, x1_ref, g_ref, b_ref, yb_ref, o_ref, buf, sem, *, alpha):
    tb = x1_ref.shape[0]
    tail = tb * TOP_K
    buf[tail:, :] = jnp.zeros((SORT_ROWS - tail, D_MODEL), F32)
    _run_segment_copies(_segment_copies(tab_ref, pl.program_id(0), yb_ref, buf, False, sem))
    info = info_ref[...]
    cols = lax.broadcasted_iota(jnp.int32, (tb, SORT_ROWS), 1).astype(F32)
    hit = None
    for k in range(TOP_K):
        eq = cols == info[:, 2 * TOP_K + k:2 * TOP_K + k + 1]
        hit = eq if hit is None else jnp.logical_or(hit, eq)
    unsort = jnp.where(hit, 1.0, 0.0).astype(BF16)
    hi, lo = _split_bf16(buf[...])
    moe = _dot(unsort, hi) + _dot(unsort, lo)
    o_ref[...] = _ln(alpha * x1_ref[...] + moe, g_ref[...], b_ref[...])


def _combine_call(seg_tab, info, x1, ln_g, ln_b, yb, alpha):
    n = x1.shape[0]
    tb = TOKEN_BLOCK
    grid_spec = pltpu.PrefetchScalarGridSpec(
        num_scalar_prefetch=1,
        grid=(n // tb,),
        in_specs=[pl.BlockSpec((tb, LANES), lambda i, t: (i, 0)),
                  pl.BlockSpec((tb, D_MODEL), lambda i, t: (i, 0)),
                  pl.BlockSpec((1, D_MODEL), lambda i, t: (0, 0)),
                  pl.BlockSpec((1, D_MODEL), lambda i, t: (0, 0)),
                  pl.BlockSpec(memory_space=pl.ANY)],
        out_specs=pl.BlockSpec((tb, D_MODEL), lambda i, t: (i, 0)),
        scratch_shapes=[pltpu.VMEM((SORT_ROWS, D_MODEL), F32), pltpu.SemaphoreType.DMA],
    )
    return pl.pallas_call(
        functools.partial(_combine_kernel, alpha=alpha),
        grid_spec=grid_spec,
        out_shape=jax.ShapeDtypeStruct((n, D_MODEL), F32),
        compiler_params=_cparams(("arbitrary",)),
        name="combine",
    )(seg_tab, info, x1, ln_g, ln_b, yb)


def _rope_tables(pos):
    half = MLA_ROPE // 2
    inv_freq = ROPE_THETA ** (-jnp.arange(half, dtype=F32) / half)
    ang = pos.astype(F32)[:, None] * inv_freq[None, :]
    cos, sin = jnp.cos(ang), jnp.sin(ang)
    zero = jnp.zeros((pos.shape[0], LANES - MLA_ROPE), F32)
    return (jnp.concatenate([cos, cos, zero], axis=1), jnp.concatenate([-sin, sin, zero], axis=1))


def _s5_params(lam_re, lam_im, log_dt, b_re, b_im, c_re, c_im):
    dt = jnp.exp(log_dt)[:, None]
    mag = jnp.exp(lam_re * dt)
    ab_re = mag * jnp.cos(lam_im * dt)
    ab_im = mag * jnp.sin(lam_im * dt)
    den = lam_re * lam_re + lam_im * lam_im
    nr, ni = ab_re - 1.0, ab_im
    f_re = (nr * lam_re + ni * lam_im) / den
    f_im = (ni * lam_re - nr * lam_im) / den
    bb_re = f_re[..., None] * b_re - f_im[..., None] * b_im
    bb_im = f_re[..., None] * b_im + f_im[..., None] * b_re
    eye = jnp.eye(S5_GROUPS, dtype=F32)

    def in_blockdiag(w):
        return jnp.einsum('gnc,gh->gchn', w, eye).reshape(S5_WIDTH, S5_COLS)

    def out_blockdiag(w):
        return jnp.einsum('gcn,gh->gnhc', w, eye).reshape(S5_COLS, S5_WIDTH)

    bb = jnp.concatenate([in_blockdiag(bb_re), in_blockdiag(bb_im)], axis=1).astype(BF16)
    cc = jnp.concatenate([out_blockdiag(c_re), -out_blockdiag(c_im)], axis=0).astype(BF16)
    ar, ai = ab_re.reshape(1, S5_COLS), ab_im.reshape(1, S5_COLS)
    pr, pi = [ar], [ai]
    for _ in range(SUBLANES - 1):
        pr, pi = pr + [pr[-1] * ar - pi[-1] * ai], pi + [pr[-1] * ai + pi[-1] * ar]
    rows = jnp.arange(SUBLANES)[:, None]
    tabs = []
    for d in (1, 2, 4):
        keep = rows >= d
        tabs.append(jnp.stack([jnp.where(keep, pr[d - 1], 0.0), jnp.where(keep, pi[d - 1], 0.0)]))
    tabs.append(jnp.stack([jnp.concatenate(pr, axis=0), jnp.concatenate(pi, axis=0)]))
    return bb, cc, jnp.stack(tabs)


def _gmlp_spatial(w_s, b_s, chunk_len):
    reps = GMLP_CHUNK // chunk_len
    i = jnp.arange(chunk_len)
    mask = (i[None, :] // CHUNK) <= (i[:, None] // CHUNK)
    w = jnp.where(mask[None], w_s[:, :chunk_len, :chunk_len], 0.0)
    eye = jnp.eye(reps, dtype=F32)
    wblk = jnp.einsum('hij,rs->hrisj', w, eye).reshape(GMLP_HEADS, GMLP_CHUNK, GMLP_CHUNK)
    wcat = jnp.transpose(wblk, (1, 0, 2)).reshape(GMLP_CHUNK, GMLP_HEADS * GMLP_CHUNK)
    bias = jnp.tile(b_s[:, :chunk_len], (1, reps))
    bias = jnp.repeat(bias.T, GMLP_HEAD_DIM, axis=1)
    return wcat.astype(BF16), bias


def _layer_weights(p, l, cos, sin, dec_seq):
    lw = {'cos': cos, 'sin': sin}
    w_in = p['w_in'][l]
    kr = w_in[:, _C_KR:_C_KR + MLA_ROPE]
    half = MLA_ROPE // 2
    zero = jnp.zeros((D_MODEL, LANES - MLA_ROPE), F32)
    rest = w_in[:, _C_KR + MLA_ROPE:]
    lw['w_in'] = jnp.concatenate(
        [w_in[:, :_C_KR], kr, zero, kr[:, half:], kr[:, :half], zero, rest], axis=1).astype(BF16)
    wq = p['w_q_b'][l].reshape(Q_LORA, MLA_HEADS, MLA_NOPE + MLA_ROPE)
    zq = jnp.zeros((Q_LORA, MLA_HEADS, LANES - MLA_ROPE), F32)
    rope = wq[:, :, MLA_NOPE:]
    plain = jnp.concatenate([wq, zq], axis=2).reshape(Q_LORA, QK_WIDTH)
    swapped = jnp.concatenate([rope[:, :, half:], rope[:, :, :half], zq], axis=2)
    lw['wq'] = jnp.concatenate([plain, swapped.reshape(Q_LORA, MLA_HEADS * LANES)], axis=1).astype(BF16)
    lw['q_g'] = p['q_a_norm_g'][l].reshape(1, Q_LORA)
    lw['kv_g'] = p['kv_a_norm_g'][l].reshape(1, KV_LORA)
    lw['wkv'] = p['w_kv_b'][l].astype(BF16)
    lw['s5_bb'], lw['s5_cc'], lw['s5_tab'] = _s5_params(
        p['s5_lam_re'][l], p['s5_lam_im'][l], p['s5_log_dt'][l], p['s5_b_re'][l], p['s5_b_im'][l],
        p['s5_c_re'][l], p['s5_c_im'][l])
    lw['s5_d'] = p['s5_d'][l].reshape(1, S5_WIDTH)
    lw['s5_wg'] = p['s5_w_glu'][l].astype(BF16)
    lw['s5_bg'] = p['s5_b_glu'][l].reshape(1, S5_WIDTH)
    lw['g_g'] = p['gmlp_norm_g'][l].reshape(1, GMLP_WIDTH)
    lw['g_b'] = p['gmlp_norm_b'][l].reshape(1, GMLP_WIDTH)
    grp = jnp.arange(GMLP_WIDTH) // GMLP_HEAD_DIM
    lw['mavg'] = jnp.where(grp[:, None] == grp[None, :], 1.0 / GMLP_HEAD_DIM, 0.0).astype(BF16)
    wp, bp = _gmlp_spatial(p['gmlp_w_s'][l], p['gmlp_b_s'][l], GMLP_CHUNK)
    ws, bs = _gmlp_spatial(p['gmlp_w_s'][l], p['gmlp_b_s'][l], dec_seq)
    lw['wsp'] = jnp.stack([wp, ws])
    lw['bsp'] = jnp.stack([bp, bs])
    lw['mix_g'] = p['mix_norm_g'][l].reshape(1, -1)
    lw['w_out'] = p['w_out'][l].astype(BF16)
    lw['ln1_g'] = p['ln1_g'][l].reshape(1, D_MODEL)
    lw['ln1_b'] = p['ln1_b'][l].reshape(1, D_MODEL)
    rw = p['router_w'][l]
    lw['rw_hi'] = rw.astype(BF16)
    lw['rw_lo'] = (rw - lw['rw_hi'].astype(F32)).astype(BF16)
    lw['router_b'] = p['router_b'][l].reshape(1, N_EXPERTS)
    t = jnp.arange(TOKEN_BLOCK)
    lw['tri'] = (t[None, :] < t[:, None]).astype(BF16)
    e = jnp.arange(N_EXPERTS)
    lw['upper'] = (e[:, None] < e[None, :]).astype(BF16)
    lw['ln2_g'] = p['ln2_g'][l].reshape(1, D_MODEL)
    lw['ln2_b'] = p['ln2_b'][l].reshape(1, D_MODEL)
    return lw


def _moe(x1, info, counts, p, l, lw, alpha):
    n = x1.shape[0]
    eb = EXPERT_BLOCK
    tb = TOKEN_BLOCK
    nb = n // tb
    cnt = counts.reshape(nb, N_EXPERTS).astype(jnp.int32)
    seg = (cnt + SUBLANES - 1) // SUBLANES * SUBLANES
    local_start = jnp.cumsum(seg, axis=1) - seg
    run = jnp.cumsum(seg, axis=0) - seg
    total = jnp.sum(seg, axis=0)
    padded = (total + eb - 1) // eb * eb
    pad_end = jnp.cumsum(padded)
    pad_start = pad_end - padded
    seg_tab = jnp.stack([local_start, pad_start[None, :] + run, seg], axis=-1).reshape(-1).astype(jnp.int32)
    n_blk = -(-(n * TOP_K + nb * N_EXPERTS * (SUBLANES - 1)) // eb) + N_EXPERTS
    n_used = (pad_end[-1] // eb).astype(jnp.int32)
    blk_start = jnp.arange(n_blk, dtype=jnp.int32) * eb
    blk_e = jnp.minimum(jnp.sum(blk_start[:, None] >= pad_end[None, :], axis=1), N_EXPERTS - 1)
    last_e = blk_e[jnp.maximum(n_used - 1, 0)]
    blk_e = jnp.where(jnp.arange(n_blk) < n_used, blk_e, last_e).astype(jnp.int32)
    per_block = lambda a: jnp.transpose(a.reshape(nb, tb, TOP_K), (0, 2, 1))
    pos_t = per_block(info[:, 2 * TOP_K:3 * TOP_K].astype(jnp.int32))
    gate_t = per_block(info[:, :TOP_K])
    xb = _dispatch_call(seg_tab, x1, pos_t, gate_t, n_blk * eb)
    depth = p['moe_w_gu'].shape[0]
    yb = _expert_call(blk_e + l * N_EXPERTS, n_used.reshape(1), xb,
                      p['moe_w_gu'].reshape(depth * N_EXPERTS, D_MODEL, 2 * D_FF),
                      p['moe_b_gu'].reshape(depth * N_EXPERTS, 1, 2 * D_FF),
                      p['moe_w_down'].reshape(depth * N_EXPERTS, D_FF, D_MODEL),
                      p['moe_b_down'].reshape(depth * N_EXPERTS, 1, D_MODEL))
    return _combine_call(seg_tab, info, x1, lw['ln2_g'], lw['ln2_b'], yb, alpha)


def kernel(x_prompt, x_sample, cache_mla_latent, cache_mla_krope, state_s5_re, state_s5_im, w_in, q_a_norm_g, w_q_b, kv_a_norm_g, w_kv_b, s5_lam_re, s5_lam_im, s5_log_dt, s5_b_re, s5_b_im, s5_c_re, s5_c_im, s5_d, s5_w_glu, s5_b_glu, gmlp_norm_g, gmlp_norm_b, gmlp_w_s, gmlp_b_s, mix_norm_g, w_out, ln1_g, ln1_b, router_w, router_b, moe_w_gu, moe_b_gu, moe_w_down, moe_b_down, ln2_g, ln2_b):
    p = dict(w_in=w_in, q_a_norm_g=q_a_norm_g, w_q_b=w_q_b, kv_a_norm_g=kv_a_norm_g, w_kv_b=w_kv_b,
             s5_lam_re=s5_lam_re, s5_lam_im=s5_lam_im, s5_log_dt=s5_log_dt, s5_b_re=s5_b_re,
             s5_b_im=s5_b_im, s5_c_re=s5_c_re, s5_c_im=s5_c_im, s5_d=s5_d, s5_w_glu=s5_w_glu,
             s5_b_glu=s5_b_glu, gmlp_norm_g=gmlp_norm_g, gmlp_norm_b=gmlp_norm_b, gmlp_w_s=gmlp_w_s,
             gmlp_b_s=gmlp_b_s, mix_norm_g=mix_norm_g, w_out=w_out, ln1_g=ln1_g, ln1_b=ln1_b,
             router_w=router_w, router_b=router_b, moe_w_gu=moe_w_gu, moe_b_gu=moe_b_gu,
             moe_w_down=moe_w_down, moe_b_down=moe_b_down, ln2_g=ln2_g, ln2_b=ln2_b)
    depth = w_in.shape[0]
    bp, sp, _ = x_prompt.shape
    bs, ss, _ = x_sample.shape
    past = cache_mla_latent.shape[2]
    n_p, n_s = bp * sp, bs * ss
    assert sp % ATTN_BLOCK == 0 and sp % S5_BLOCK == 0 and sp % TOKEN_BLOCK == 0
    assert n_s % TOKEN_BLOCK == 0 and GMLP_CHUNK % ss == 0 and ss % SUBLANES == 0
    alpha = float((2 * depth) ** 0.25)

    pos = jnp.concatenate([jnp.tile(jnp.arange(sp, dtype=jnp.int32), bp),
                           jnp.tile(past + jnp.arange(ss, dtype=jnp.int32), bs)])
    cos, sin = _rope_tables(pos)
    x = jnp.concatenate([x_prompt.reshape(n_p, D_MODEL), x_sample.reshape(n_s, D_MODEL)], axis=0)
    zero_state = jnp.zeros((bp, 1, S5_COLS), F32)

    outs = {k: [] for k in ('lat_p', 'kr_p', 'sre_p', 'sim_p', 'lat_s', 'kr_s', 'sre_s', 'sim_s', 'gv_s')}
    for l in range(depth):
        lw = _layer_weights(p, l, cos, sin, ss)
        q, k, v, lat, kr, u, gm, gv = _pre_call(x, lw, n_p // TOKEN_BLOCK)
        attn_p = _attn_call(q, k, v, bp, sp)
        attn_s = _attn_sample_call(q, k, v, cache_mla_latent[l], cache_mla_krope[l], lw['wkv'],
                                   n_p, bs, ss)
        ssm_p, sre_p, sim_p = _s5_call(u, zero_state, zero_state, lw, 0, bp, sp, S5_BLOCK, "s5_prompt")
        ssm_s, sre_s, sim_s = _s5_call(u, state_s5_re[l].reshape(bs, 1, S5_COLS).astype(F32),
                                       state_s5_im[l].reshape(bs, 1, S5_COLS).astype(F32),
                                       lw, n_p, bs, ss, ss, "s5_sample")
        attn = jnp.concatenate([attn_p, attn_s], axis=0)
        ssm = jnp.concatenate([ssm_p, ssm_s], axis=0)
        x1, info, counts = _post_call(attn, ssm, gm, x, lw, alpha)
        x = _moe(x1, info, counts, p, l, lw, alpha)

        outs['lat_p'].append(lat[:n_p].reshape(bp, sp, KV_LORA))
        outs['kr_p'].append(kr[:n_p].reshape(bp, sp, MLA_ROPE))
        outs['sre_p'].append(sre_p.reshape(bp, S5_GROUPS, S5_STATE))
        outs['sim_p'].append(sim_p.reshape(bp, S5_GROUPS, S5_STATE))
        outs['lat_s'].append(lat[n_p:].reshape(bs, ss, KV_LORA))
        outs['kr_s'].append(kr[n_p:].reshape(bs, ss, MLA_ROPE))
        outs['sre_s'].append(sre_s.reshape(bs, S5_GROUPS, S5_STATE))
        outs['sim_s'].append(sim_s.reshape(bs, S5_GROUPS, S5_STATE))
        outs['gv_s'].append(gv[n_p:].reshape(bs, ss, GMLP_WIDTH))

    st = lambda name: jnp.stack(outs[name])
    return (x[:n_p].reshape(bp, sp, D_MODEL), x[n_p:].reshape(bs, ss, D_MODEL),
            st('lat_p'), st('kr_p'), st('sre_p'), st('sim_p'),
            st('lat_s'), st('kr_s'), st('sre_s'), st('sim_s'), st('gv_s'))
```

```python
import functools
import math

import jax
import jax.numpy as jnp
import numpy as np
from jax import lax
from jax.experimental import pallas as pl
from jax.experimental.pallas import tpu as pltpu

F32 = jnp.float32
BF16 = jnp.bfloat16

D_MODEL = 1024
CHUNK = 64
MLA_HEADS = 4
MLA_NOPE = 128
MLA_ROPE = 64
MLA_V = 128
Q_LORA = 256
KV_LORA = 128
MLA_WIDTH = MLA_HEADS * MLA_V
MLA_SCALE = (MLA_NOPE + MLA_ROPE) ** -0.5
Q_SCALE = MLA_SCALE * math.log2(math.e)
ROPE_THETA = 10000.0
S5_GROUP = 16
S5_GROUPS = 16
S5_WIDTH = S5_GROUP * S5_GROUPS
S5_STATE = 64
S5_COLS = S5_GROUPS * S5_STATE
GMLP_HEADS = 4
GMLP_HEAD_DIM = 64
GMLP_WIDTH = GMLP_HEADS * GMLP_HEAD_DIM
GMLP_CHUNK = 128
N_EXPERTS = 32
TOP_K = 4
D_FF = D_MODEL
SWIGLU_LIMIT = 7.0
SWIGLU_ALPHA = 1.702
NORM_EPS = 1e-5

LANES = 128
SUBLANES = 8
TOKEN_BLOCK = 256
ATTN_BLOCK = 1024
ATTN_HEADS = 2
EXPERT_BLOCK = 256
S5_BLOCK = 256
SORT_ROWS = TOKEN_BLOCK * TOP_K + N_EXPERTS * SUBLANES
GROUPED_WIDTH = D_MODEL + LANES
VMEM_LIMIT = 56 * 1024 * 1024

_C_Q = 0
_C_KV = _C_Q + Q_LORA
_C_KR = _C_KV + KV_LORA
_C_KRS = _C_KR + LANES
_C_S5 = _C_KRS + LANES
_C_G = _C_S5 + S5_WIDTH
IN_EXT = _C_G + 2 * GMLP_WIDTH
HEAD_SLAB = MLA_NOPE + LANES
QK_WIDTH = MLA_HEADS * HEAD_SLAB


def _cparams(sem, vmem=VMEM_LIMIT):
    return pltpu.CompilerParams(dimension_semantics=sem, vmem_limit_bytes=vmem)


def _dot(a, b):
    return jnp.dot(a, b, preferred_element_type=F32)


def _dot_nt(a, b):
    return lax.dot_general(a, b, (((1,), (1,)), ((), ())), preferred_element_type=F32)


def _split_bf16(x):
    hi = x.astype(BF16)
    lo = (x - hi.astype(F32)).astype(BF16)
    return hi, lo


def _rms(x, g):
    return x * lax.rsqrt(jnp.mean(x * x, axis=-1, keepdims=True) + NORM_EPS) * g


def _ln(x, g, b):
    xc = x - jnp.mean(x, axis=-1, keepdims=True)
    var = jnp.mean(xc * xc, axis=-1, keepdims=True)
    return xc * lax.rsqrt(var + NORM_EPS) * g + b


def _pre_kernel(x_ref, win_ref, qg_ref, wq_ref, kvg_ref, wkv_ref, cos_ref, sin_ref,
                gg_ref, gb_ref, mavg_ref, wsp_ref, bsp_ref,
                q_ref, k_ref, v_ref, lat_ref, kr_ref, u_ref, gm_ref, gv_ref):
    xb = x_ref[...].astype(BF16)
    proj = _dot(xb, win_ref[...])
    cos = cos_ref[...]
    sin = sin_ref[...]

    qa = _rms(proj[:, _C_Q:_C_Q + Q_LORA], qg_ref[...]).astype(BF16)
    qq = _dot(qa, wq_ref[...])
    for h in range(MLA_HEADS):
        c0 = h * HEAD_SLAB
        nope = qq[:, c0:c0 + MLA_NOPE]
        rope = (qq[:, c0 + MLA_NOPE:c0 + HEAD_SLAB] * cos
                + qq[:, QK_WIDTH + h * LANES:QK_WIDTH + (h + 1) * LANES] * sin)
        q_ref[:, c0:c0 + MLA_NOPE] = (nope * Q_SCALE).astype(BF16)
        q_ref[:, c0 + MLA_NOPE:c0 + HEAD_SLAB] = (rope * Q_SCALE).astype(BF16)

    lat = _rms(proj[:, _C_KV:_C_KV + KV_LORA], kvg_ref[...])
    lat_ref[...] = lat
    kv = _dot(lat.astype(BF16), wkv_ref[...])
    kr = proj[:, _C_KR:_C_KR + LANES] * cos + proj[:, _C_KRS:_C_KRS + LANES] * sin
    kr_ref[...] = kr[:, :MLA_ROPE]
    krb = kr.astype(BF16)
    for h in range(MLA_HEADS):
        c0 = h * HEAD_SLAB
        k_ref[:, c0:c0 + MLA_NOPE] = kv[:, h * 256:h * 256 + MLA_NOPE].astype(BF16)
        k_ref[:, c0 + MLA_NOPE:c0 + HEAD_SLAB] = krb
        v_ref[:, h * MLA_V:(h + 1) * MLA_V] = kv[:, h * 256 + MLA_NOPE:(h + 1) * 256].astype(BF16)

    u_ref[...] = proj[:, _C_S5:_C_S5 + S5_WIDTH]

    z = jax.nn.gelu(proj[:, _C_G:_C_G + 2 * GMLP_WIDTH])
    ug = z[:, :GMLP_WIDTH]
    vg = z[:, GMLP_WIDTH:]
    mavg = mavg_ref[...]
    hi, lo = _split_bf16(vg)
    xc = vg - (_dot(hi, mavg) + _dot(lo, mavg))
    hi, lo = _split_bf16(xc * xc)
    var = _dot(hi, mavg) + _dot(lo, mavg)
    vn = xc * lax.rsqrt(var + NORM_EPS) * gg_ref[...] + gb_ref[...]
    gv_ref[...] = vn
    lane = lax.broadcasted_iota(jnp.int32, (GMLP_CHUNK, GMLP_WIDTH), 1)
    wsp = wsp_ref[0]
    bsp = bsp_ref[0]
    for c in range(x_ref.shape[0] // GMLP_CHUNK):
        r0 = c * GMLP_CHUNK
        vc = vn[r0:r0 + GMLP_CHUNK, :].astype(BF16)
        stack = jnp.concatenate(
            [jnp.where(lane // GMLP_HEAD_DIM == h, vc, jnp.zeros_like(vc))
             for h in range(GMLP_HEADS)], axis=0)
        mix = _dot(wsp, stack) + bsp
        gm_ref[r0:r0 + GMLP_CHUNK, :] = ug[r0:r0 + GMLP_CHUNK, :] * mix


def _pre_call(x, lw, n_prompt_blocks):
    n = x.shape[0]
    nb = n // TOKEN_BLOCK
    tb = TOKEN_BLOCK
    row = lambda w: pl.BlockSpec((tb, w), lambda i: (i, 0))
    full = lambda a: pl.BlockSpec(a.shape, lambda i: (0,) * a.ndim)
    variant = lambda i: (jnp.where(i < n_prompt_blocks, 0, 1), 0, 0)
    in_specs = [row(D_MODEL), full(lw['w_in']), full(lw['q_g']), full(lw['wq']), full(lw['kv_g']),
                full(lw['wkv']), row(LANES), row(LANES), full(lw['g_g']), full(lw['g_b']),
                full(lw['mavg']),
                pl.BlockSpec((1, GMLP_CHUNK, GMLP_HEADS * GMLP_CHUNK), variant),
                pl.BlockSpec((1, GMLP_CHUNK, GMLP_WIDTH), variant)]
    widths = [(QK_WIDTH, BF16), (QK_WIDTH, BF16), (MLA_WIDTH, BF16), (KV_LORA, F32),
              (MLA_ROPE, F32), (S5_WIDTH, F32), (GMLP_WIDTH, F32), (GMLP_WIDTH, F32)]
    return pl.pallas_call(
        _pre_kernel,
        grid=(nb,),
        in_specs=in_specs,
        out_specs=[row(w) for w, _ in widths],
        out_shape=[jax.ShapeDtypeStruct((n, w), dt) for w, dt in widths],
        compiler_params=_cparams(("parallel",)),
        name="pre",
    )(x, lw['w_in'], lw['q_g'], lw['wq'], lw['kv_g'], lw['wkv'], lw['cos'], lw['sin'],
      lw['g_g'], lw['g_b'], lw['mavg'], lw['wsp'], lw['bsp'])


def _attn_kernel(q_ref, k_ref, v_ref, o_ref):
    i = pl.program_id(2)
    bq = q_ref.shape[0]
    qs = [q_ref[:, h * HEAD_SLAB:(h + 1) * HEAD_SLAB] for h in range(ATTN_HEADS)]

    def step(h, r0, carry, mask):
        m, l, acc = carry
        s = _dot_nt(qs[h], k_ref[pl.ds(r0, bq), h * HEAD_SLAB:(h + 1) * HEAD_SLAB])
        if mask is not None:
            s = jnp.where(mask, s, -jnp.inf)
        m_new = jnp.maximum(m, jnp.max(s, axis=-1, keepdims=True))
        p = jnp.exp2(s - m_new)
        alpha = jnp.exp2(m - m_new)
        l = alpha * l + jnp.sum(p, axis=-1, keepdims=True)
        acc = alpha * acc + _dot(p.astype(BF16), v_ref[pl.ds(r0, bq), h * MLA_V:(h + 1) * MLA_V])
        return m_new, l, acc

    def body(j, carries):
        r0 = pl.multiple_of(j * bq, bq)
        return tuple(step(h, r0, carries[h], None) for h in range(ATTN_HEADS))

    init = (jnp.full((bq, 1), -jnp.inf, F32), jnp.zeros((bq, 1), F32), jnp.zeros((bq, MLA_V), F32))
    carries = lax.fori_loop(0, i, body, (init,) * ATTN_HEADS)
    r0 = pl.multiple_of(i * bq, bq)
    qc = lax.broadcasted_iota(jnp.int32, (bq, bq), 0) // CHUNK
    kc = lax.broadcasted_iota(jnp.int32, (bq, bq), 1) // CHUNK
    for h in range(ATTN_HEADS):
        m, l, acc = step(h, r0, carries[h], kc <= qc)
        o_ref[:, h * MLA_V:(h + 1) * MLA_V] = acc / l


def _attn_call(q, k, v, batch, seq):
    bq = ATTN_BLOCK
    nq = seq // bq
    nh = ATTN_HEADS
    return pl.pallas_call(
        _attn_kernel,
        grid=(batch, MLA_HEADS // nh, nq),
        in_specs=[pl.BlockSpec((bq, nh * HEAD_SLAB), lambda b, h, i: (b * nq + i, h)),
                  pl.BlockSpec((seq, nh * HEAD_SLAB), lambda b, h, i: (b, h)),
                  pl.BlockSpec((seq, nh * MLA_V), lambda b, h, i: (b, h))],
        out_specs=pl.BlockSpec((bq, nh * MLA_V), lambda b, h, i: (b * nq + i, h)),
        out_shape=jax.ShapeDtypeStruct((batch * seq, MLA_WIDTH), F32),
        compiler_params=_cparams(("parallel", "parallel", "arbitrary")),
        name="attn_prompt",
    )(q, k, v)


def _attn_sample_kernel(q_ref, k_ref, v_ref, plat_ref, pkr_ref, wkv_ref, o_ref, *, past):
    t = q_ref.shape[0]
    q = q_ref[...]
    kn = k_ref[...]
    vn = v_ref[...]
    kvp = _dot(plat_ref[0].astype(BF16), wkv_ref[...]).astype(BF16)
    krp = pkr_ref[0].astype(BF16)
    q_chunk = (past + lax.broadcasted_iota(jnp.int32, (t, 1), 0)) // CHUNK
    mask_p = lax.broadcasted_iota(jnp.int32, (t, past), 1) // CHUNK <= q_chunk
    mask_n = (past + lax.broadcasted_iota(jnp.int32, (t, t), 1)) // CHUNK <= q_chunk
    for h in range(MLA_HEADS):
        c0 = h * HEAD_SLAB
        qh = q[:, c0:c0 + HEAD_SLAB]
        s_p = (_dot_nt(qh[:, :MLA_NOPE], kvp[:, h * 256:h * 256 + MLA_NOPE])
               + _dot_nt(qh[:, MLA_NOPE:MLA_NOPE + MLA_ROPE], krp))
        s_n = _dot_nt(qh, kn[:, c0:c0 + HEAD_SLAB])
        s_p = jnp.where(mask_p, s_p, -jnp.inf)
        s_n = jnp.where(mask_n, s_n, -jnp.inf)
        m = jnp.maximum(jnp.max(s_p, axis=-1, keepdims=True), jnp.max(s_n, axis=-1, keepdims=True))
        p_p = jnp.exp2(s_p - m)
        p_n = jnp.exp2(s_n - m)
        l = jnp.sum(p_p, axis=-1, keepdims=True) + jnp.sum(p_n, axis=-1, keepdims=True)
        o = (_dot(p_p.astype(BF16), kvp[:, h * 256 + MLA_NOPE:(h + 1) * 256])
             + _dot(p_n.astype(BF16), vn[:, h * MLA_V:(h + 1) * MLA_V]))
        o_ref[:, h * MLA_V:(h + 1) * MLA_V] = o / l


def _attn_sample_call(q, k, v, past_lat, past_kr, wkv, row0, batch, t):
    past = past_lat.shape[1]
    blk0 = row0 // t
    return pl.pallas_call(
        functools.partial(_attn_sample_kernel, past=past),
        grid=(batch,),
        in_specs=[pl.BlockSpec((t, QK_WIDTH), lambda b: (blk0 + b, 0)),
                  pl.BlockSpec((t, QK_WIDTH), lambda b: (blk0 + b, 0)),
                  pl.BlockSpec((t, MLA_WIDTH), lambda b: (blk0 + b, 0)),
                  pl.BlockSpec((1, past, KV_LORA), lambda b: (b, 0, 0)),
                  pl.BlockSpec((1, past, MLA_ROPE), lambda b: (b, 0, 0)),
                  pl.BlockSpec(wkv.shape, lambda b: (0, 0))],
        out_specs=pl.BlockSpec((t, MLA_WIDTH), lambda b: (b, 0)),
        out_shape=jax.ShapeDtypeStruct((batch * t, MLA_WIDTH), F32),
        compiler_params=_cparams(("parallel",)),
        name="attn_sample",
    )(q, k, v, past_lat, past_kr, wkv)


def _s5_kernel(u_ref, s0r_ref, s0i_ref, bb_ref, tab_ref, cc_ref, d_ref, wg_ref, bg_ref,
               y_ref, sr_ref, si_ref, st_ref, cr_ref, ci_ref):
    tb = u_ref.shape[0]
    nc = S5_COLS

    @pl.when(pl.program_id(1) == 0)
    def _():
        cr_ref[...] = jnp.broadcast_to(s0r_ref[0], (SUBLANES, nc))
        ci_ref[...] = jnp.broadcast_to(s0i_ref[0], (SUBLANES, nc))

    u = u_ref[...]
    st_ref[...] = _dot(u.astype(BF16), bb_ref[...])

    def tile(r, carry):
        car, cai = carry
        r0 = pl.multiple_of(r * SUBLANES, SUBLANES)
        xr = st_ref[pl.ds(r0, SUBLANES), :nc]
        xi = st_ref[pl.ds(r0, SUBLANES), nc:]
        for si, d in enumerate((1, 2, 4)):
            pr = tab_ref[si, 0]
            pi = tab_ref[si, 1]
            sr = pltpu.roll(xr, d, 0)
            sim = pltpu.roll(xi, d, 0)
            xr, xi = xr + (pr * sr - pi * sim), xi + (pr * sim + pi * sr)
        pr = tab_ref[3, 0]
        pi = tab_ref[3, 1]
        xr, xi = xr + (pr * car - pi * cai), xi + (pr * cai + pi * car)
        st_ref[pl.ds(r0, SUBLANES), :nc] = xr
        st_ref[pl.ds(r0, SUBLANES), nc:] = xi
        return (jnp.broadcast_to(xr[SUBLANES - 1:SUBLANES, :], (SUBLANES, nc)),
                jnp.broadcast_to(xi[SUBLANES - 1:SUBLANES, :], (SUBLANES, nc)))

    car, cai = lax.fori_loop(0, tb // SUBLANES, tile, (cr_ref[...], ci_ref[...]))
    cr_ref[...] = car
    ci_ref[...] = cai
    sr_ref[0] = car[0:1, :]
    si_ref[0] = cai[0:1, :]

    y = _dot(st_ref[...].astype(BF16), cc_ref[...]) + d_ref[...] * u
    z = jax.nn.gelu(y)
    y_ref[...] = z * jax.nn.sigmoid(_dot(z.astype(BF16), wg_ref[...]) + bg_ref[...])


def _s5_call(u, s0r, s0i, lw, row0, batch, seq, tblk, name):
    nt = seq // tblk
    blk0 = row0 // tblk
    full = lambda a: pl.BlockSpec(a.shape, lambda b, t: (0,) * a.ndim)
    st_spec = pl.BlockSpec((1, 1, S5_COLS), lambda b, t: (b, 0, 0))
    return pl.pallas_call(
        _s5_kernel,
        grid=(batch, nt),
        in_specs=[pl.BlockSpec((tblk, S5_WIDTH), lambda b, t: (blk0 + b * nt + t, 0)),
                  st_spec, st_spec,
                  full(lw['s5_bb']), full(lw['s5_tab']), full(lw['s5_cc']), full(lw['s5_d']),
                  full(lw['s5_wg']), full(lw['s5_bg'])],
        out_specs=[pl.BlockSpec((tblk, S5_WIDTH), lambda b, t: (b * nt + t, 0)), st_spec, st_spec],
        out_shape=[jax.ShapeDtypeStruct((batch * seq, S5_WIDTH), F32),
                   jax.ShapeDtypeStruct((batch, 1, S5_COLS), F32),
                   jax.ShapeDtypeStruct((batch, 1, S5_COLS), F32)],
        scratch_shapes=[pltpu.VMEM((tblk, 2 * S5_COLS), F32),
                        pltpu.VMEM((SUBLANES, S5_COLS), F32),
                        pltpu.VMEM((SUBLANES, S5_COLS), F32)],
        compiler_params=_cparams(("parallel", "arbitrary")),
        name=name,
    )(u, s0r, s0i, lw['s5_bb'], lw['s5_tab'], lw['s5_cc'], lw['s5_d'], lw['s5_wg'], lw['s5_bg'])


def _post_kernel(ap_ref, as_ref, sp_ref, ss_ref, g_ref, x_ref, mg_ref, wo_ref, l1g_ref, l1b_ref,
                 rwh_ref, rwl_ref, rb_ref, tri_ref, upper_ref,
                 x1_ref, info_ref, cnt_ref, *, alpha, n_prompt_blocks):
    tb = x_ref.shape[0]
    mg = mg_ref[...]
    is_prompt = pl.program_id(0) < n_prompt_blocks
    na = _rms(jnp.where(is_prompt, ap_ref[...], as_ref[...]), mg[:, :MLA_WIDTH]).astype(BF16)
    ns = _rms(jnp.where(is_prompt, sp_ref[...], ss_ref[...]),
              mg[:, MLA_WIDTH:MLA_WIDTH + S5_WIDTH]).astype(BF16)
    ng = _rms(g_ref[...], mg[:, MLA_WIDTH + S5_WIDTH:]).astype(BF16)
    mixed = (_dot(na, wo_ref[:MLA_WIDTH, :]) + _dot(ns, wo_ref[MLA_WIDTH:MLA_WIDTH + S5_WIDTH, :])
             + _dot(ng, wo_ref[MLA_WIDTH + S5_WIDTH:, :]))
    x1 = _ln(alpha * x_ref[...] + mixed, l1g_ref[...], l1b_ref[...])
    x1_ref[...] = x1

    hi, lo = _split_bf16(x1)
    rwh = rwh_ref[...]
    logits = _dot(hi, rwh) + (_dot(lo, rwh) + _dot(hi, rwl_ref[...])) + rb_ref[...]

    eidx = lax.broadcasted_iota(jnp.int32, (tb, N_EXPERTS), 1).astype(F32)
    work = logits
    tops, sels, hots = [], [], []
    for _ in range(TOP_K):
        m = jnp.max(work, axis=-1, keepdims=True)
        sel = jnp.min(jnp.where(work == m, eidx, float(N_EXPERTS)), axis=-1, keepdims=True)
        hot = eidx == sel
        tops.append(m)
        sels.append(sel)
        hots.append(hot)
        work = jnp.where(hot, -jnp.inf, work)
    exps = [jnp.exp(t - tops[0]) for t in tops]
    den = exps[0] + exps[1] + exps[2] + exps[3]

    cnt = jnp.zeros((tb, N_EXPERTS), F32)
    for hot in hots:
        cnt = cnt + hot.astype(F32)
    tot = jnp.sum(cnt, axis=0, keepdims=True)
    seg = jnp.ceil(tot * (1.0 / SUBLANES))
    seg8 = jnp.broadcast_to(seg, (SUBLANES, N_EXPERTS)).astype(BF16)
    start = _dot(seg8, upper_ref[...])[0:1, :] * float(SUBLANES)
    before = _dot(tri_ref[...], cnt.astype(BF16)) + start
    lane = lax.broadcasted_iota(jnp.int32, (tb, LANES), 1)
    info = jnp.zeros((tb, LANES), F32)
    for k in range(TOP_K):
        pos = jnp.sum(jnp.where(hots[k], before, 0.0), axis=-1, keepdims=True)
        info = jnp.where(lane == k, exps[k] / den, info)
        info = jnp.where(lane == TOP_K + k, sels[k], info)
        info = jnp.where(lane == 2 * TOP_K + k, pos, info)
    info_ref[...] = info
    cnt_ref[0] = tot


def _post_call(attn_p, attn_s, ssm_p, ssm_s, gm, x, lw, alpha):
    n = x.shape[0]
    tb = TOKEN_BLOCK
    npb = attn_p.shape[0] // tb
    row = lambda w: pl.BlockSpec((tb, w), lambda i: (i, 0))
    prompt_row = lambda w: pl.BlockSpec((tb, w), lambda i: (jnp.minimum(i, npb - 1), 0))
    sample_row = lambda w: pl.BlockSpec((tb, w), lambda i: (jnp.maximum(i - npb, 0), 0))
    full = lambda a: pl.BlockSpec(a.shape, lambda i: (0,) * a.ndim)
    names = ['mix_g', 'w_out', 'ln1_g', 'ln1_b', 'rw_hi', 'rw_lo', 'router_b', 'tri', 'upper']
    return pl.pallas_call(
        functools.partial(_post_kernel, alpha=alpha, n_prompt_blocks=npb),
        grid=(n // tb,),
        in_specs=[prompt_row(MLA_WIDTH), sample_row(MLA_WIDTH), prompt_row(S5_WIDTH), sample_row(S5_WIDTH),
                  row(GMLP_WIDTH), row(D_MODEL)]
                 + [full(lw[k]) for k in names],
        out_specs=[row(D_MODEL), row(LANES), pl.BlockSpec((1, 1, N_EXPERTS), lambda i: (i, 0, 0))],
        out_shape=[jax.ShapeDtypeStruct((n, D_MODEL), F32),
                   jax.ShapeDtypeStruct((n, LANES), F32),
                   jax.ShapeDtypeStruct((n // tb, 1, N_EXPERTS), F32)],
        compiler_params=_cparams(("parallel",)),
        name="post",
    )(attn_p, attn_s, ssm_p, ssm_s, gm, x, *[lw[k] for k in names])


def _segment_copies(tab_ref, step, src, dst, src_is_local, sem):
    copies = []
    for e in range(N_EXPERTS):
        base = (step * N_EXPERTS + e) * 3
        loc = pl.multiple_of(tab_ref[base], SUBLANES)
        glo = pl.multiple_of(tab_ref[base + 1], SUBLANES)
        n = pl.multiple_of(tab_ref[base + 2], SUBLANES)
        s_at, d_at = (loc, glo) if src_is_local else (glo, loc)
        copies.append((n, pltpu.make_async_copy(src.at[pl.ds(s_at, n), :], dst.at[pl.ds(d_at, n), :], sem)))
    return copies


def _start_copies(copies):
    for n, cp in copies:
        @pl.when(n > 0)
        def _(cp=cp):
            cp.start()


def _wait_copies(copies):
    for n, cp in copies:
        @pl.when(n > 0)
        def _(cp=cp):
            cp.wait()


def _dispatch_kernel(tab_ref, x_ref, pos_ref, gate_ref, xb_ref, sbuf, zbuf, sem):
    step = pl.program_id(0)
    slot = step % 2
    tb = x_ref.shape[0]
    rows = lax.broadcasted_iota(jnp.int32, (SORT_ROWS, tb), 0)
    hit = None
    gates = jnp.zeros((SORT_ROWS, tb), F32)
    for k in range(TOP_K):
        eq = rows == pos_ref[0, k:k + 1, :]
        hit = eq if hit is None else jnp.logical_or(hit, eq)
        gates = jnp.where(eq, gate_ref[0, k:k + 1, :], gates)
    perm = jnp.where(hit, 1.0, 0.0).astype(BF16)
    sbuf[slot, :, :D_MODEL] = _dot(perm, x_ref[...].astype(BF16))
    row_gate = jnp.sum(gates, axis=-1, keepdims=True)
    sbuf[slot, :, D_MODEL:] = jnp.broadcast_to(row_gate, (SORT_ROWS, LANES))

    def copies(s):
        return _segment_copies(tab_ref, s, sbuf.at[s % 2], xb_ref, True, sem.at[s % 2])

    n_steps = pl.num_programs(0)
    fill_base = n_steps * N_EXPERTS * 3
    fill_sem = sem.at[2]

    def pad_fills():
        out = []
        for e in range(N_EXPERTS):
            start = pl.multiple_of(tab_ref[fill_base + 2 * e], SUBLANES)
            n = pl.multiple_of(tab_ref[fill_base + 2 * e + 1], SUBLANES)
            out.append((n, pltpu.make_async_copy(zbuf.at[pl.ds(0, n), :], xb_ref.at[pl.ds(start, n), :], fill_sem)))
        return out

    def block_fill(j):
        row = pl.multiple_of(tab_ref[fill_base + 2 * N_EXPERTS] + j * EXPERT_BLOCK, EXPERT_BLOCK)
        return pltpu.make_async_copy(zbuf, xb_ref.at[pl.ds(row, EXPERT_BLOCK), :], fill_sem)

    n_unused = tab_ref[fill_base + 2 * N_EXPERTS + 1]

    @pl.when(step == 0)
    def _():
        zbuf[...] = jnp.zeros_like(zbuf)
        _start_copies(pad_fills())
        lax.fori_loop(0, n_unused, lambda j, c: (block_fill(j).start(), c)[1], 0)

    @pl.when(step > 0)
    def _():
        _wait_copies(copies(step - 1))

    _start_copies(copies(step))

    @pl.when(step == n_steps - 1)
    def _():
        _wait_copies(copies(step))
        _wait_copies(pad_fills())
        lax.fori_loop(0, n_unused, lambda j, c: (block_fill(j).wait(), c)[1], 0)


def _dispatch_call(seg_tab, x1, pos_t, gate_t, n_rows):
    n = x1.shape[0]
    tb = TOKEN_BLOCK
    grid_spec = pltpu.PrefetchScalarGridSpec(
        num_scalar_prefetch=1,
        grid=(n // tb,),
        in_specs=[pl.BlockSpec((tb, D_MODEL), lambda i, t: (i, 0)),
                  pl.BlockSpec((1, TOP_K, tb), lambda i, t: (i, 0, 0)),
                  pl.BlockSpec((1, TOP_K, tb), lambda i, t: (i, 0, 0))],
        out_specs=pl.BlockSpec(memory_space=pl.ANY),
        scratch_shapes=[pltpu.VMEM((2, SORT_ROWS, GROUPED_WIDTH), F32),
                        pltpu.VMEM((EXPERT_BLOCK, GROUPED_WIDTH), F32),
                        pltpu.SemaphoreType.DMA((3,))],
    )
    return pl.pallas_call(
        _dispatch_kernel,
        grid_spec=grid_spec,
        out_shape=jax.ShapeDtypeStruct((n_rows, GROUPED_WIDTH), F32),
        compiler_params=_cparams(("arbitrary",)),
        name="dispatch",
    )(seg_tab, x1, pos_t, gate_t)


def _expert_kernel(be_ref, nv_ref, x_ref, wgu_ref, bgu_ref, wdn_ref, bdn_ref, y_ref,
                   wgu_bf, wdn_bf):
    i = pl.program_id(0)
    changed = jnp.logical_or(i == 0, be_ref[jnp.maximum(i - 1, 0)] != be_ref[i])

    @pl.when(changed)
    def _():
        wgu_bf[...] = wgu_ref[0].astype(BF16)
        wdn_bf[...] = wdn_ref[0].astype(BF16)

    @pl.when(nv_ref[i] > 0)
    def _():
        live = lax.broadcasted_iota(jnp.int32, (x_ref.shape[0], 1), 0) < nv_ref[i]
        row_gate = jnp.where(live, x_ref[:, D_MODEL:D_MODEL + 1], 0.0)
        xin = jnp.where(live, x_ref[:, :D_MODEL], 0.0).astype(BF16)
        gu = _dot(xin, wgu_bf[...]) + bgu_ref[0]
        gate = jnp.minimum(gu[:, :D_FF], SWIGLU_LIMIT)
        up = jnp.clip(gu[:, D_FF:], -SWIGLU_LIMIT, SWIGLU_LIMIT)
        h = (up + 1.0) * (gate * jax.nn.sigmoid(SWIGLU_ALPHA * gate))
        y_ref[...] = (_dot(h.astype(BF16), wdn_bf[...]) + bdn_ref[0]) * row_gate

    @pl.when(nv_ref[i] == 0)
    def _():
        y_ref[...] = jnp.zeros_like(y_ref)


def _expert_call(blk_e, n_valid, xb, wgu, bgu, wdn, bdn):
    n_rows = xb.shape[0]
    eb = EXPERT_BLOCK
    grid_spec = pltpu.PrefetchScalarGridSpec(
        num_scalar_prefetch=2,
        grid=(n_rows // eb,),
        in_specs=[pl.BlockSpec((eb, GROUPED_WIDTH), lambda i, be, nu: (i, 0)),
                  pl.BlockSpec((1, D_MODEL, 2 * D_FF), lambda i, be, nu: (be[i], 0, 0)),
                  pl.BlockSpec((1, 1, 2 * D_FF), lambda i, be, nu: (be[i], 0, 0)),
                  pl.BlockSpec((1, D_FF, D_MODEL), lambda i, be, nu: (be[i], 0, 0)),
                  pl.BlockSpec((1, 1, D_MODEL), lambda i, be, nu: (be[i], 0, 0))],
        out_specs=pl.BlockSpec((eb, D_MODEL), lambda i, be, nu: (i, 0)),
        scratch_shapes=[pltpu.VMEM((D_MODEL, 2 * D_FF), BF16), pltpu.VMEM((D_FF, D_MODEL), BF16)],
    )
    return pl.pallas_call(
        _expert_kernel,
        grid_spec=grid_spec,
        out_shape=jax.ShapeDtypeStruct((n_rows, D_MODEL), F32),
        compiler_params=_cparams(("arbitrary",)),
        name="experts",
    )(blk_e, n_valid, xb, wgu, bgu, wdn, bdn)


def _combine_kernel(tab_ref, info_ref, x1_ref, g_ref, b_ref, yb_ref, o_ref, buf, sem, *, alpha):
    tb = x1_ref.shape[0]
    step = pl.program_id(0)
    slot = step % 2
    tail = tb * TOP_K

    def fetch(s):
        buf[s % 2, tail:, :] = jnp.zeros((SORT_ROWS - tail, D_MODEL), F32)
        _start_copies(_segment_copies(tab_ref, s, yb_ref, buf.at[s % 2], False, sem.at[s % 2]))

    @pl.when(step == 0)
    def _():
        fetch(step)

    @pl.when(step + 1 < pl.num_programs(0))
    def _():
        fetch(step + 1)

    _wait_copies(_segment_copies(tab_ref, step, yb_ref, buf.at[slot], False, sem.at[slot]))
    info = info_ref[...]
    cols = lax.broadcasted_iota(jnp.int32, (tb, SORT_ROWS), 1).astype(F32)
    hit = None
    for k in range(TOP_K):
        eq = cols == info[:, 2 * TOP_K + k:2 * TOP_K + k + 1]
        hit = eq if hit is None else jnp.logical_or(hit, eq)
    unsort = jnp.where(hit, 1.0, 0.0).astype(BF16)
    hi, lo = _split_bf16(buf[slot])
    moe = _dot(unsort, hi) + _dot(unsort, lo)
    o_ref[...] = _ln(alpha * x1_ref[...] + moe, g_ref[...], b_ref[...])


def _combine_call(seg_tab, info, x1, ln_g, ln_b, yb, alpha):
    n = x1.shape[0]
    tb = TOKEN_BLOCK
    grid_spec = pltpu.PrefetchScalarGridSpec(
        num_scalar_prefetch=1,
        grid=(n // tb,),
        in_specs=[pl.BlockSpec((tb, LANES), lambda i, t: (i, 0)),
                  pl.BlockSpec((tb, D_MODEL), lambda i, t: (i, 0)),
                  pl.BlockSpec((1, D_MODEL), lambda i, t: (0, 0)),
                  pl.BlockSpec((1, D_MODEL), lambda i, t: (0, 0)),
                  pl.BlockSpec(memory_space=pl.ANY)],
        out_specs=pl.BlockSpec((tb, D_MODEL), lambda i, t: (i, 0)),
        scratch_shapes=[pltpu.VMEM((2, SORT_ROWS, D_MODEL), F32), pltpu.SemaphoreType.DMA((2,))],
    )
    return pl.pallas_call(
        functools.partial(_combine_kernel, alpha=alpha),
        grid_spec=grid_spec,
        out_shape=jax.ShapeDtypeStruct((n, D_MODEL), F32),
        compiler_params=_cparams(("arbitrary",)),
        name="combine",
    )(seg_tab, info, x1, ln_g, ln_b, yb)


def _rope_tables(pos):
    half = MLA_ROPE // 2
    inv_freq = ROPE_THETA ** (-jnp.arange(half, dtype=F32) / half)
    ang = pos.astype(F32)[:, None] * inv_freq[None, :]
    cos, sin = jnp.cos(ang), jnp.sin(ang)
    zero = jnp.zeros((pos.shape[0], LANES - MLA_ROPE), F32)
    return (jnp.concatenate([cos, cos, zero], axis=1), jnp.concatenate([-sin, sin, zero], axis=1))


def _s5_params(lam_re, lam_im, log_dt, b_re, b_im, c_re, c_im):
    dt = jnp.exp(log_dt)[:, None]
    mag = jnp.exp(lam_re * dt)
    ab_re = mag * jnp.cos(lam_im * dt)
    ab_im = mag * jnp.sin(lam_im * dt)
    den = lam_re * lam_re + lam_im * lam_im
    nr, ni = ab_re - 1.0, ab_im
    f_re = (nr * lam_re + ni * lam_im) / den
    f_im = (ni * lam_re - nr * lam_im) / den
    bb_re = f_re[..., None] * b_re - f_im[..., None] * b_im
    bb_im = f_re[..., None] * b_im + f_im[..., None] * b_re
    eye = jnp.eye(S5_GROUPS, dtype=F32)

    def in_blockdiag(w):
        return jnp.einsum('gnc,gh->gchn', w, eye).reshape(S5_WIDTH, S5_COLS)

    def out_blockdiag(w):
        return jnp.einsum('gcn,gh->gnhc', w, eye).reshape(S5_COLS, S5_WIDTH)

    bb = jnp.concatenate([in_blockdiag(bb_re), in_blockdiag(bb_im)], axis=1).astype(BF16)
    cc = jnp.concatenate([out_blockdiag(c_re), -out_blockdiag(c_im)], axis=0).astype(BF16)
    ar, ai = ab_re.reshape(1, S5_COLS), ab_im.reshape(1, S5_COLS)
    pr, pi = [ar], [ai]
    for _ in range(SUBLANES - 1):
        pr, pi = pr + [pr[-1] * ar - pi[-1] * ai], pi + [pr[-1] * ai + pi[-1] * ar]
    rows = jnp.arange(SUBLANES)[:, None]
    tabs = []
    for d in (1, 2, 4):
        keep = rows >= d
        tabs.append(jnp.stack([jnp.where(keep, pr[d - 1], 0.0), jnp.where(keep, pi[d - 1], 0.0)]))
    tabs.append(jnp.stack([jnp.concatenate(pr, axis=0), jnp.concatenate(pi, axis=0)]))
    return bb, cc, jnp.stack(tabs)


def _gmlp_spatial(w_s, b_s, chunk_len):
    reps = GMLP_CHUNK // chunk_len
    i = jnp.arange(chunk_len)
    mask = (i[None, :] // CHUNK) <= (i[:, None] // CHUNK)
    w = jnp.where(mask[None], w_s[:, :chunk_len, :chunk_len], 0.0)
    eye = jnp.eye(reps, dtype=F32)
    wblk = jnp.einsum('hij,rs->hrisj', w, eye).reshape(GMLP_HEADS, GMLP_CHUNK, GMLP_CHUNK)
    wcat = jnp.transpose(wblk, (1, 0, 2)).reshape(GMLP_CHUNK, GMLP_HEADS * GMLP_CHUNK)
    bias = jnp.tile(b_s[:, :chunk_len], (1, reps))
    bias = jnp.repeat(bias.T, GMLP_HEAD_DIM, axis=1)
    return wcat.astype(BF16), bias


def _layer_weights(p, l, cos, sin, dec_seq):
    lw = {'cos': cos, 'sin': sin}
    w_in = p['w_in'][l]
    kr = w_in[:, _C_KR:_C_KR + MLA_ROPE]
    half = MLA_ROPE // 2
    zero = jnp.zeros((D_MODEL, LANES - MLA_ROPE), F32)
    rest = w_in[:, _C_KR + MLA_ROPE:]
    lw['w_in'] = jnp.concatenate(
        [w_in[:, :_C_KR], kr, zero, kr[:, half:], kr[:, :half], zero, rest], axis=1).astype(BF16)
    wq = p['w_q_b'][l].reshape(Q_LORA, MLA_HEADS, MLA_NOPE + MLA_ROPE)
    zq = jnp.zeros((Q_LORA, MLA_HEADS, LANES - MLA_ROPE), F32)
    rope = wq[:, :, MLA_NOPE:]
    plain = jnp.concatenate([wq, zq], axis=2).reshape(Q_LORA, QK_WIDTH)
    swapped = jnp.concatenate([rope[:, :, half:], rope[:, :, :half], zq], axis=2)
    lw['wq'] = jnp.concatenate([plain, swapped.reshape(Q_LORA, MLA_HEADS * LANES)], axis=1).astype(BF16)
    lw['q_g'] = p['q_a_norm_g'][l].reshape(1, Q_LORA)
    lw['kv_g'] = p['kv_a_norm_g'][l].reshape(1, KV_LORA)
    lw['wkv'] = p['w_kv_b'][l].astype(BF16)
    lw['s5_bb'], lw['s5_cc'], lw['s5_tab'] = _s5_params(
        p['s5_lam_re'][l], p['s5_lam_im'][l], p['s5_log_dt'][l], p['s5_b_re'][l], p['s5_b_im'][l],
        p['s5_c_re'][l], p['s5_c_im'][l])
    lw['s5_d'] = p['s5_d'][l].reshape(1, S5_WIDTH)
    lw['s5_wg'] = p['s5_w_glu'][l].astype(BF16)
    lw['s5_bg'] = p['s5_b_glu'][l].reshape(1, S5_WIDTH)
    lw['g_g'] = p['gmlp_norm_g'][l].reshape(1, GMLP_WIDTH)
    lw['g_b'] = p['gmlp_norm_b'][l].reshape(1, GMLP_WIDTH)
    grp = jnp.arange(GMLP_WIDTH) // GMLP_HEAD_DIM
    lw['mavg'] = jnp.where(grp[:, None] == grp[None, :], 1.0 / GMLP_HEAD_DIM, 0.0).astype(BF16)
    wp, bp = _gmlp_spatial(p['gmlp_w_s'][l], p['gmlp_b_s'][l], GMLP_CHUNK)
    ws, bs = _gmlp_spatial(p['gmlp_w_s'][l], p['gmlp_b_s'][l], dec_seq)
    lw['wsp'] = jnp.stack([wp, ws])
    lw['bsp'] = jnp.stack([bp, bs])
    lw['mix_g'] = p['mix_norm_g'][l].reshape(1, -1)
    lw['w_out'] = p['w_out'][l].astype(BF16)
    lw['ln1_g'] = p['ln1_g'][l].reshape(1, D_MODEL)
    lw['ln1_b'] = p['ln1_b'][l].reshape(1, D_MODEL)
    rw = p['router_w'][l]
    lw['rw_hi'] = rw.astype(BF16)
    lw['rw_lo'] = (rw - lw['rw_hi'].astype(F32)).astype(BF16)
    lw['router_b'] = p['router_b'][l].reshape(1, N_EXPERTS)
    t = jnp.arange(TOKEN_BLOCK)
    lw['tri'] = (t[None, :] < t[:, None]).astype(BF16)
    e = jnp.arange(N_EXPERTS)
    lw['upper'] = (e[:, None] < e[None, :]).astype(BF16)
    lw['ln2_g'] = p['ln2_g'][l].reshape(1, D_MODEL)
    lw['ln2_b'] = p['ln2_b'][l].reshape(1, D_MODEL)
    return lw


def _moe(x1, info, counts, p, l, lw, alpha):
    n = x1.shape[0]
    eb = EXPERT_BLOCK
    tb = TOKEN_BLOCK
    nb = n // tb
    cnt = counts.reshape(nb, N_EXPERTS).astype(jnp.int32)
    seg = (cnt + SUBLANES - 1) // SUBLANES * SUBLANES
    local_start = jnp.cumsum(seg, axis=1) - seg
    run = jnp.cumsum(seg, axis=0) - seg
    total = jnp.sum(seg, axis=0)
    padded = (total + eb - 1) // eb * eb
    pad_end = jnp.cumsum(padded)
    pad_start = pad_end - padded
    n_blk = -(-(n * TOP_K + nb * N_EXPERTS * (SUBLANES - 1)) // eb) + N_EXPERTS
    n_used = (pad_end[-1] // eb).astype(jnp.int32)
    seg_tab = jnp.concatenate([
        jnp.stack([local_start, pad_start[None, :] + run, seg], axis=-1).reshape(-1),
        jnp.stack([pad_start + total, padded - total], axis=-1).reshape(-1),
        jnp.stack([pad_end[-1], n_blk - n_used])]).astype(jnp.int32)
    blk_start = jnp.arange(n_blk, dtype=jnp.int32) * eb
    blk_e = jnp.minimum(jnp.sum(blk_start[:, None] >= pad_end[None, :], axis=1), N_EXPERTS - 1)
    last_e = blk_e[jnp.maximum(n_used - 1, 0)]
    blk_e = jnp.where(jnp.arange(n_blk) < n_used, blk_e, last_e).astype(jnp.int32)
    filled_end = (pad_start + total)[blk_e]
    n_valid = jnp.where(jnp.arange(n_blk) < n_used, jnp.clip(filled_end - blk_start, 0, eb), 0).astype(jnp.int32)
    per_block = lambda a: jnp.transpose(a.reshape(nb, tb, TOP_K), (0, 2, 1))
    pos_t = per_block(info[:, 2 * TOP_K:3 * TOP_K].astype(jnp.int32))
    gate_t = per_block(info[:, :TOP_K])
    xb = _dispatch_call(seg_tab, x1, pos_t, gate_t, n_blk * eb)
    depth = p['moe_w_gu'].shape[0]
    yb = _expert_call(blk_e + l * N_EXPERTS, n_valid, xb,
                      p['moe_w_gu'].reshape(depth * N_EXPERTS, D_MODEL, 2 * D_FF),
                      p['moe_b_gu'].reshape(depth * N_EXPERTS, 1, 2 * D_FF),
                      p['moe_w_down'].reshape(depth * N_EXPERTS, D_FF, D_MODEL),
                      p['moe_b_down'].reshape(depth * N_EXPERTS, 1, D_MODEL))
    return _combine_call(seg_tab, info, x1, lw['ln2_g'], lw['ln2_b'], yb, alpha)


def kernel(x_prompt, x_sample, cache_mla_latent, cache_mla_krope, state_s5_re, state_s5_im, w_in, q_a_norm_g, w_q_b, kv_a_norm_g, w_kv_b, s5_lam_re, s5_lam_im, s5_log_dt, s5_b_re, s5_b_im, s5_c_re, s5_c_im, s5_d, s5_w_glu, s5_b_glu, gmlp_norm_g, gmlp_norm_b, gmlp_w_s, gmlp_b_s, mix_norm_g, w_out, ln1_g, ln1_b, router_w, router_b, moe_w_gu, moe_b_gu, moe_w_down, moe_b_down, ln2_g, ln2_b):
    p = dict(w_in=w_in, q_a_norm_g=q_a_norm_g, w_q_b=w_q_b, kv_a_norm_g=kv_a_norm_g, w_kv_b=w_kv_b,
             s5_lam_re=s5_lam_re, s5_lam_im=s5_lam_im, s5_log_dt=s5_log_dt, s5_b_re=s5_b_re,
             s5_b_im=s5_b_im, s5_c_re=s5_c_re, s5_c_im=s5_c_im, s5_d=s5_d, s5_w_glu=s5_w_glu,
             s5_b_glu=s5_b_glu, gmlp_norm_g=gmlp_norm_g, gmlp_norm_b=gmlp_norm_b, gmlp_w_s=gmlp_w_s,
             gmlp_b_s=gmlp_b_s, mix_norm_g=mix_norm_g, w_out=w_out, ln1_g=ln1_g, ln1_b=ln1_b,
             router_w=router_w, router_b=router_b, moe_w_gu=moe_w_gu, moe_b_gu=moe_b_gu,
             moe_w_down=moe_w_down, moe_b_down=moe_b_down, ln2_g=ln2_g, ln2_b=ln2_b)
    depth = w_in.shape[0]
    bp, sp, _ = x_prompt.shape
    bs, ss, _ = x_sample.shape
    past = cache_mla_latent.shape[2]
    n_p, n_s = bp * sp, bs * ss
    assert sp % ATTN_BLOCK == 0 and sp % S5_BLOCK == 0 and sp % TOKEN_BLOCK == 0
    assert n_s % TOKEN_BLOCK == 0 and GMLP_CHUNK % ss == 0 and ss % SUBLANES == 0
    alpha = float((2 * depth) ** 0.25)

    pos = jnp.concatenate([jnp.tile(jnp.arange(sp, dtype=jnp.int32), bp),
                           jnp.tile(past + jnp.arange(ss, dtype=jnp.int32), bs)])
    cos, sin = _rope_tables(pos)
    x = jnp.concatenate([x_prompt.reshape(n_p, D_MODEL), x_sample.reshape(n_s, D_MODEL)], axis=0)
    zero_state = jnp.zeros((bp, 1, S5_COLS), F32)

    outs = {k: [] for k in ('lat_p', 'kr_p', 'sre_p', 'sim_p', 'lat_s', 'kr_s', 'sre_s', 'sim_s', 'gv_s')}
    for l in range(depth):
        lw = _layer_weights(p, l, cos, sin, ss)
        q, k, v, lat, kr, u, gm, gv = _pre_call(x, lw, n_p // TOKEN_BLOCK)
        attn_p = _attn_call(q, k, v, bp, sp)
        attn_s = _attn_sample_call(q, k, v, cache_mla_latent[l], cache_mla_krope[l], lw['wkv'],
                                   n_p, bs, ss)
        ssm_p, sre_p, sim_p = _s5_call(u, zero_state, zero_state, lw, 0, bp, sp, S5_BLOCK, "s5_prompt")
        ssm_s, sre_s, sim_s = _s5_call(u, state_s5_re[l].reshape(bs, 1, S5_COLS).astype(F32),
                                       state_s5_im[l].reshape(bs, 1, S5_COLS).astype(F32),
                                       lw, n_p, bs, ss, ss, "s5_sample")
        x1, info, counts = _post_call(attn_p, attn_s, ssm_p, ssm_s, gm, x, lw, alpha)
        x = _moe(x1, info, counts, p, l, lw, alpha)

        outs['lat_p'].append(lat[:n_p].reshape(bp, sp, KV_LORA))
        outs['kr_p'].append(kr[:n_p].reshape(bp, sp, MLA_ROPE))
        outs['sre_p'].append(sre_p.reshape(bp, S5_GROUPS, S5_STATE))
        outs['sim_p'].append(sim_p.reshape(bp, S5_GROUPS, S5_STATE))
        outs['lat_s'].append(lat[n_p:].reshape(bs, ss, KV_LORA))
        outs['kr_s'].append(kr[n_p:].reshape(bs, ss, MLA_ROPE))
        outs['sre_s'].append(sre_s.reshape(bs, S5_GROUPS, S5_STATE))
        outs['sim_s'].append(sim_s.reshape(bs, S5_GROUPS, S5_STATE))
        outs['gv_s'].append(gv[n_p:].reshape(bs, ss, GMLP_WIDTH))

    st = lambda name: jnp.stack(outs[name])
    return (x[:n_p].reshape(bp, sp, D_MODEL), x[n_p:].reshape(bs, ss, D_MODEL),
            st('lat_p'), st('kr_p'), st('sre_p'), st('sim_p'),
            st('lat_s'), st('kr_s'), st('sre_s'), st('sim_s'), st('gv_s'))
```

```python
import functools
import math

import jax
import jax.numpy as jnp
import numpy as np
from jax import lax
from jax.experimental import pallas as pl
from jax.experimental.pallas import tpu as pltpu

F32 = jnp.float32
BF16 = jnp.bfloat16

D_MODEL = 1024
CHUNK = 64
MLA_HEADS = 4
MLA_NOPE = 128
MLA_ROPE = 64
MLA_V = 128
Q_LORA = 256
KV_LORA = 128
MLA_WIDTH = MLA_HEADS * MLA_V
MLA_SCALE = (MLA_NOPE + MLA_ROPE) ** -0.5
Q_SCALE = MLA_SCALE * math.log2(math.e)
ROPE_THETA = 10000.0
S5_GROUP = 16
S5_GROUPS = 16
S5_WIDTH = S5_GROUP * S5_GROUPS
S5_STATE = 64
S5_COLS = S5_GROUPS * S5_STATE
GMLP_HEADS = 4
GMLP_HEAD_DIM = 64
GMLP_WIDTH = GMLP_HEADS * GMLP_HEAD_DIM
GMLP_CHUNK = 128
N_EXPERTS = 32
TOP_K = 4
D_FF = D_MODEL
SWIGLU_LIMIT = 7.0
SWIGLU_ALPHA = 1.702
NORM_EPS = 1e-5

LANES = 128
SUBLANES = 8
TOKEN_BLOCK = 256
ATTN_BLOCK = 1024
ATTN_HEADS = 2
EXPERT_BLOCK = 256
S5_BLOCK = 256
SORT_ROWS = TOKEN_BLOCK * TOP_K + N_EXPERTS * SUBLANES
GROUPED_WIDTH = D_MODEL // 2 + LANES
VMEM_LIMIT = 56 * 1024 * 1024

_C_Q = 0
_C_KV = _C_Q + Q_LORA
_C_KR = _C_KV + KV_LORA
_C_KRS = _C_KR + LANES
_C_S5 = _C_KRS + LANES
_C_G = _C_S5 + S5_WIDTH
IN_EXT = _C_G + 2 * GMLP_WIDTH
HEAD_SLAB = MLA_NOPE + LANES
QK_WIDTH = MLA_HEADS * HEAD_SLAB


def _cparams(sem, vmem=VMEM_LIMIT):
    return pltpu.CompilerParams(dimension_semantics=sem, vmem_limit_bytes=vmem)


def _dot(a, b):
    return jnp.dot(a, b, preferred_element_type=F32)


def _dot_nt(a, b):
    return lax.dot_general(a, b, (((1,), (1,)), ((), ())), preferred_element_type=F32)


def _split_bf16(x):
    hi = x.astype(BF16)
    lo = (x - hi.astype(F32)).astype(BF16)
    return hi, lo


def _rms(x, g):
    return x * lax.rsqrt(jnp.mean(x * x, axis=-1, keepdims=True) + NORM_EPS) * g


def _ln(x, g, b):
    xc = x - jnp.mean(x, axis=-1, keepdims=True)
    var = jnp.mean(xc * xc, axis=-1, keepdims=True)
    return xc * lax.rsqrt(var + NORM_EPS) * g + b


def _pre_kernel(xp_ref, xs_ref, win_ref, qg_ref, wq_ref, kvg_ref, wkv_ref, cos_ref, sin_ref,
                gg_ref, gb_ref, mavg_ref, wsp_ref, bsp_ref,
                q_ref, k_ref, v_ref, lat_ref, kr_ref, u_ref, gm_ref, gv_ref, *, n_prompt_blocks):
    is_prompt = pl.program_id(0) < n_prompt_blocks
    xb = jnp.where(is_prompt, xp_ref[...], xs_ref[...]).astype(BF16)
    proj = _dot(xb, win_ref[...])
    cos = cos_ref[...]
    sin = sin_ref[...]

    qa = _rms(proj[:, _C_Q:_C_Q + Q_LORA], qg_ref[...]).astype(BF16)
    qq = _dot(qa, wq_ref[...])
    for h in range(MLA_HEADS):
        c0 = h * HEAD_SLAB
        nope = qq[:, c0:c0 + MLA_NOPE]
        rope = (qq[:, c0 + MLA_NOPE:c0 + HEAD_SLAB] * cos
                + qq[:, QK_WIDTH + h * LANES:QK_WIDTH + (h + 1) * LANES] * sin)
        q_ref[:, c0:c0 + MLA_NOPE] = (nope * Q_SCALE).astype(BF16)
        q_ref[:, c0 + MLA_NOPE:c0 + HEAD_SLAB] = (rope * Q_SCALE).astype(BF16)

    lat = _rms(proj[:, _C_KV:_C_KV + KV_LORA], kvg_ref[...])
    lat_ref[...] = lat
    kv = _dot(lat.astype(BF16), wkv_ref[...])
    kr = proj[:, _C_KR:_C_KR + LANES] * cos + proj[:, _C_KRS:_C_KRS + LANES] * sin
    kr_ref[...] = kr[:, :MLA_ROPE]
    krb = kr.astype(BF16)
    for h in range(MLA_HEADS):
        c0 = h * HEAD_SLAB
        k_ref[:, c0:c0 + MLA_NOPE] = kv[:, h * 256:h * 256 + MLA_NOPE].astype(BF16)
        k_ref[:, c0 + MLA_NOPE:c0 + HEAD_SLAB] = krb
        v_ref[:, h * MLA_V:(h + 1) * MLA_V] = kv[:, h * 256 + MLA_NOPE:(h + 1) * 256].astype(BF16)

    u_ref[...] = proj[:, _C_S5:_C_S5 + S5_WIDTH]

    z = jax.nn.gelu(proj[:, _C_G:_C_G + 2 * GMLP_WIDTH])
    ug = z[:, :GMLP_WIDTH]
    vg = z[:, GMLP_WIDTH:]
    mavg = mavg_ref[...]
    hi, lo = _split_bf16(vg)
    xc = vg - (_dot(hi, mavg) + _dot(lo, mavg))
    hi, lo = _split_bf16(xc * xc)
    var = _dot(hi, mavg) + _dot(lo, mavg)
    vn = xc * lax.rsqrt(var + NORM_EPS) * gg_ref[...] + gb_ref[...]
    gv_ref[...] = vn
    lane = lax.broadcasted_iota(jnp.int32, (GMLP_CHUNK, GMLP_WIDTH), 1)
    wsp = wsp_ref[0]
    bsp = bsp_ref[0]
    for c in range(xp_ref.shape[0] // GMLP_CHUNK):
        r0 = c * GMLP_CHUNK
        vc = vn[r0:r0 + GMLP_CHUNK, :].astype(BF16)
        stack = jnp.concatenate(
            [jnp.where(lane // GMLP_HEAD_DIM == h, vc, jnp.zeros_like(vc))
             for h in range(GMLP_HEADS)], axis=0)
        mix = _dot(wsp, stack) + bsp
        gm_ref[r0:r0 + GMLP_CHUNK, :] = ug[r0:r0 + GMLP_CHUNK, :] * mix


def _split_rows(tb, npb):
    prompt = lambda w: pl.BlockSpec((tb, w), lambda i, *_: (jnp.minimum(i, npb - 1), 0))
    sample = lambda w: pl.BlockSpec((tb, w), lambda i, *_: (jnp.maximum(i - npb, 0), 0))
    return prompt, sample


def _pre_call(xp, xs, lw):
    tb = TOKEN_BLOCK
    n = xp.shape[0] + xs.shape[0]
    nb = n // tb
    n_prompt_blocks = xp.shape[0] // tb
    row = lambda w: pl.BlockSpec((tb, w), lambda i: (i, 0))
    prompt_row, sample_row = _split_rows(tb, n_prompt_blocks)
    full = lambda a: pl.BlockSpec(a.shape, lambda i: (0,) * a.ndim)
    variant = lambda i: (jnp.where(i < n_prompt_blocks, 0, 1), 0, 0)
    in_specs = [prompt_row(D_MODEL), sample_row(D_MODEL),
                full(lw['w_in']), full(lw['q_g']), full(lw['wq']), full(lw['kv_g']),
                full(lw['wkv']), row(LANES), row(LANES), full(lw['g_g']), full(lw['g_b']),
                full(lw['mavg']),
                pl.BlockSpec((1, GMLP_CHUNK, GMLP_HEADS * GMLP_CHUNK), variant),
                pl.BlockSpec((1, GMLP_CHUNK, GMLP_WIDTH), variant)]
    widths = [(QK_WIDTH, BF16), (QK_WIDTH, BF16), (MLA_WIDTH, BF16), (KV_LORA, F32),
              (MLA_ROPE, F32), (S5_WIDTH, F32), (GMLP_WIDTH, F32), (GMLP_WIDTH, F32)]
    return pl.pallas_call(
        functools.partial(_pre_kernel, n_prompt_blocks=n_prompt_blocks),
        grid=(nb,),
        in_specs=in_specs,
        out_specs=[row(w) for w, _ in widths],
        out_shape=[jax.ShapeDtypeStruct((n, w), dt) for w, dt in widths],
        compiler_params=_cparams(("parallel",)),
        name="pre",
    )(xp, xs, lw['w_in'], lw['q_g'], lw['wq'], lw['kv_g'], lw['wkv'], lw['cos'], lw['sin'],
      lw['g_g'], lw['g_b'], lw['mavg'], lw['wsp'], lw['bsp'])


def _attn_kernel(q_ref, k_ref, v_ref, o_ref):
    i = pl.program_id(2)
    bq = q_ref.shape[0]
    qs = [q_ref[:, h * HEAD_SLAB:(h + 1) * HEAD_SLAB] for h in range(ATTN_HEADS)]

    def step(h, r0, carry, mask):
        m, l, acc = carry
        s = _dot_nt(qs[h], k_ref[pl.ds(r0, bq), h * HEAD_SLAB:(h + 1) * HEAD_SLAB])
        if mask is not None:
            s = jnp.where(mask, s, -jnp.inf)
        m_new = jnp.maximum(m, jnp.max(s, axis=-1, keepdims=True))
        p = jnp.exp2(s - m_new)
        alpha = jnp.exp2(m - m_new)
        l = alpha * l + jnp.sum(p, axis=-1, keepdims=True)
        acc = alpha * acc + _dot(p.astype(BF16), v_ref[pl.ds(r0, bq), h * MLA_V:(h + 1) * MLA_V])
        return m_new, l, acc

    def body(j, carries):
        r0 = pl.multiple_of(j * bq, bq)
        return tuple(step(h, r0, carries[h], None) for h in range(ATTN_HEADS))

    init = (jnp.full((bq, 1), -jnp.inf, F32), jnp.zeros((bq, 1), F32), jnp.zeros((bq, MLA_V), F32))
    carries = lax.fori_loop(0, i, body, (init,) * ATTN_HEADS)
    r0 = pl.multiple_of(i * bq, bq)
    qc = lax.broadcasted_iota(jnp.int32, (bq, bq), 0) // CHUNK
    kc = lax.broadcasted_iota(jnp.int32, (bq, bq), 1) // CHUNK
    for h in range(ATTN_HEADS):
        m, l, acc = step(h, r0, carries[h], kc <= qc)
        o_ref[:, h * MLA_V:(h + 1) * MLA_V] = acc / l


def _attn_call(q, k, v, batch, seq):
    bq = ATTN_BLOCK
    nq = seq // bq
    nh = ATTN_HEADS
    return pl.pallas_call(
        _attn_kernel,
        grid=(batch, MLA_HEADS // nh, nq),
        in_specs=[pl.BlockSpec((bq, nh * HEAD_SLAB), lambda b, h, i: (b * nq + i, h)),
                  pl.BlockSpec((seq, nh * HEAD_SLAB), lambda b, h, i: (b, h)),
                  pl.BlockSpec((seq, nh * MLA_V), lambda b, h, i: (b, h))],
        out_specs=pl.BlockSpec((bq, nh * MLA_V), lambda b, h, i: (b * nq + i, h)),
        out_shape=jax.ShapeDtypeStruct((batch * seq, MLA_WIDTH), F32),
        compiler_params=_cparams(("parallel", "parallel", "arbitrary")),
        name="attn_prompt",
    )(q, k, v)


def _attn_sample_kernel(q_ref, k_ref, v_ref, plat_ref, pkr_ref, wkv_ref, o_ref, *, past):
    t = q_ref.shape[0]
    q = q_ref[...]
    kn = k_ref[...]
    vn = v_ref[...]
    kvp = _dot(plat_ref[0].astype(BF16), wkv_ref[...]).astype(BF16)
    krp = pkr_ref[0].astype(BF16)
    q_chunk = (past + lax.broadcasted_iota(jnp.int32, (t, 1), 0)) // CHUNK
    mask_p = lax.broadcasted_iota(jnp.int32, (t, past), 1) // CHUNK <= q_chunk
    mask_n = (past + lax.broadcasted_iota(jnp.int32, (t, t), 1)) // CHUNK <= q_chunk
    for h in range(MLA_HEADS):
        c0 = h * HEAD_SLAB
        qh = q[:, c0:c0 + HEAD_SLAB]
        s_p = (_dot_nt(qh[:, :MLA_NOPE], kvp[:, h * 256:h * 256 + MLA_NOPE])
               + _dot_nt(qh[:, MLA_NOPE:MLA_NOPE + MLA_ROPE], krp))
        s_n = _dot_nt(qh, kn[:, c0:c0 + HEAD_SLAB])
        s_p = jnp.where(mask_p, s_p, -jnp.inf)
        s_n = jnp.where(mask_n, s_n, -jnp.inf)
        m = jnp.maximum(jnp.max(s_p, axis=-1, keepdims=True), jnp.max(s_n, axis=-1, keepdims=True))
        p_p = jnp.exp2(s_p - m)
        p_n = jnp.exp2(s_n - m)
        l = jnp.sum(p_p, axis=-1, keepdims=True) + jnp.sum(p_n, axis=-1, keepdims=True)
        o = (_dot(p_p.astype(BF16), kvp[:, h * 256 + MLA_NOPE:(h + 1) * 256])
             + _dot(p_n.astype(BF16), vn[:, h * MLA_V:(h + 1) * MLA_V]))
        o_ref[:, h * MLA_V:(h + 1) * MLA_V] = o / l


def _attn_sample_call(q, k, v, past_lat, past_kr, wkv, row0, batch, t):
    past = past_lat.shape[1]
    blk0 = row0 // t
    return pl.pallas_call(
        functools.partial(_attn_sample_kernel, past=past),
        grid=(batch,),
        in_specs=[pl.BlockSpec((t, QK_WIDTH), lambda b: (blk0 + b, 0)),
                  pl.BlockSpec((t, QK_WIDTH), lambda b: (blk0 + b, 0)),
                  pl.BlockSpec((t, MLA_WIDTH), lambda b: (blk0 + b, 0)),
                  pl.BlockSpec((1, past, KV_LORA), lambda b: (b, 0, 0)),
                  pl.BlockSpec((1, past, MLA_ROPE), lambda b: (b, 0, 0)),
                  pl.BlockSpec(wkv.shape, lambda b: (0, 0))],
        out_specs=pl.BlockSpec((t, MLA_WIDTH), lambda b: (b, 0)),
        out_shape=jax.ShapeDtypeStruct((batch * t, MLA_WIDTH), F32),
        compiler_params=_cparams(("parallel",)),
        name="attn_sample",
    )(q, k, v, past_lat, past_kr, wkv)


def _s5_kernel(u_ref, s0r_ref, s0i_ref, bb_ref, tab_ref, cc_ref, d_ref, wg_ref, bg_ref,
               y_ref, sr_ref, si_ref, st_ref, cr_ref, ci_ref):
    tb = u_ref.shape[0]
    nc = S5_COLS

    @pl.when(pl.program_id(1) == 0)
    def _():
        cr_ref[...] = jnp.broadcast_to(s0r_ref[0], (SUBLANES, nc))
        ci_ref[...] = jnp.broadcast_to(s0i_ref[0], (SUBLANES, nc))

    u = u_ref[...]
    st_ref[...] = _dot(u.astype(BF16), bb_ref[...])

    def tile(r, carry):
        car, cai = carry
        r0 = pl.multiple_of(r * SUBLANES, SUBLANES)
        xr = st_ref[pl.ds(r0, SUBLANES), :nc]
        xi = st_ref[pl.ds(r0, SUBLANES), nc:]
        for si, d in enumerate((1, 2, 4)):
            pr = tab_ref[si, 0]
            pi = tab_ref[si, 1]
            sr = pltpu.roll(xr, d, 0)
            sim = pltpu.roll(xi, d, 0)
            xr, xi = xr + (pr * sr - pi * sim), xi + (pr * sim + pi * sr)
        pr = tab_ref[3, 0]
        pi = tab_ref[3, 1]
        xr, xi = xr + (pr * car - pi * cai), xi + (pr * cai + pi * car)
        st_ref[pl.ds(r0, SUBLANES), :nc] = xr
        st_ref[pl.ds(r0, SUBLANES), nc:] = xi
        return (jnp.broadcast_to(xr[SUBLANES - 1:SUBLANES, :], (SUBLANES, nc)),
                jnp.broadcast_to(xi[SUBLANES - 1:SUBLANES, :], (SUBLANES, nc)))

    car, cai = lax.fori_loop(0, tb // SUBLANES, tile, (cr_ref[...], ci_ref[...]))
    cr_ref[...] = car
    ci_ref[...] = cai
    sr_ref[0] = car[0:1, :]
    si_ref[0] = cai[0:1, :]

    y = _dot(st_ref[...].astype(BF16), cc_ref[...]) + d_ref[...] * u
    z = jax.nn.gelu(y)
    y_ref[...] = z * jax.nn.sigmoid(_dot(z.astype(BF16), wg_ref[...]) + bg_ref[...])


def _s5_call(u, s0r, s0i, lw, row0, batch, seq, tblk, name):
    nt = seq // tblk
    blk0 = row0 // tblk
    full = lambda a: pl.BlockSpec(a.shape, lambda b, t: (0,) * a.ndim)
    st_spec = pl.BlockSpec((1, 1, S5_COLS), lambda b, t: (b, 0, 0))
    return pl.pallas_call(
        _s5_kernel,
        grid=(batch, nt),
        in_specs=[pl.BlockSpec((tblk, S5_WIDTH), lambda b, t: (blk0 + b * nt + t, 0)),
                  st_spec, st_spec,
                  full(lw['s5_bb']), full(lw['s5_tab']), full(lw['s5_cc']), full(lw['s5_d']),
                  full(lw['s5_wg']), full(lw['s5_bg'])],
        out_specs=[pl.BlockSpec((tblk, S5_WIDTH), lambda b, t: (b * nt + t, 0)), st_spec, st_spec],
        out_shape=[jax.ShapeDtypeStruct((batch * seq, S5_WIDTH), F32),
                   jax.ShapeDtypeStruct((batch, 1, S5_COLS), F32),
                   jax.ShapeDtypeStruct((batch, 1, S5_COLS), F32)],
        scratch_shapes=[pltpu.VMEM((tblk, 2 * S5_COLS), F32),
                        pltpu.VMEM((SUBLANES, S5_COLS), F32),
                        pltpu.VMEM((SUBLANES, S5_COLS), F32)],
        compiler_params=_cparams(("parallel", "arbitrary")),
        name=name,
    )(u, s0r, s0i, lw['s5_bb'], lw['s5_tab'], lw['s5_cc'], lw['s5_d'], lw['s5_wg'], lw['s5_bg'])


def _post_kernel(ap_ref, as_ref, sp_ref, ss_ref, g_ref, xp_ref, xs_ref, mg_ref, wo_ref, l1g_ref,
                 l1b_ref, rwh_ref, rwl_ref, rb_ref, tri_ref, upper_ref,
                 x1_ref, info_ref, cnt_ref, *, alpha, n_prompt_blocks):
    tb = xp_ref.shape[0]
    mg = mg_ref[...]
    is_prompt = pl.program_id(0) < n_prompt_blocks
    na = _rms(jnp.where(is_prompt, ap_ref[...], as_ref[...]), mg[:, :MLA_WIDTH]).astype(BF16)
    ns = _rms(jnp.where(is_prompt, sp_ref[...], ss_ref[...]),
              mg[:, MLA_WIDTH:MLA_WIDTH + S5_WIDTH]).astype(BF16)
    ng = _rms(g_ref[...], mg[:, MLA_WIDTH + S5_WIDTH:]).astype(BF16)
    mixed = (_dot(na, wo_ref[:MLA_WIDTH, :]) + _dot(ns, wo_ref[MLA_WIDTH:MLA_WIDTH + S5_WIDTH, :])
             + _dot(ng, wo_ref[MLA_WIDTH + S5_WIDTH:, :]))
    x = jnp.where(is_prompt, xp_ref[...], xs_ref[...])
    x1 = _ln(alpha * x + mixed, l1g_ref[...], l1b_ref[...])
    x1_ref[...] = x1

    hi, lo = _split_bf16(x1)
    rwh = rwh_ref[...]
    logits = _dot(hi, rwh) + (_dot(lo, rwh) + _dot(hi, rwl_ref[...])) + rb_ref[...]

    eidx = lax.broadcasted_iota(jnp.int32, (tb, N_EXPERTS), 1).astype(F32)
    work = logits
    tops, sels, hots = [], [], []
    for _ in range(TOP_K):
        m = jnp.max(work, axis=-1, keepdims=True)
        sel = jnp.min(jnp.where(work == m, eidx, float(N_EXPERTS)), axis=-1, keepdims=True)
        hot = eidx == sel
        tops.append(m)
        sels.append(sel)
        hots.append(hot)
        work = jnp.where(hot, -jnp.inf, work)
    exps = [jnp.exp(t - tops[0]) for t in tops]
    den = exps[0] + exps[1] + exps[2] + exps[3]

    cnt = jnp.zeros((tb, N_EXPERTS), F32)
    for hot in hots:
        cnt = cnt + hot.astype(F32)
    tot = jnp.sum(cnt, axis=0, keepdims=True)
    seg = jnp.ceil(tot * (1.0 / SUBLANES))
    seg8 = jnp.broadcast_to(seg, (SUBLANES, N_EXPERTS)).astype(BF16)
    start = _dot(seg8, upper_ref[...])[0:1, :] * float(SUBLANES)
    before = _dot(tri_ref[...], cnt.astype(BF16)) + start
    lane = lax.broadcasted_iota(jnp.int32, (tb, LANES), 1)
    info = jnp.zeros((tb, LANES), F32)
    for k in range(TOP_K):
        pos = jnp.sum(jnp.where(hots[k], before, 0.0), axis=-1, keepdims=True)
        info = jnp.where(lane == k, exps[k] / den, info)
        info = jnp.where(lane == TOP_K + k, sels[k], info)
        info = jnp.where(lane == 2 * TOP_K + k, pos, info)
    info_ref[...] = info
    cnt_ref[0] = tot


def _post_call(attn_p, attn_s, ssm_p, ssm_s, gm, xp, xs, lw, alpha):
    n = xp.shape[0] + xs.shape[0]
    tb = TOKEN_BLOCK
    npb = xp.shape[0] // tb
    row = lambda w: pl.BlockSpec((tb, w), lambda i: (i, 0))
    prompt_row, sample_row = _split_rows(tb, npb)
    full = lambda a: pl.BlockSpec(a.shape, lambda i: (0,) * a.ndim)
    names = ['mix_g', 'w_out', 'ln1_g', 'ln1_b', 'rw_hi', 'rw_lo', 'router_b', 'tri', 'upper']
    return pl.pallas_call(
        functools.partial(_post_kernel, alpha=alpha, n_prompt_blocks=npb),
        grid=(n // tb,),
        in_specs=[prompt_row(MLA_WIDTH), sample_row(MLA_WIDTH), prompt_row(S5_WIDTH), sample_row(S5_WIDTH),
                  row(GMLP_WIDTH), prompt_row(D_MODEL), sample_row(D_MODEL)]
                 + [full(lw[k]) for k in names],
        out_specs=[row(D_MODEL), row(LANES), pl.BlockSpec((1, 1, N_EXPERTS), lambda i: (i, 0, 0))],
        out_shape=[jax.ShapeDtypeStruct((n, D_MODEL), F32),
                   jax.ShapeDtypeStruct((n, LANES), F32),
                   jax.ShapeDtypeStruct((n // tb, 1, N_EXPERTS), F32)],
        compiler_params=_cparams(("parallel",)),
        name="post",
    )(attn_p, attn_s, ssm_p, ssm_s, gm, xp, xs, *[lw[k] for k in names])


def _segment_copies(tab_ref, step, src, dst, src_is_local, sem):
    copies = []
    for e in range(N_EXPERTS):
        base = (step * N_EXPERTS + e) * 3
        loc = pl.multiple_of(tab_ref[base], SUBLANES)
        glo = pl.multiple_of(tab_ref[base + 1], SUBLANES)
        n = pl.multiple_of(tab_ref[base + 2], SUBLANES)
        s_at, d_at = (loc, glo) if src_is_local else (glo, loc)
        copies.append((n, pltpu.make_async_copy(src.at[pl.ds(s_at, n), :], dst.at[pl.ds(d_at, n), :], sem)))
    return copies


def _start_copies(copies):
    for n, cp in copies:
        @pl.when(n > 0)
        def _(cp=cp):
            cp.start()


def _wait_copies(copies):
    for n, cp in copies:
        @pl.when(n > 0)
        def _(cp=cp):
            cp.wait()


def _dispatch_kernel(tab_ref, x_ref, pos_ref, gate_ref, xb_ref, sbuf, zbuf, sem):
    step = pl.program_id(0)
    slot = step % 2
    tb = x_ref.shape[0]
    rows = lax.broadcasted_iota(jnp.int32, (SORT_ROWS, tb), 0)
    hit = None
    gates = jnp.zeros((SORT_ROWS, tb), F32)
    for k in range(TOP_K):
        eq = rows == pos_ref[0, k:k + 1, :]
        hit = eq if hit is None else jnp.logical_or(hit, eq)
        gates = jnp.where(eq, gate_ref[0, k:k + 1, :], gates)
    perm = jnp.where(hit, 1.0, 0.0).astype(BF16)
    xs = _dot(perm, x_ref[...].astype(BF16))
    half = D_MODEL // 2
    sbuf[slot, :, :half] = pltpu.pack_elementwise([xs[:, :half], xs[:, half:]], packed_dtype=BF16)
    row_gate = jnp.broadcast_to(jnp.sum(gates, axis=-1, keepdims=True), (SORT_ROWS, LANES))
    gate_hi = row_gate.astype(BF16).astype(F32)
    sbuf[slot, :, half:] = pltpu.pack_elementwise([gate_hi, row_gate - gate_hi], packed_dtype=BF16)

    def copies(s):
        return _segment_copies(tab_ref, s, sbuf.at[s % 2], xb_ref, True, sem.at[s % 2])

    n_steps = pl.num_programs(0)
    fill_base = n_steps * N_EXPERTS * 3
    fill_sem = sem.at[2]

    def pad_fills():
        out = []
        for e in range(N_EXPERTS):
            start = pl.multiple_of(tab_ref[fill_base + 2 * e], SUBLANES)
            n = pl.multiple_of(tab_ref[fill_base + 2 * e + 1], SUBLANES)
            out.append((n, pltpu.make_async_copy(zbuf.at[pl.ds(0, n), :], xb_ref.at[pl.ds(start, n), :], fill_sem)))
        return out

    def block_fill(j):
        row = pl.multiple_of(tab_ref[fill_base + 2 * N_EXPERTS] + j * EXPERT_BLOCK, EXPERT_BLOCK)
        return pltpu.make_async_copy(zbuf, xb_ref.at[pl.ds(row, EXPERT_BLOCK), :], fill_sem)

    n_unused = tab_ref[fill_base + 2 * N_EXPERTS + 1]

    @pl.when(step == 0)
    def _():
        zbuf[...] = jnp.zeros_like(zbuf)
        _start_copies(pad_fills())
        lax.fori_loop(0, n_unused, lambda j, c: (block_fill(j).start(), c)[1], 0)

    @pl.when(step > 0)
    def _():
        _wait_copies(copies(step - 1))

    _start_copies(copies(step))

    @pl.when(step == n_steps - 1)
    def _():
        _wait_copies(copies(step))
        _wait_copies(pad_fills())
        lax.fori_loop(0, n_unused, lambda j, c: (block_fill(j).wait(), c)[1], 0)


def _dispatch_call(seg_tab, x1, pos_t, gate_t, n_rows):
    n = x1.shape[0]
    tb = TOKEN_BLOCK
    grid_spec = pltpu.PrefetchScalarGridSpec(
        num_scalar_prefetch=1,
        grid=(n // tb,),
        in_specs=[pl.BlockSpec((tb, D_MODEL), lambda i, t: (i, 0)),
                  pl.BlockSpec((1, TOP_K, tb), lambda i, t: (i, 0, 0)),
                  pl.BlockSpec((1, TOP_K, tb), lambda i, t: (i, 0, 0))],
        out_specs=pl.BlockSpec(memory_space=pl.ANY),
        scratch_shapes=[pltpu.VMEM((2, SORT_ROWS, GROUPED_WIDTH), jnp.uint32),
                        pltpu.VMEM((EXPERT_BLOCK, GROUPED_WIDTH), jnp.uint32),
                        pltpu.SemaphoreType.DMA((3,))],
    )
    return pl.pallas_call(
        _dispatch_kernel,
        grid_spec=grid_spec,
        out_shape=jax.ShapeDtypeStruct((n_rows, GROUPED_WIDTH), jnp.uint32),
        compiler_params=_cparams(("arbitrary",)),
        name="dispatch",
    )(seg_tab, x1, pos_t, gate_t)


def _expert_kernel(be_ref, nv_ref, nx_ref, x_ref, bgu_ref, bdn_ref, wgu_hbm, wdn_hbm, y_ref,
                   wgu_st, wdn_st, wgu_bf, wdn_bf, sem):
    i = pl.program_id(0)
    e = be_ref[i]
    changed = jnp.logical_or(i == 0, be_ref[jnp.maximum(i - 1, 0)] != e)

    def fetch(ex):
        return (pltpu.make_async_copy(wgu_hbm.at[ex], wgu_st, sem.at[0]),
                pltpu.make_async_copy(wdn_hbm.at[ex], wdn_st, sem.at[1]))

    @pl.when(i == 0)
    def _():
        for cp in fetch(e):
            cp.start()

    @pl.when(changed)
    def _():
        for cp in fetch(e):
            cp.wait()
        wgu_bf[...] = wgu_st[...].astype(BF16)
        wdn_bf[...] = wdn_st[...].astype(BF16)

        @pl.when(nx_ref[i] >= 0)
        def _():
            for cp in fetch(nx_ref[i]):
                cp.start()

    @pl.when(nv_ref[i] > 0)
    def _():
        live = lax.broadcasted_iota(jnp.int32, (x_ref.shape[0], 1), 0) < nv_ref[i]
        half = D_MODEL // 2
        halves = lambda words: [jnp.where(live, pltpu.unpack_elementwise(
            words, index=idx, packed_dtype=BF16, unpacked_dtype=F32), 0.0) for idx in (0, 1)]
        x_lo, x_hi = halves(x_ref[:, :half])
        g_hi, g_rest = halves(x_ref[:, half:half + LANES])
        row_gate = (g_hi + g_rest)[:, 0:1]
        gu = (_dot(x_lo.astype(BF16), wgu_bf[:half, :]) + _dot(x_hi.astype(BF16), wgu_bf[half:, :])
              + bgu_ref[0])
        gate = jnp.minimum(gu[:, :D_FF], SWIGLU_LIMIT)
        up = jnp.clip(gu[:, D_FF:], -SWIGLU_LIMIT, SWIGLU_LIMIT)
        h = (up + 1.0) * (gate * jax.nn.sigmoid(SWIGLU_ALPHA * gate))
        y_ref[...] = (_dot(h.astype(BF16), wdn_bf[...]) + bdn_ref[0]) * row_gate

    @pl.when(nv_ref[i] == 0)
    def _():
        y_ref[...] = jnp.zeros_like(y_ref)


def _expert_call(blk_e, n_valid, next_e, xb, wgu, bgu, wdn, bdn):
    n_rows = xb.shape[0]
    eb = EXPERT_BLOCK
    grid_spec = pltpu.PrefetchScalarGridSpec(
        num_scalar_prefetch=3,
        grid=(n_rows // eb,),
        in_specs=[pl.BlockSpec((eb, GROUPED_WIDTH), lambda i, be, nv, nx: (i, 0)),
                  pl.BlockSpec((1, 1, 2 * D_FF), lambda i, be, nv, nx: (be[i], 0, 0)),
                  pl.BlockSpec((1, 1, D_MODEL), lambda i, be, nv, nx: (be[i], 0, 0)),
                  pl.BlockSpec(memory_space=pl.ANY),
                  pl.BlockSpec(memory_space=pl.ANY)],
        out_specs=pl.BlockSpec((eb, D_MODEL), lambda i, be, nv, nx: (i, 0)),
        scratch_shapes=[pltpu.VMEM((D_MODEL, 2 * D_FF), F32), pltpu.VMEM((D_FF, D_MODEL), F32),
                        pltpu.VMEM((D_MODEL, 2 * D_FF), BF16), pltpu.VMEM((D_FF, D_MODEL), BF16),
                        pltpu.SemaphoreType.DMA((2,))],
    )
    return pl.pallas_call(
        _expert_kernel,
        grid_spec=grid_spec,
        out_shape=jax.ShapeDtypeStruct((n_rows, D_MODEL), F32),
        compiler_params=_cparams(("arbitrary",)),
        name="experts",
    )(blk_e, n_valid, next_e, xb, bgu, bdn, wgu, wdn)


def _combine_kernel(tab_ref, info_ref, x1_ref, g_ref, b_ref, yb_ref, op_ref, os_ref, buf, sem, *,
                    alpha, n_prompt_blocks):
    tb = x1_ref.shape[0]
    step = pl.program_id(0)
    slot = step % 2
    tail = tb * TOP_K

    def fetch(s):
        buf[s % 2, tail:, :] = jnp.zeros((SORT_ROWS - tail, D_MODEL), F32)
        _start_copies(_segment_copies(tab_ref, s, yb_ref, buf.at[s % 2], False, sem.at[s % 2]))

    @pl.when(step == 0)
    def _():
        fetch(step)

    @pl.when(step + 1 < pl.num_programs(0))
    def _():
        fetch(step + 1)

    _wait_copies(_segment_copies(tab_ref, step, yb_ref, buf.at[slot], False, sem.at[slot]))
    info = info_ref[...]
    cols = lax.broadcasted_iota(jnp.int32, (tb, SORT_ROWS), 1).astype(F32)
    hit = None
    for k in range(TOP_K):
        eq = cols == info[:, 2 * TOP_K + k:2 * TOP_K + k + 1]
        hit = eq if hit is None else jnp.logical_or(hit, eq)
    unsort = jnp.where(hit, 1.0, 0.0).astype(BF16)
    hi, lo = _split_bf16(buf[slot])
    moe = _dot(unsort, hi) + _dot(unsort, lo)
    out = _ln(alpha * x1_ref[...] + moe, g_ref[...], b_ref[...])

    @pl.when(step < n_prompt_blocks)
    def _():
        op_ref[...] = out

    @pl.when(step >= n_prompt_blocks)
    def _():
        os_ref[...] = out


def _combine_call(seg_tab, info, x1, ln_g, ln_b, yb, alpha, n_prompt):
    n = x1.shape[0]
    tb = TOKEN_BLOCK
    npb = n_prompt // tb
    prompt_row, sample_row = _split_rows(tb, npb)
    grid_spec = pltpu.PrefetchScalarGridSpec(
        num_scalar_prefetch=1,
        grid=(n // tb,),
        in_specs=[pl.BlockSpec((tb, LANES), lambda i, t: (i, 0)),
                  pl.BlockSpec((tb, D_MODEL), lambda i, t: (i, 0)),
                  pl.BlockSpec((1, D_MODEL), lambda i, t: (0, 0)),
                  pl.BlockSpec((1, D_MODEL), lambda i, t: (0, 0)),
                  pl.BlockSpec(memory_space=pl.ANY)],
        out_specs=[prompt_row(D_MODEL), sample_row(D_MODEL)],
        scratch_shapes=[pltpu.VMEM((2, SORT_ROWS, D_MODEL), F32), pltpu.SemaphoreType.DMA((2,))],
    )
    return pl.pallas_call(
        functools.partial(_combine_kernel, alpha=alpha, n_prompt_blocks=npb),
        grid_spec=grid_spec,
        out_shape=[jax.ShapeDtypeStruct((n_prompt, D_MODEL), F32),
                   jax.ShapeDtypeStruct((n - n_prompt, D_MODEL), F32)],
        compiler_params=_cparams(("arbitrary",)),
        name="combine",
    )(seg_tab, info, x1, ln_g, ln_b, yb)


def _rope_tables(pos):
    half = MLA_ROPE // 2
    inv_freq = ROPE_THETA ** (-jnp.arange(half, dtype=F32) / half)
    ang = pos.astype(F32)[:, None] * inv_freq[None, :]
    cos, sin = jnp.cos(ang), jnp.sin(ang)
    zero = jnp.zeros((pos.shape[0], LANES - MLA_ROPE), F32)
    return (jnp.concatenate([cos, cos, zero], axis=1), jnp.concatenate([-sin, sin, zero], axis=1))


def _s5_params(lam_re, lam_im, log_dt, b_re, b_im, c_re, c_im):
    dt = jnp.exp(log_dt)[:, None]
    mag = jnp.exp(lam_re * dt)
    ab_re = mag * jnp.cos(lam_im * dt)
    ab_im = mag * jnp.sin(lam_im * dt)
    den = lam_re * lam_re + lam_im * lam_im
    nr, ni = ab_re - 1.0, ab_im
    f_re = (nr * lam_re + ni * lam_im) / den
    f_im = (ni * lam_re - nr * lam_im) / den
    bb_re = f_re[..., None] * b_re - f_im[..., None] * b_im
    bb_im = f_re[..., None] * b_im + f_im[..., None] * b_re
    eye = jnp.eye(S5_GROUPS, dtype=F32)

    def in_blockdiag(w):
        return jnp.einsum('gnc,gh->gchn', w, eye).reshape(S5_WIDTH, S5_COLS)

    def out_blockdiag(w):
        return jnp.einsum('gcn,gh->gnhc', w, eye).reshape(S5_COLS, S5_WIDTH)

    bb = jnp.concatenate([in_blockdiag(bb_re), in_blockdiag(bb_im)], axis=1).astype(BF16)
    cc = jnp.concatenate([out_blockdiag(c_re), -out_blockdiag(c_im)], axis=0).astype(BF16)
    ar, ai = ab_re.reshape(1, S5_COLS), ab_im.reshape(1, S5_COLS)
    pr, pi = [ar], [ai]
    for _ in range(SUBLANES - 1):
        pr, pi = pr + [pr[-1] * ar - pi[-1] * ai], pi + [pr[-1] * ai + pi[-1] * ar]
    rows = jnp.arange(SUBLANES)[:, None]
    tabs = []
    for d in (1, 2, 4):
        keep = rows >= d
        tabs.append(jnp.stack([jnp.where(keep, pr[d - 1], 0.0), jnp.where(keep, pi[d - 1], 0.0)]))
    tabs.append(jnp.stack([jnp.concatenate(pr, axis=0), jnp.concatenate(pi, axis=0)]))
    return bb, cc, jnp.stack(tabs)


def _gmlp_spatial(w_s, b_s, chunk_len):
    reps = GMLP_CHUNK // chunk_len
    i = jnp.arange(chunk_len)
    mask = (i[None, :] // CHUNK) <= (i[:, None] // CHUNK)
    w = jnp.where(mask[None], w_s[:, :chunk_len, :chunk_len], 0.0)
    eye = jnp.eye(reps, dtype=F32)
    wblk = jnp.einsum('hij,rs->hrisj', w, eye).reshape(GMLP_HEADS, GMLP_CHUNK, GMLP_CHUNK)
    wcat = jnp.transpose(wblk, (1, 0, 2)).reshape(GMLP_CHUNK, GMLP_HEADS * GMLP_CHUNK)
    bias = jnp.tile(b_s[:, :chunk_len], (1, reps))
    bias = jnp.repeat(bias.T, GMLP_HEAD_DIM, axis=1)
    return wcat.astype(BF16), bias


def _layer_weights(p, l, cos, sin, dec_seq):
    lw = {'cos': cos, 'sin': sin}
    w_in = p['w_in'][l]
    kr = w_in[:, _C_KR:_C_KR + MLA_ROPE]
    half = MLA_ROPE // 2
    zero = jnp.zeros((D_MODEL, LANES - MLA_ROPE), F32)
    rest = w_in[:, _C_KR + MLA_ROPE:]
    lw['w_in'] = jnp.concatenate(
        [w_in[:, :_C_KR], kr, zero, kr[:, half:], kr[:, :half], zero, rest], axis=1).astype(BF16)
    wq = p['w_q_b'][l].reshape(Q_LORA, MLA_HEADS, MLA_NOPE + MLA_ROPE)
    zq = jnp.zeros((Q_LORA, MLA_HEADS, LANES - MLA_ROPE), F32)
    rope = wq[:, :, MLA_NOPE:]
    plain = jnp.concatenate([wq, zq], axis=2).reshape(Q_LORA, QK_WIDTH)
    swapped = jnp.concatenate([rope[:, :, half:], rope[:, :, :half], zq], axis=2)
    lw['wq'] = jnp.concatenate([plain, swapped.reshape(Q_LORA, MLA_HEADS * LANES)], axis=1).astype(BF16)
    lw['q_g'] = p['q_a_norm_g'][l].reshape(1, Q_LORA)
    lw['kv_g'] = p['kv_a_norm_g'][l].reshape(1, KV_LORA)
    lw['wkv'] = p['w_kv_b'][l].astype(BF16)
    lw['s5_bb'], lw['s5_cc'], lw['s5_tab'] = _s5_params(
        p['s5_lam_re'][l], p['s5_lam_im'][l], p['s5_log_dt'][l], p['s5_b_re'][l], p['s5_b_im'][l],
        p['s5_c_re'][l], p['s5_c_im'][l])
    lw['s5_d'] = p['s5_d'][l].reshape(1, S5_WIDTH)
    lw['s5_wg'] = p['s5_w_glu'][l].astype(BF16)
    lw['s5_bg'] = p['s5_b_glu'][l].reshape(1, S5_WIDTH)
    lw['g_g'] = p['gmlp_norm_g'][l].reshape(1, GMLP_WIDTH)
    lw['g_b'] = p['gmlp_norm_b'][l].reshape(1, GMLP_WIDTH)
    grp = jnp.arange(GMLP_WIDTH) // GMLP_HEAD_DIM
    lw['mavg'] = jnp.where(grp[:, None] == grp[None, :], 1.0 / GMLP_HEAD_DIM, 0.0).astype(BF16)
    wp, bp = _gmlp_spatial(p['gmlp_w_s'][l], p['gmlp_b_s'][l], GMLP_CHUNK)
    ws, bs = _gmlp_spatial(p['gmlp_w_s'][l], p['gmlp_b_s'][l], dec_seq)
    lw['wsp'] = jnp.stack([wp, ws])
    lw['bsp'] = jnp.stack([bp, bs])
    lw['mix_g'] = p['mix_norm_g'][l].reshape(1, -1)
    lw['w_out'] = p['w_out'][l].astype(BF16)
    lw['ln1_g'] = p['ln1_g'][l].reshape(1, D_MODEL)
    lw['ln1_b'] = p['ln1_b'][l].reshape(1, D_MODEL)
    rw = p['router_w'][l]
    lw['rw_hi'] = rw.astype(BF16)
    lw['rw_lo'] = (rw - lw['rw_hi'].astype(F32)).astype(BF16)
    lw['router_b'] = p['router_b'][l].reshape(1, N_EXPERTS)
    t = jnp.arange(TOKEN_BLOCK)
    lw['tri'] = (t[None, :] < t[:, None]).astype(BF16)
    e = jnp.arange(N_EXPERTS)
    lw['upper'] = (e[:, None] < e[None, :]).astype(BF16)
    lw['ln2_g'] = p['ln2_g'][l].reshape(1, D_MODEL)
    lw['ln2_b'] = p['ln2_b'][l].reshape(1, D_MODEL)
    return lw


def _moe(x1, info, counts, p, l, lw, alpha, n_prompt):
    n = x1.shape[0]
    eb = EXPERT_BLOCK
    tb = TOKEN_BLOCK
    nb = n // tb
    cnt = counts.reshape(nb, N_EXPERTS).astype(jnp.int32)
    seg = (cnt + SUBLANES - 1) // SUBLANES * SUBLANES
    local_start = jnp.cumsum(seg, axis=1) - seg
    run = jnp.cumsum(seg, axis=0) - seg
    total = jnp.sum(seg, axis=0)
    padded = (total + eb - 1) // eb * eb
    pad_end = jnp.cumsum(padded)
    pad_start = pad_end - padded
    n_blk = -(-(n * TOP_K + nb * N_EXPERTS * (SUBLANES - 1)) // eb) + N_EXPERTS
    n_used = (pad_end[-1] // eb).astype(jnp.int32)
    seg_tab = jnp.concatenate([
        jnp.stack([local_start, pad_start[None, :] + run, seg], axis=-1).reshape(-1),
        jnp.stack([pad_start + total, padded - total], axis=-1).reshape(-1),
        jnp.stack([pad_end[-1], n_blk - n_used])]).astype(jnp.int32)
    blk_start = jnp.arange(n_blk, dtype=jnp.int32) * eb
    blk_e = jnp.minimum(jnp.sum(blk_start[:, None] >= pad_end[None, :], axis=1), N_EXPERTS - 1)
    last_e = blk_e[jnp.maximum(n_used - 1, 0)]
    blk_e = jnp.where(jnp.arange(n_blk) < n_used, blk_e, last_e).astype(jnp.int32)
    filled_end = (pad_start + total)[blk_e]
    n_valid = jnp.where(jnp.arange(n_blk) < n_used, jnp.clip(filled_end - blk_start, 0, eb), 0).astype(jnp.int32)
    per_block = lambda a: jnp.transpose(a.reshape(nb, tb, TOP_K), (0, 2, 1))
    pos_t = per_block(info[:, 2 * TOP_K:3 * TOP_K].astype(jnp.int32))
    gate_t = per_block(info[:, :TOP_K])
    xb = _dispatch_call(seg_tab, x1, pos_t, gate_t, n_blk * eb)
    has_rows = jnp.where(padded > 0, jnp.arange(N_EXPERTS), 2 * N_EXPERTS)
    later = jnp.concatenate([lax.cummin(has_rows[::-1])[::-1][1:], jnp.array([2 * N_EXPERTS])])
    next_e = jnp.where(later < N_EXPERTS, later + l * N_EXPERTS, -1)[blk_e].astype(jnp.int32)
    depth = p['moe_w_gu'].shape[0]
    yb = _expert_call(blk_e + l * N_EXPERTS, n_valid, next_e, xb,
                      p['moe_w_gu'].reshape(depth * N_EXPERTS, D_MODEL, 2 * D_FF),
                      p['moe_b_gu'].reshape(depth * N_EXPERTS, 1, 2 * D_FF),
                      p['moe_w_down'].reshape(depth * N_EXPERTS, D_FF, D_MODEL),
                      p['moe_b_down'].reshape(depth * N_EXPERTS, 1, D_MODEL))
    return _combine_call(seg_tab, info, x1, lw['ln2_g'], lw['ln2_b'], yb, alpha, n_prompt)


def kernel(x_prompt, x_sample, cache_mla_latent, cache_mla_krope, state_s5_re, state_s5_im, w_in, q_a_norm_g, w_q_b, kv_a_norm_g, w_kv_b, s5_lam_re, s5_lam_im, s5_log_dt, s5_b_re, s5_b_im, s5_c_re, s5_c_im, s5_d, s5_w_glu, s5_b_glu, gmlp_norm_g, gmlp_norm_b, gmlp_w_s, gmlp_b_s, mix_norm_g, w_out, ln1_g, ln1_b, router_w, router_b, moe_w_gu, moe_b_gu, moe_w_down, moe_b_down, ln2_g, ln2_b):
    p = dict(w_in=w_in, q_a_norm_g=q_a_norm_g, w_q_b=w_q_b, kv_a_norm_g=kv_a_norm_g, w_kv_b=w_kv_b,
             s5_lam_re=s5_lam_re, s5_lam_im=s5_lam_im, s5_log_dt=s5_log_dt, s5_b_re=s5_b_re,
             s5_b_im=s5_b_im, s5_c_re=s5_c_re, s5_c_im=s5_c_im, s5_d=s5_d, s5_w_glu=s5_w_glu,
             s5_b_glu=s5_b_glu, gmlp_norm_g=gmlp_norm_g, gmlp_norm_b=gmlp_norm_b, gmlp_w_s=gmlp_w_s,
             gmlp_b_s=gmlp_b_s, mix_norm_g=mix_norm_g, w_out=w_out, ln1_g=ln1_g, ln1_b=ln1_b,
             router_w=router_w, router_b=router_b, moe_w_gu=moe_w_gu, moe_b_gu=moe_b_gu,
             moe_w_down=moe_w_down, moe_b_down=moe_b_down, ln2_g=ln2_g, ln2_b=ln2_b)
    depth = w_in.shape[0]
    bp, sp, _ = x_prompt.shape
    bs, ss, _ = x_sample.shape
    past = cache_mla_latent.shape[2]
    n_p, n_s = bp * sp, bs * ss
    assert sp % ATTN_BLOCK == 0 and sp % S5_BLOCK == 0 and sp % TOKEN_BLOCK == 0
    assert n_s % TOKEN_BLOCK == 0 and GMLP_CHUNK % ss == 0 and ss % SUBLANES == 0
    alpha = float((2 * depth) ** 0.25)

    pos = jnp.concatenate([jnp.tile(jnp.arange(sp, dtype=jnp.int32), bp),
                           jnp.tile(past + jnp.arange(ss, dtype=jnp.int32), bs)])
    cos, sin = _rope_tables(pos)
    xp, xs = x_prompt.reshape(n_p, D_MODEL), x_sample.reshape(n_s, D_MODEL)
    zero_state = jnp.zeros((bp, 1, S5_COLS), F32)

    outs = {k: [] for k in ('lat_p', 'kr_p', 'sre_p', 'sim_p', 'lat_s', 'kr_s', 'sre_s', 'sim_s', 'gv_s')}
    for l in range(depth):
        lw = _layer_weights(p, l, cos, sin, ss)
        q, k, v, lat, kr, u, gm, gv = _pre_call(xp, xs, lw)
        attn_p = _attn_call(q, k, v, bp, sp)
        attn_s = _attn_sample_call(q, k, v, cache_mla_latent[l], cache_mla_krope[l], lw['wkv'],
                                   n_p, bs, ss)
        ssm_p, sre_p, sim_p = _s5_call(u, zero_state, zero_state, lw, 0, bp, sp, S5_BLOCK, "s5_prompt")
        ssm_s, sre_s, sim_s = _s5_call(u, state_s5_re[l].reshape(bs, 1, S5_COLS).astype(F32),
                                       state_s5_im[l].reshape(bs, 1, S5_COLS).astype(F32),
                                       lw, n_p, bs, ss, ss, "s5_sample")
        x1, info, counts = _post_call(attn_p, attn_s, ssm_p, ssm_s, gm, xp, xs, lw, alpha)
        xp, xs = _moe(x1, info, counts, p, l, lw, alpha, n_p)

        outs['lat_p'].append(lat[:n_p].reshape(bp, sp, KV_LORA))
        outs['kr_p'].append(kr[:n_p].reshape(bp, sp, MLA_ROPE))
        outs['sre_p'].append(sre_p.reshape(bp, S5_GROUPS, S5_STATE))
        outs['sim_p'].append(sim_p.reshape(bp, S5_GROUPS, S5_STATE))
        outs['lat_s'].append(lat[n_p:].reshape(bs, ss, KV_LORA))
        outs['kr_s'].append(kr[n_p:].reshape(bs, ss, MLA_ROPE))
        outs['sre_s'].append(sre_s.reshape(bs, S5_GROUPS, S5_STATE))
        outs['sim_s'].append(sim_s.reshape(bs, S5_GROUPS, S5_STATE))
        outs['gv_s'].append(gv[n_p:].reshape(bs, ss, GMLP_WIDTH))

    st = lambda name: jnp.stack(outs[name])
    return (xp.reshape(bp, sp, D_MODEL), xs.reshape(bs, ss, D_MODEL),
            st('lat_p'), st('kr_p'), st('sre_p'), st('sim_p'),
            st('lat_s'), st('kr_s'), st('sre_s'), st('sim_s'), st('gv_s'))
```

```python
import functools
import math

import jax
import jax.numpy as jnp
import numpy as np
from jax import lax
from jax.experimental import pallas as pl
from jax.experimental.pallas import tpu as pltpu

F32 = jnp.float32
BF16 = jnp.bfloat16

D_MODEL = 1024
CHUNK = 64
MLA_HEADS = 4
MLA_NOPE = 128
MLA_ROPE = 64
MLA_V = 128
Q_LORA = 256
KV_LORA = 128
MLA_WIDTH = MLA_HEADS * MLA_V
MLA_SCALE = (MLA_NOPE + MLA_ROPE) ** -0.5
Q_SCALE = MLA_SCALE * math.log2(math.e)
ROPE_THETA = 10000.0
S5_GROUP = 16
S5_GROUPS = 16
S5_WIDTH = S5_GROUP * S5_GROUPS
S5_STATE = 64
S5_COLS = S5_GROUPS * S5_STATE
GMLP_HEADS = 4
GMLP_HEAD_DIM = 64
GMLP_WIDTH = GMLP_HEADS * GMLP_HEAD_DIM
GMLP_CHUNK = 128
N_EXPERTS = 32
TOP_K = 4
D_FF = D_MODEL
SWIGLU_LIMIT = 7.0
SWIGLU_ALPHA = 1.702
NORM_EPS = 1e-5

LANES = 128
SUBLANES = 8
TOKEN_BLOCK = 256
ATTN_BLOCK = 1024
ATTN_HEADS = 2
EXPERT_BLOCK = 256
S5_BLOCK = 256
SORT_ROWS = TOKEN_BLOCK * TOP_K + N_EXPERTS * SUBLANES
ROUTE_ROWS = 2 * SUBLANES
POST_ROWS = 128
GROUPED_WIDTH = D_MODEL // 2 + LANES
VMEM_LIMIT = 56 * 1024 * 1024

_C_Q = 0
_C_KV = _C_Q + Q_LORA
_C_KR = _C_KV + KV_LORA
_C_KRS = _C_KR + LANES
_C_S5 = _C_KRS + LANES
_C_G = _C_S5 + S5_WIDTH
IN_EXT = _C_G + 2 * GMLP_WIDTH
HEAD_SLAB = MLA_NOPE + LANES
QK_WIDTH = MLA_HEADS * HEAD_SLAB


def _cparams(sem, vmem=VMEM_LIMIT):
    return pltpu.CompilerParams(dimension_semantics=sem, vmem_limit_bytes=vmem)


def _dot(a, b):
    return jnp.dot(a, b, preferred_element_type=F32)


def _dot_nt(a, b):
    return lax.dot_general(a, b, (((1,), (1,)), ((), ())), preferred_element_type=F32)


def _split_bf16(x):
    hi = x.astype(BF16)
    lo = (x - hi.astype(F32)).astype(BF16)
    return hi, lo


def _rms(x, g):
    return x * lax.rsqrt(jnp.mean(x * x, axis=-1, keepdims=True) + NORM_EPS) * g


def _ln(x, g, b):
    xc = x - jnp.mean(x, axis=-1, keepdims=True)
    var = jnp.mean(xc * xc, axis=-1, keepdims=True)
    return xc * lax.rsqrt(var + NORM_EPS) * g + b


def _pre_kernel(xp_ref, xs_ref, win_ref, qg_ref, wq_ref, kvg_ref, wkv_ref, cos_ref, sin_ref,
                gg_ref, gb_ref, mavg_ref, wsp_ref, bsp_ref,
                q_ref, k_ref, v_ref, lat_ref, kr_ref, u_ref, gm_ref, gv_ref, *, n_prompt_blocks):
    is_prompt = pl.program_id(0) < n_prompt_blocks
    xb = jnp.where(is_prompt, xp_ref[...], xs_ref[...]).astype(BF16)
    proj = _dot(xb, win_ref[...])
    cos = cos_ref[...]
    sin = sin_ref[...]

    qa = _rms(proj[:, _C_Q:_C_Q + Q_LORA], qg_ref[...]).astype(BF16)
    qq = _dot(qa, wq_ref[...])
    for h in range(MLA_HEADS):
        c0 = h * HEAD_SLAB
        nope = qq[:, c0:c0 + MLA_NOPE]
        rope = (qq[:, c0 + MLA_NOPE:c0 + HEAD_SLAB] * cos
                + qq[:, QK_WIDTH + h * LANES:QK_WIDTH + (h + 1) * LANES] * sin)
        q_ref[:, c0:c0 + MLA_NOPE] = (nope * Q_SCALE).astype(BF16)
        q_ref[:, c0 + MLA_NOPE:c0 + HEAD_SLAB] = (rope * Q_SCALE).astype(BF16)

    lat = _rms(proj[:, _C_KV:_C_KV + KV_LORA], kvg_ref[...])
    lat_ref[...] = lat
    kv = _dot(lat.astype(BF16), wkv_ref[...])
    kr = proj[:, _C_KR:_C_KR + LANES] * cos + proj[:, _C_KRS:_C_KRS + LANES] * sin
    kr_ref[...] = kr[:, :MLA_ROPE]
    krb = kr.astype(BF16)
    for h in range(MLA_HEADS):
        c0 = h * HEAD_SLAB
        k_ref[:, c0:c0 + MLA_NOPE] = kv[:, h * 256:h * 256 + MLA_NOPE].astype(BF16)
        k_ref[:, c0 + MLA_NOPE:c0 + HEAD_SLAB] = krb
        v_ref[:, h * MLA_V:(h + 1) * MLA_V] = kv[:, h * 256 + MLA_NOPE:(h + 1) * 256].astype(BF16)

    u_ref[...] = proj[:, _C_S5:_C_S5 + S5_WIDTH]

    z = jax.nn.gelu(proj[:, _C_G:_C_G + 2 * GMLP_WIDTH])
    ug = z[:, :GMLP_WIDTH]
    vg = z[:, GMLP_WIDTH:]
    mavg = mavg_ref[...]
    hi, lo = _split_bf16(vg)
    xc = vg - (_dot(hi, mavg) + _dot(lo, mavg))
    hi, lo = _split_bf16(xc * xc)
    var = _dot(hi, mavg) + _dot(lo, mavg)
    vn = xc * lax.rsqrt(var + NORM_EPS) * gg_ref[...] + gb_ref[...]
    gv_ref[...] = vn
    lane = lax.broadcasted_iota(jnp.int32, (GMLP_CHUNK, GMLP_WIDTH), 1)
    wsp = wsp_ref[0]
    bsp = bsp_ref[0]
    for c in range(xp_ref.shape[0] // GMLP_CHUNK):
        r0 = c * GMLP_CHUNK
        vc = vn[r0:r0 + GMLP_CHUNK, :].astype(BF16)
        stack = jnp.concatenate(
            [jnp.where(lane // GMLP_HEAD_DIM == h, vc, jnp.zeros_like(vc))
             for h in range(GMLP_HEADS)], axis=0)
        mix = _dot(wsp, stack) + bsp
        gm_ref[r0:r0 + GMLP_CHUNK, :] = ug[r0:r0 + GMLP_CHUNK, :] * mix


def _split_rows(tb, npb):
    prompt = lambda w: pl.BlockSpec((tb, w), lambda i, *_: (jnp.minimum(i, npb - 1), 0))
    sample = lambda w: pl.BlockSpec((tb, w), lambda i, *_: (jnp.maximum(i - npb, 0), 0))
    return prompt, sample


def _pre_call(xp, xs, lw):
    tb = TOKEN_BLOCK
    n = xp.shape[0] + xs.shape[0]
    nb = n // tb
    n_prompt_blocks = xp.shape[0] // tb
    row = lambda w: pl.BlockSpec((tb, w), lambda i: (i, 0))
    prompt_row, sample_row = _split_rows(tb, n_prompt_blocks)
    full = lambda a: pl.BlockSpec(a.shape, lambda i: (0,) * a.ndim)
    variant = lambda i: (jnp.where(i < n_prompt_blocks, 0, 1), 0, 0)
    seq_blocks = lw['cos'].shape[0] // tb - (nb - n_prompt_blocks)
    rope_row = pl.BlockSpec((tb, LANES), lambda i: (
        jnp.where(i < n_prompt_blocks, i % seq_blocks, i - n_prompt_blocks + seq_blocks), 0))
    in_specs = [prompt_row(D_MODEL), sample_row(D_MODEL),
                full(lw['w_in']), full(lw['q_g']), full(lw['wq']), full(lw['kv_g']),
                full(lw['wkv']), rope_row, rope_row, full(lw['g_g']), full(lw['g_b']),
                full(lw['mavg']),
                pl.BlockSpec((1, GMLP_CHUNK, GMLP_HEADS * GMLP_CHUNK), variant),
                pl.BlockSpec((1, GMLP_CHUNK, GMLP_WIDTH), variant)]
    widths = [(QK_WIDTH, BF16), (QK_WIDTH, BF16), (MLA_WIDTH, BF16), (KV_LORA, F32),
              (MLA_ROPE, F32), (S5_WIDTH, F32), (GMLP_WIDTH, F32), (GMLP_WIDTH, F32)]
    return pl.pallas_call(
        functools.partial(_pre_kernel, n_prompt_blocks=n_prompt_blocks),
        grid=(nb,),
        in_specs=in_specs,
        out_specs=[row(w) for w, _ in widths],
        out_shape=[jax.ShapeDtypeStruct((n, w), dt) for w, dt in widths],
        compiler_params=_cparams(("parallel",)),
        name="pre",
    )(xp, xs, lw['w_in'], lw['q_g'], lw['wq'], lw['kv_g'], lw['wkv'], lw['cos'], lw['sin'],
      lw['g_g'], lw['g_b'], lw['mavg'], lw['wsp'], lw['bsp'])


def _attn_kernel(q_ref, k_ref, v_ref, o_ref):
    i = pl.program_id(2)
    bq = q_ref.shape[0]
    qs = [q_ref[:, h * HEAD_SLAB:(h + 1) * HEAD_SLAB] for h in range(ATTN_HEADS)]

    def step(h, r0, carry, mask):
        m, l, acc = carry
        s = _dot_nt(qs[h], k_ref[pl.ds(r0, bq), h * HEAD_SLAB:(h + 1) * HEAD_SLAB])
        if mask is not None:
            s = jnp.where(mask, s, -jnp.inf)
        m_new = jnp.maximum(m, jnp.max(s, axis=-1, keepdims=True))
        p = jnp.exp2(s - m_new)
        alpha = jnp.exp2(m - m_new)
        l = alpha * l + jnp.sum(p, axis=-1, keepdims=True)
        acc = alpha * acc + _dot(p.astype(BF16), v_ref[pl.ds(r0, bq), h * MLA_V:(h + 1) * MLA_V])
        return m_new, l, acc

    def body(j, carries):
        r0 = pl.multiple_of(j * bq, bq)
        return tuple(step(h, r0, carries[h], None) for h in range(ATTN_HEADS))

    init = (jnp.full((bq, 1), -jnp.inf, F32), jnp.zeros((bq, 1), F32), jnp.zeros((bq, MLA_V), F32))
    carries = lax.fori_loop(0, i, body, (init,) * ATTN_HEADS)
    r0 = pl.multiple_of(i * bq, bq)
    qc = lax.broadcasted_iota(jnp.int32, (bq, bq), 0) // CHUNK
    kc = lax.broadcasted_iota(jnp.int32, (bq, bq), 1) // CHUNK
    for h in range(ATTN_HEADS):
        m, l, acc = step(h, r0, carries[h], kc <= qc)
        o_ref[:, h * MLA_V:(h + 1) * MLA_V] = acc / l


def _attn_call(q, k, v, batch, seq):
    bq = ATTN_BLOCK
    nq = seq // bq
    nh = ATTN_HEADS
    return pl.pallas_call(
        _attn_kernel,
        grid=(batch, MLA_HEADS // nh, nq),
        in_specs=[pl.BlockSpec((bq, nh * HEAD_SLAB), lambda b, h, i: (b * nq + i, h)),
                  pl.BlockSpec((seq, nh * HEAD_SLAB), lambda b, h, i: (b, h)),
                  pl.BlockSpec((seq, nh * MLA_V), lambda b, h, i: (b, h))],
        out_specs=pl.BlockSpec((bq, nh * MLA_V), lambda b, h, i: (b * nq + i, h)),
        out_shape=jax.ShapeDtypeStruct((batch * seq, MLA_WIDTH), F32),
        compiler_params=_cparams(("parallel", "parallel", "arbitrary")),
        name="attn_prompt",
    )(q, k, v)


def _attn_sample_kernel(q_ref, k_ref, v_ref, plat_ref, pkr_ref, wkv_ref, o_ref, *, past):
    t = q_ref.shape[0]
    q = q_ref[...]
    kn = k_ref[...]
    vn = v_ref[...]
    kvp = _dot(plat_ref[0].astype(BF16), wkv_ref[...]).astype(BF16)
    krp = pkr_ref[0].astype(BF16)
    q_chunk = (past + lax.broadcasted_iota(jnp.int32, (t, 1), 0)) // CHUNK
    mask_p = lax.broadcasted_iota(jnp.int32, (t, past), 1) // CHUNK <= q_chunk
    mask_n = (past + lax.broadcasted_iota(jnp.int32, (t, t), 1)) // CHUNK <= q_chunk
    for h in range(MLA_HEADS):
        c0 = h * HEAD_SLAB
        qh = q[:, c0:c0 + HEAD_SLAB]
        s_p = (_dot_nt(qh[:, :MLA_NOPE], kvp[:, h * 256:h * 256 + MLA_NOPE])
               + _dot_nt(qh[:, MLA_NOPE:MLA_NOPE + MLA_ROPE], krp))
        s_n = _dot_nt(qh, kn[:, c0:c0 + HEAD_SLAB])
        s_p = jnp.where(mask_p, s_p, -jnp.inf)
        s_n = jnp.where(mask_n, s_n, -jnp.inf)
        m = jnp.maximum(jnp.max(s_p, axis=-1, keepdims=True), jnp.max(s_n, axis=-1, keepdims=True))
        p_p = jnp.exp2(s_p - m)
        p_n = jnp.exp2(s_n - m)
        l = jnp.sum(p_p, axis=-1, keepdims=True) + jnp.sum(p_n, axis=-1, keepdims=True)
        o = (_dot(p_p.astype(BF16), kvp[:, h * 256 + MLA_NOPE:(h + 1) * 256])
             + _dot(p_n.astype(BF16), vn[:, h * MLA_V:(h + 1) * MLA_V]))
        o_ref[:, h * MLA_V:(h + 1) * MLA_V] = o / l


def _attn_sample_call(q, k, v, cache_lat, cache_kr, layer, wkv, row0, batch, t):
    past = cache_lat.shape[2]
    past_lat = cache_lat.reshape(-1, past, KV_LORA)
    past_kr = cache_kr.reshape(-1, past, MLA_ROPE)
    lb0 = layer * batch
    blk0 = row0 // t
    return pl.pallas_call(
        functools.partial(_attn_sample_kernel, past=past),
        grid=(batch,),
        in_specs=[pl.BlockSpec((t, QK_WIDTH), lambda b: (blk0 + b, 0)),
                  pl.BlockSpec((t, QK_WIDTH), lambda b: (blk0 + b, 0)),
                  pl.BlockSpec((t, MLA_WIDTH), lambda b: (blk0 + b, 0)),
                  pl.BlockSpec((1, past, KV_LORA), lambda b: (lb0 + b, 0, 0)),
                  pl.BlockSpec((1, past, MLA_ROPE), lambda b: (lb0 + b, 0, 0)),
                  pl.BlockSpec(wkv.shape, lambda b: (0, 0))],
        out_specs=pl.BlockSpec((t, MLA_WIDTH), lambda b: (b, 0)),
        out_shape=jax.ShapeDtypeStruct((batch * t, MLA_WIDTH), F32),
        compiler_params=_cparams(("parallel",)),
        name="attn_sample",
    )(q, k, v, past_lat, past_kr, wkv)


def _s5_kernel(u_ref, s0r_ref, s0i_ref, bb_ref, tab_ref, cc_ref, d_ref, wg_ref, bg_ref,
               y_ref, sr_ref, si_ref, st_ref, cr_ref, ci_ref):
    tb = u_ref.shape[0]
    nc = S5_COLS

    @pl.when(pl.program_id(1) == 0)
    def _():
        cr_ref[...] = jnp.broadcast_to(s0r_ref[0], (SUBLANES, nc))
        ci_ref[...] = jnp.broadcast_to(s0i_ref[0], (SUBLANES, nc))

    u = u_ref[...]
    st_ref[...] = _dot(u.astype(BF16), bb_ref[...])

    def tile(r, carry):
        car, cai = carry
        r0 = pl.multiple_of(r * SUBLANES, SUBLANES)
        xr = st_ref[pl.ds(r0, SUBLANES), :nc]
        xi = st_ref[pl.ds(r0, SUBLANES), nc:]
        for si, d in enumerate((1, 2, 4)):
            pr = tab_ref[si, 0]
            pi = tab_ref[si, 1]
            sr = pltpu.roll(xr, d, 0)
            sim = pltpu.roll(xi, d, 0)
            xr, xi = xr + (pr * sr - pi * sim), xi + (pr * sim + pi * sr)
        pr = tab_ref[3, 0]
        pi = tab_ref[3, 1]
        xr, xi = xr + (pr * car - pi * cai), xi + (pr * cai + pi * car)
        st_ref[pl.ds(r0, SUBLANES), :nc] = xr
        st_ref[pl.ds(r0, SUBLANES), nc:] = xi
        return (jnp.broadcast_to(xr[SUBLANES - 1:SUBLANES, :], (SUBLANES, nc)),
                jnp.broadcast_to(xi[SUBLANES - 1:SUBLANES, :], (SUBLANES, nc)))

    car, cai = lax.fori_loop(0, tb // SUBLANES, tile, (cr_ref[...], ci_ref[...]))
    cr_ref[...] = car
    ci_ref[...] = cai
    sr_ref[0] = car[0:1, :]
    si_ref[0] = cai[0:1, :]

    y = _dot(st_ref[...].astype(BF16), cc_ref[...]) + d_ref[...] * u
    z = jax.nn.gelu(y)
    y_ref[...] = z * jax.nn.sigmoid(_dot(z.astype(BF16), wg_ref[...]) + bg_ref[...])


def _s5_call(u, s0r, s0i, lw, row0, batch, seq, tblk, name):
    nt = seq // tblk
    blk0 = row0 // tblk
    full = lambda a: pl.BlockSpec(a.shape, lambda b, t: (0,) * a.ndim)
    st_spec = pl.BlockSpec((1, 1, S5_COLS), lambda b, t: (b, 0, 0))
    return pl.pallas_call(
        _s5_kernel,
        grid=(batch, nt),
        in_specs=[pl.BlockSpec((tblk, S5_WIDTH), lambda b, t: (blk0 + b * nt + t, 0)),
                  st_spec, st_spec,
                  full(lw['s5_bb']), full(lw['s5_tab']), full(lw['s5_cc']), full(lw['s5_d']),
                  full(lw['s5_wg']), full(lw['s5_bg'])],
        out_specs=[pl.BlockSpec((tblk, S5_WIDTH), lambda b, t: (b * nt + t, 0)), st_spec, st_spec],
        out_shape=[jax.ShapeDtypeStruct((batch * seq, S5_WIDTH), F32),
                   jax.ShapeDtypeStruct((batch, 1, S5_COLS), F32),
                   jax.ShapeDtypeStruct((batch, 1, S5_COLS), F32)],
        scratch_shapes=[pltpu.VMEM((tblk, 2 * S5_COLS), F32),
                        pltpu.VMEM((SUBLANES, S5_COLS), F32),
                        pltpu.VMEM((SUBLANES, S5_COLS), F32)],
        compiler_params=_cparams(("parallel", "arbitrary")),
        name=name,
    )(u, s0r, s0i, lw['s5_bb'], lw['s5_tab'], lw['s5_cc'], lw['s5_d'], lw['s5_wg'], lw['s5_bg'])


def _post_kernel(ap_ref, as_ref, sp_ref, ss_ref, g_ref, xp_ref, xs_ref, mg_ref, wo_ref, l1g_ref,
                 l1b_ref, rwh_ref, rwl_ref, rb_ref, tri_ref, lower_ref,
                 x1_ref, info_ref, rt_ref, cnt_ref, *, alpha, n_prompt_blocks):
    tb = xp_ref.shape[0]
    mg = mg_ref[...]
    is_prompt = pl.program_id(0) < n_prompt_blocks
    rwh = rwh_ref[...]
    rwl = rwl_ref[...]
    eidx = lax.broadcasted_iota(jnp.int32, (N_EXPERTS, POST_ROWS), 0).astype(F32)

    gates, hots = [], []
    for r0 in range(0, tb, POST_ROWS):
        sl = slice(r0, r0 + POST_ROWS)
        na = _rms(jnp.where(is_prompt, ap_ref[sl, :], as_ref[sl, :]), mg[:, :MLA_WIDTH]).astype(BF16)
        ns = _rms(jnp.where(is_prompt, sp_ref[sl, :], ss_ref[sl, :]),
                  mg[:, MLA_WIDTH:MLA_WIDTH + S5_WIDTH]).astype(BF16)
        ng = _rms(g_ref[sl, :], mg[:, MLA_WIDTH + S5_WIDTH:]).astype(BF16)
        mixed = (_dot(na, wo_ref[:MLA_WIDTH, :]) + _dot(ns, wo_ref[MLA_WIDTH:MLA_WIDTH + S5_WIDTH, :])
                 + _dot(ng, wo_ref[MLA_WIDTH + S5_WIDTH:, :]))
        x = jnp.where(is_prompt, xp_ref[sl, :], xs_ref[sl, :])
        x1 = _ln(alpha * x + mixed, l1g_ref[...], l1b_ref[...])
        x1_ref[sl, :] = x1

        hi, lo = _split_bf16(x1)
        work = _dot_nt(rwh, hi) + (_dot_nt(rwh, lo) + _dot_nt(rwl, hi)) + rb_ref[...]
        tops, part_hots = [], []
        for _ in range(TOP_K):
            m = jnp.max(work, axis=0, keepdims=True)
            sel = jnp.min(jnp.where(work == m, eidx, float(N_EXPERTS)), axis=0, keepdims=True)
            hot = eidx == sel
            tops.append(m)
            part_hots.append(hot)
            work = jnp.where(hot, -jnp.inf, work)
        exps = [jnp.exp(t - tops[0]) for t in tops]
        den = exps[0] + exps[1] + exps[2] + exps[3]
        gates.append([e / den for e in exps])
        hots.append(part_hots)

    cnt = jnp.concatenate(
        [sum(h.astype(F32) for h in part_hots[1:]) + part_hots[0].astype(F32) for part_hots in hots],
        axis=1)
    tot = jnp.sum(cnt, axis=1, keepdims=True)
    seg = jnp.ceil(tot * (1.0 / SUBLANES))
    seg_l = jnp.broadcast_to(seg, (N_EXPERTS, LANES)).astype(BF16)
    start = _dot(lower_ref[...], seg_l)[:, 0:1] * float(SUBLANES)
    before = _dot(cnt.astype(BF16), tri_ref[...]) + start
    row = lax.broadcasted_iota(jnp.int32, (LANES, POST_ROWS), 0)
    for part, r0 in enumerate(range(0, tb, POST_ROWS)):
        slab = jnp.zeros((LANES, POST_ROWS), F32)
        for k in range(TOP_K):
            pos = jnp.sum(jnp.where(hots[part][k], before[:, r0:r0 + POST_ROWS], 0.0),
                          axis=0, keepdims=True)
            slab = jnp.where(row == k, gates[part][k], slab)
            slab = jnp.where(row == 2 * TOP_K + k, pos, slab)
        rt_ref[0, :, r0:r0 + POST_ROWS] = slab[:ROUTE_ROWS, :]
        info_ref[r0:r0 + POST_ROWS, :] = slab.T
    cnt_ref[0] = tot


def _post_call(attn_p, attn_s, ssm_p, ssm_s, gm, xp, xs, lw, alpha):
    n = xp.shape[0] + xs.shape[0]
    tb = TOKEN_BLOCK
    npb = xp.shape[0] // tb
    row = lambda w: pl.BlockSpec((tb, w), lambda i: (i, 0))
    prompt_row, sample_row = _split_rows(tb, npb)
    full = lambda a: pl.BlockSpec(a.shape, lambda i: (0,) * a.ndim)
    names = ['mix_g', 'w_out', 'ln1_g', 'ln1_b', 'rw_hi', 'rw_lo', 'router_b', 'tri', 'lower']
    return pl.pallas_call(
        functools.partial(_post_kernel, alpha=alpha, n_prompt_blocks=npb),
        grid=(n // tb,),
        in_specs=[prompt_row(MLA_WIDTH), sample_row(MLA_WIDTH), prompt_row(S5_WIDTH), sample_row(S5_WIDTH),
                  row(GMLP_WIDTH), prompt_row(D_MODEL), sample_row(D_MODEL)]
                 + [full(lw[k]) for k in names],
        out_specs=[row(D_MODEL), row(LANES),
                   pl.BlockSpec((1, ROUTE_ROWS, tb), lambda i: (i, 0, 0)),
                   pl.BlockSpec((1, N_EXPERTS, 1), lambda i: (i, 0, 0))],
        out_shape=[jax.ShapeDtypeStruct((n, D_MODEL), F32),
                   jax.ShapeDtypeStruct((n, LANES), F32),
                   jax.ShapeDtypeStruct((n // tb, ROUTE_ROWS, tb), F32),
                   jax.ShapeDtypeStruct((n // tb, N_EXPERTS, 1), F32)],
        compiler_params=_cparams(("parallel",)),
        name="post",
    )(attn_p, attn_s, ssm_p, ssm_s, gm, xp, xs, *[lw[k] for k in names])


def _segment_copies(tab_ref, step, src, dst, src_is_local, sem):
    copies = []
    for e in range(N_EXPERTS):
        base = (step * N_EXPERTS + e) * 3
        loc = pl.multiple_of(tab_ref[base], SUBLANES)
        glo = pl.multiple_of(tab_ref[base + 1], SUBLANES)
        n = pl.multiple_of(tab_ref[base + 2], SUBLANES)
        s_at, d_at = (loc, glo) if src_is_local else (glo, loc)
        copies.append((n, pltpu.make_async_copy(src.at[pl.ds(s_at, n), :], dst.at[pl.ds(d_at, n), :], sem)))
    return copies


def _start_copies(copies):
    for n, cp in copies:
        @pl.when(n > 0)
        def _(cp=cp):
            cp.start()


def _wait_copies(copies):
    for n, cp in copies:
        @pl.when(n > 0)
        def _(cp=cp):
            cp.wait()


def _dispatch_kernel(tab_ref, x_ref, rt_ref, xb_ref, sbuf, zbuf, sem):
    step = pl.program_id(0)
    slot = step % 2
    tb = x_ref.shape[0]
    rows = lax.broadcasted_iota(jnp.int32, (SORT_ROWS, tb), 0)
    hit = None
    gates = jnp.zeros((SORT_ROWS, tb), F32)
    for k in range(TOP_K):
        eq = rows == rt_ref[0, 2 * TOP_K + k:2 * TOP_K + k + 1, :].astype(jnp.int32)
        hit = eq if hit is None else jnp.logical_or(hit, eq)
        gates = jnp.where(eq, rt_ref[0, k:k + 1, :], gates)
    perm = jnp.where(hit, 1.0, 0.0).astype(BF16)
    xs = _dot(perm, x_ref[...].astype(BF16))
    half = D_MODEL // 2
    sbuf[slot, :, :half] = pltpu.pack_elementwise([xs[:, :half], xs[:, half:]], packed_dtype=BF16)
    row_gate = jnp.broadcast_to(jnp.sum(gates, axis=-1, keepdims=True), (SORT_ROWS, LANES))
    gate_hi = row_gate.astype(BF16).astype(F32)
    sbuf[slot, :, half:] = pltpu.pack_elementwise([gate_hi, row_gate - gate_hi], packed_dtype=BF16)

    def copies(s):
        return _segment_copies(tab_ref, s, sbuf.at[s % 2], xb_ref, True, sem.at[s % 2])

    n_steps = pl.num_programs(0)
    fill_base = n_steps * N_EXPERTS * 3
    fill_sem = sem.at[2]

    def pad_fills():
        out = []
        for e in range(N_EXPERTS):
            start = pl.multiple_of(tab_ref[fill_base + 2 * e], SUBLANES)
            n = pl.multiple_of(tab_ref[fill_base + 2 * e + 1], SUBLANES)
            out.append((n, pltpu.make_async_copy(zbuf.at[pl.ds(0, n), :], xb_ref.at[pl.ds(start, n), :], fill_sem)))
        return out

    def block_fill(j):
        row = pl.multiple_of(tab_ref[fill_base + 2 * N_EXPERTS] + j * EXPERT_BLOCK, EXPERT_BLOCK)
        return pltpu.make_async_copy(zbuf, xb_ref.at[pl.ds(row, EXPERT_BLOCK), :], fill_sem)

    n_unused = tab_ref[fill_base + 2 * N_EXPERTS + 1]

    @pl.when(step == 0)
    def _():
        zbuf[...] = jnp.zeros_like(zbuf)
        _start_copies(pad_fills())
        lax.fori_loop(0, n_unused, lambda j, c: (block_fill(j).start(), c)[1], 0)

    @pl.when(step > 0)
    def _():
        _wait_copies(copies(step - 1))

    _start_copies(copies(step))

    @pl.when(step == n_steps - 1)
    def _():
        _wait_copies(copies(step))
        _wait_copies(pad_fills())
        lax.fori_loop(0, n_unused, lambda j, c: (block_fill(j).wait(), c)[1], 0)


def _dispatch_call(seg_tab, x1, route, n_rows):
    n = x1.shape[0]
    tb = TOKEN_BLOCK
    grid_spec = pltpu.PrefetchScalarGridSpec(
        num_scalar_prefetch=1,
        grid=(n // tb,),
        in_specs=[pl.BlockSpec((tb, D_MODEL), lambda i, t: (i, 0)),
                  pl.BlockSpec((1, ROUTE_ROWS, tb), lambda i, t: (i, 0, 0))],
        out_specs=pl.BlockSpec(memory_space=pl.ANY),
        scratch_shapes=[pltpu.VMEM((2, SORT_ROWS, GROUPED_WIDTH), jnp.uint32),
                        pltpu.VMEM((EXPERT_BLOCK, GROUPED_WIDTH), jnp.uint32),
                        pltpu.SemaphoreType.DMA((3,))],
    )
    return pl.pallas_call(
        _dispatch_kernel,
        grid_spec=grid_spec,
        out_shape=jax.ShapeDtypeStruct((n_rows, GROUPED_WIDTH), jnp.uint32),
        compiler_params=_cparams(("arbitrary",)),
        name="dispatch",
    )(seg_tab, x1, route)


def _expert_kernel(be_ref, nv_ref, nx_ref, x_ref, bgu_ref, bdn_ref, wgu_hbm, wdn_hbm, y_ref,
                   wgu_st, wdn_st, wgu_bf, wdn_bf, sem):
    i = pl.program_id(0)
    e = be_ref[i]
    changed = jnp.logical_or(i == 0, be_ref[jnp.maximum(i - 1, 0)] != e)

    def fetch(ex):
        return (pltpu.make_async_copy(wgu_hbm.at[ex], wgu_st, sem.at[0]),
                pltpu.make_async_copy(wdn_hbm.at[ex], wdn_st, sem.at[1]))

    @pl.when(i == 0)
    def _():
        for cp in fetch(e):
            cp.start()

    @pl.when(changed)
    def _():
        for cp in fetch(e):
            cp.wait()
        wgu_bf[...] = wgu_st[...].astype(BF16)
        wdn_bf[...] = wdn_st[...].astype(BF16)

        @pl.when(nx_ref[i] >= 0)
        def _():
            for cp in fetch(nx_ref[i]):
                cp.start()

    @pl.when(nv_ref[i] > 0)
    def _():
        live = lax.broadcasted_iota(jnp.int32, (x_ref.shape[0], 1), 0) < nv_ref[i]
        half = D_MODEL // 2
        halves = lambda words: [jnp.where(live, pltpu.unpack_elementwise(
            words, index=idx, packed_dtype=BF16, unpacked_dtype=F32), 0.0) for idx in (0, 1)]
        x_lo, x_hi = halves(x_ref[:, :half])
        g_hi, g_rest = halves(x_ref[:, half:half + LANES])
        row_gate = (g_hi + g_rest)[:, 0:1]
        gu = (_dot(x_lo.astype(BF16), wgu_bf[:half, :]) + _dot(x_hi.astype(BF16), wgu_bf[half:, :])
              + bgu_ref[0])
        gate = jnp.minimum(gu[:, :D_FF], SWIGLU_LIMIT)
        up = jnp.clip(gu[:, D_FF:], -SWIGLU_LIMIT, SWIGLU_LIMIT)
        h = (up + 1.0) * (gate * jax.nn.sigmoid(SWIGLU_ALPHA * gate))
        y_ref[...] = (_dot(h.astype(BF16), wdn_bf[...]) + bdn_ref[0]) * row_gate

    @pl.when(nv_ref[i] == 0)
    def _():
        y_ref[...] = jnp.zeros_like(y_ref)


def _expert_call(blk_e, n_valid, next_e, xb, wgu, bgu, wdn, bdn):
    n_rows = xb.shape[0]
    eb = EXPERT_BLOCK
    grid_spec = pltpu.PrefetchScalarGridSpec(
        num_scalar_prefetch=3,
        grid=(n_rows // eb,),
        in_specs=[pl.BlockSpec((eb, GROUPED_WIDTH), lambda i, be, nv, nx: (i, 0)),
                  pl.BlockSpec((1, 1, 2 * D_FF), lambda i, be, nv, nx: (be[i], 0, 0)),
                  pl.BlockSpec((1, 1, D_MODEL), lambda i, be, nv, nx: (be[i], 0, 0)),
                  pl.BlockSpec(memory_space=pl.ANY),
                  pl.BlockSpec(memory_space=pl.ANY)],
        out_specs=pl.BlockSpec((eb, D_MODEL), lambda i, be, nv, nx: (i, 0)),
        scratch_shapes=[pltpu.VMEM((D_MODEL, 2 * D_FF), F32), pltpu.VMEM((D_FF, D_MODEL), F32),
                        pltpu.VMEM((D_MODEL, 2 * D_FF), BF16), pltpu.VMEM((D_FF, D_MODEL), BF16),
                        pltpu.SemaphoreType.DMA((2,))],
    )
    return pl.pallas_call(
        _expert_kernel,
        grid_spec=grid_spec,
        out_shape=jax.ShapeDtypeStruct((n_rows, D_MODEL), F32),
        compiler_params=_cparams(("arbitrary",)),
        name="experts",
    )(blk_e, n_valid, next_e, xb, bgu, bdn, wgu, wdn)


def _combine_kernel(tab_ref, info_ref, x1_ref, g_ref, b_ref, yb_ref, op_ref, os_ref, buf, sem, *,
                    alpha, n_prompt_blocks):
    tb = x1_ref.shape[0]
    step = pl.program_id(0)
    slot = step % 2
    tail = tb * TOP_K

    def fetch(s):
        buf[s % 2, tail:, :] = jnp.zeros((SORT_ROWS - tail, D_MODEL), F32)
        _start_copies(_segment_copies(tab_ref, s, yb_ref, buf.at[s % 2], False, sem.at[s % 2]))

    @pl.when(step == 0)
    def _():
        fetch(step)

    @pl.when(step + 1 < pl.num_programs(0))
    def _():
        fetch(step + 1)

    _wait_copies(_segment_copies(tab_ref, step, yb_ref, buf.at[slot], False, sem.at[slot]))
    info = info_ref[...]
    cols = lax.broadcasted_iota(jnp.int32, (tb, SORT_ROWS), 1).astype(F32)
    hit = None
    for k in range(TOP_K):
        eq = cols == info[:, 2 * TOP_K + k:2 * TOP_K + k + 1]
        hit = eq if hit is None else jnp.logical_or(hit, eq)
    unsort = jnp.where(hit, 1.0, 0.0).astype(BF16)
    hi, lo = _split_bf16(buf[slot])
    moe = _dot(unsort, hi) + _dot(unsort, lo)
    out = _ln(alpha * x1_ref[...] + moe, g_ref[...], b_ref[...])

    @pl.when(step < n_prompt_blocks)
    def _():
        op_ref[...] = out

    @pl.when(step >= n_prompt_blocks)
    def _():
        os_ref[...] = out


def _combine_call(seg_tab, info, x1, ln_g, ln_b, yb, alpha, n_prompt):
    n = x1.shape[0]
    tb = TOKEN_BLOCK
    npb = n_prompt // tb
    prompt_row, sample_row = _split_rows(tb, npb)
    grid_spec = pltpu.PrefetchScalarGridSpec(
        num_scalar_prefetch=1,
        grid=(n // tb,),
        in_specs=[pl.BlockSpec((tb, LANES), lambda i, t: (i, 0)),
                  pl.BlockSpec((tb, D_MODEL), lambda i, t: (i, 0)),
                  pl.BlockSpec((1, D_MODEL), lambda i, t: (0, 0)),
                  pl.BlockSpec((1, D_MODEL), lambda i, t: (0, 0)),
                  pl.BlockSpec(memory_space=pl.ANY)],
        out_specs=[prompt_row(D_MODEL), sample_row(D_MODEL)],
        scratch_shapes=[pltpu.VMEM((2, SORT_ROWS, D_MODEL), F32), pltpu.SemaphoreType.DMA((2,))],
    )
    return pl.pallas_call(
        functools.partial(_combine_kernel, alpha=alpha, n_prompt_blocks=npb),
        grid_spec=grid_spec,
        out_shape=[jax.ShapeDtypeStruct((n_prompt, D_MODEL), F32),
                   jax.ShapeDtypeStruct((n - n_prompt, D_MODEL), F32)],
        compiler_params=_cparams(("arbitrary",)),
        name="combine",
    )(seg_tab, info, x1, ln_g, ln_b, yb)


def _rope_tables(pos):
    half = MLA_ROPE // 2
    inv_freq = ROPE_THETA ** (-jnp.arange(half, dtype=F32) / half)
    ang = pos.astype(F32)[:, None] * inv_freq[None, :]
    cos, sin = jnp.cos(ang), jnp.sin(ang)
    zero = jnp.zeros((pos.shape[0], LANES - MLA_ROPE), F32)
    return (jnp.concatenate([cos, cos, zero], axis=1), jnp.concatenate([-sin, sin, zero], axis=1))


def _s5_params(lam_re, lam_im, log_dt, b_re, b_im, c_re, c_im):
    dt = jnp.exp(log_dt)[:, None]
    mag = jnp.exp(lam_re * dt)
    ab_re = mag * jnp.cos(lam_im * dt)
    ab_im = mag * jnp.sin(lam_im * dt)
    den = lam_re * lam_re + lam_im * lam_im
    nr, ni = ab_re - 1.0, ab_im
    f_re = (nr * lam_re + ni * lam_im) / den
    f_im = (ni * lam_re - nr * lam_im) / den
    bb_re = f_re[..., None] * b_re - f_im[..., None] * b_im
    bb_im = f_re[..., None] * b_im + f_im[..., None] * b_re
    eye = jnp.eye(S5_GROUPS, dtype=F32)

    def in_blockdiag(w):
        return jnp.einsum('gnc,gh->gchn', w, eye).reshape(S5_WIDTH, S5_COLS)

    def out_blockdiag(w):
        return jnp.einsum('gcn,gh->gnhc', w, eye).reshape(S5_COLS, S5_WIDTH)

    bb = jnp.concatenate([in_blockdiag(bb_re), in_blockdiag(bb_im)], axis=1).astype(BF16)
    cc = jnp.concatenate([out_blockdiag(c_re), -out_blockdiag(c_im)], axis=0).astype(BF16)
    ar, ai = ab_re.reshape(1, S5_COLS), ab_im.reshape(1, S5_COLS)
    pr, pi = [ar], [ai]
    for _ in range(SUBLANES - 1):
        pr, pi = pr + [pr[-1] * ar - pi[-1] * ai], pi + [pr[-1] * ai + pi[-1] * ar]
    rows = jnp.arange(SUBLANES)[:, None]
    tabs = []
    for d in (1, 2, 4):
        keep = rows >= d
        tabs.append(jnp.stack([jnp.where(keep, pr[d - 1], 0.0), jnp.where(keep, pi[d - 1], 0.0)]))
    tabs.append(jnp.stack([jnp.concatenate(pr, axis=0), jnp.concatenate(pi, axis=0)]))
    return bb, cc, jnp.stack(tabs)


def _gmlp_spatial(w_s, b_s, chunk_len):
    reps = GMLP_CHUNK // chunk_len
    i = jnp.arange(chunk_len)
    mask = (i[None, :] // CHUNK) <= (i[:, None] // CHUNK)
    w = jnp.where(mask[None], w_s[:, :chunk_len, :chunk_len], 0.0)
    eye = jnp.eye(reps, dtype=F32)
    wblk = jnp.einsum('hij,rs->hrisj', w, eye).reshape(GMLP_HEADS, GMLP_CHUNK, GMLP_CHUNK)
    wcat = jnp.transpose(wblk, (1, 0, 2)).reshape(GMLP_CHUNK, GMLP_HEADS * GMLP_CHUNK)
    bias = jnp.tile(b_s[:, :chunk_len], (1, reps))
    bias = jnp.repeat(bias.T, GMLP_HEAD_DIM, axis=1)
    return wcat.astype(BF16), bias


def _layer_weights(p, l, cos, sin, dec_seq):
    lw = {'cos': cos, 'sin': sin}
    w_in = p['w_in'][l]
    kr = w_in[:, _C_KR:_C_KR + MLA_ROPE]
    half = MLA_ROPE // 2
    zero = jnp.zeros((D_MODEL, LANES - MLA_ROPE), F32)
    rest = w_in[:, _C_KR + MLA_ROPE:]
    lw['w_in'] = jnp.concatenate(
        [w_in[:, :_C_KR], kr, zero, kr[:, half:], kr[:, :half], zero, rest], axis=1).astype(BF16)
    wq = p['w_q_b'][l].reshape(Q_LORA, MLA_HEADS, MLA_NOPE + MLA_ROPE)
    zq = jnp.zeros((Q_LORA, MLA_HEADS, LANES - MLA_ROPE), F32)
    rope = wq[:, :, MLA_NOPE:]
    plain = jnp.concatenate([wq, zq], axis=2).reshape(Q_LORA, QK_WIDTH)
    swapped = jnp.concatenate([rope[:, :, half:], rope[:, :, :half], zq], axis=2)
    lw['wq'] = jnp.concatenate([plain, swapped.reshape(Q_LORA, MLA_HEADS * LANES)], axis=1).astype(BF16)
    lw['q_g'] = p['q_a_norm_g'][l].reshape(1, Q_LORA)
    lw['kv_g'] = p['kv_a_norm_g'][l].reshape(1, KV_LORA)
    lw['wkv'] = p['w_kv_b'][l].astype(BF16)
    lw['s5_bb'], lw['s5_cc'], lw['s5_tab'] = _s5_params(
        p['s5_lam_re'][l], p['s5_lam_im'][l], p['s5_log_dt'][l], p['s5_b_re'][l], p['s5_b_im'][l],
        p['s5_c_re'][l], p['s5_c_im'][l])
    lw['s5_d'] = p['s5_d'][l].reshape(1, S5_WIDTH)
    lw['s5_wg'] = p['s5_w_glu'][l].astype(BF16)
    lw['s5_bg'] = p['s5_b_glu'][l].reshape(1, S5_WIDTH)
    lw['g_g'] = p['gmlp_norm_g'][l].reshape(1, GMLP_WIDTH)
    lw['g_b'] = p['gmlp_norm_b'][l].reshape(1, GMLP_WIDTH)
    grp = jnp.arange(GMLP_WIDTH) // GMLP_HEAD_DIM
    lw['mavg'] = jnp.where(grp[:, None] == grp[None, :], 1.0 / GMLP_HEAD_DIM, 0.0).astype(BF16)
    wp, bp = _gmlp_spatial(p['gmlp_w_s'][l], p['gmlp_b_s'][l], GMLP_CHUNK)
    ws, bs = _gmlp_spatial(p['gmlp_w_s'][l], p['gmlp_b_s'][l], dec_seq)
    lw['wsp'] = jnp.stack([wp, ws])
    lw['bsp'] = jnp.stack([bp, bs])
    lw['mix_g'] = p['mix_norm_g'][l].reshape(1, -1)
    lw['w_out'] = p['w_out'][l].astype(BF16)
    lw['ln1_g'] = p['ln1_g'][l].reshape(1, D_MODEL)
    lw['ln1_b'] = p['ln1_b'][l].reshape(1, D_MODEL)
    rw = p['router_w'][l].T
    lw['rw_hi'] = rw.astype(BF16)
    lw['rw_lo'] = (rw - lw['rw_hi'].astype(F32)).astype(BF16)
    lw['router_b'] = p['router_b'][l].reshape(N_EXPERTS, 1)
    t = jnp.arange(TOKEN_BLOCK)
    lw['tri'] = (t[:, None] < t[None, :]).astype(BF16)
    e = jnp.arange(N_EXPERTS)
    lw['lower'] = (e[None, :] < e[:, None]).astype(BF16)
    lw['ln2_g'] = p['ln2_g'][l].reshape(1, D_MODEL)
    lw['ln2_b'] = p['ln2_b'][l].reshape(1, D_MODEL)
    return lw


def _moe(x1, info, route, counts, p, l, lw, alpha, n_prompt):
    n = x1.shape[0]
    eb = EXPERT_BLOCK
    tb = TOKEN_BLOCK
    nb = n // tb
    cnt = counts.reshape(nb, N_EXPERTS).astype(jnp.int32)
    seg = (cnt + SUBLANES - 1) // SUBLANES * SUBLANES
    local_start = jnp.cumsum(seg, axis=1) - seg
    run = jnp.cumsum(seg, axis=0) - seg
    total = jnp.sum(seg, axis=0)
    padded = (total + eb - 1) // eb * eb
    pad_end = jnp.cumsum(padded)
    pad_start = pad_end - padded
    n_blk = -(-(n * TOP_K + nb * N_EXPERTS * (SUBLANES - 1)) // eb) + N_EXPERTS
    n_used = (pad_end[-1] // eb).astype(jnp.int32)
    seg_tab = jnp.concatenate([
        jnp.stack([local_start, pad_start[None, :] + run, seg], axis=-1).reshape(-1),
        jnp.stack([pad_start + total, padded - total], axis=-1).reshape(-1),
        jnp.stack([pad_end[-1], n_blk - n_used])]).astype(jnp.int32)
    experts = jnp.arange(N_EXPERTS, dtype=jnp.int32)
    blk_start = jnp.arange(n_blk, dtype=jnp.int32)[:, None] * eb
    member = (blk_start >= pad_start[None, :]) & (blk_start < pad_end[None, :])
    used = jnp.any(member, axis=1)
    pick = lambda per_expert: jnp.sum(jnp.where(member, per_expert[None, :], 0), axis=1)
    last_e = jnp.max(jnp.where(padded > 0, experts, 0))
    blk_e = jnp.where(used, pick(experts), last_e).astype(jnp.int32)
    n_valid = jnp.sum(jnp.where(member, jnp.clip((pad_start + total)[None, :] - blk_start, 0, eb), 0),
                      axis=1).astype(jnp.int32)
    has_rows = jnp.where(padded > 0, experts, 2 * N_EXPERTS)
    later = jnp.concatenate([lax.cummin(has_rows, reverse=True)[1:],
                             jnp.full((1,), 2 * N_EXPERTS, jnp.int32)])
    next_e = jnp.where(used, pick(jnp.where(later < N_EXPERTS, later + l * N_EXPERTS, -1)), -1).astype(jnp.int32)
    xb = _dispatch_call(seg_tab, x1, route, n_blk * eb)
    depth = p['moe_w_gu'].shape[0]
    yb = _expert_call(blk_e + l * N_EXPERTS, n_valid, next_e, xb,
                      p['moe_w_gu'].reshape(depth * N_EXPERTS, D_MODEL, 2 * D_FF),
                      p['moe_b_gu'].reshape(depth * N_EXPERTS, 1, 2 * D_FF),
                      p['moe_w_down'].reshape(depth * N_EXPERTS, D_FF, D_MODEL),
                      p['moe_b_down'].reshape(depth * N_EXPERTS, 1, D_MODEL))
    return _combine_call(seg_tab, info, x1, lw['ln2_g'], lw['ln2_b'], yb, alpha, n_prompt)


def kernel(x_prompt, x_sample, cache_mla_latent, cache_mla_krope, state_s5_re, state_s5_im, w_in, q_a_norm_g, w_q_b, kv_a_norm_g, w_kv_b, s5_lam_re, s5_lam_im, s5_log_dt, s5_b_re, s5_b_im, s5_c_re, s5_c_im, s5_d, s5_w_glu, s5_b_glu, gmlp_norm_g, gmlp_norm_b, gmlp_w_s, gmlp_b_s, mix_norm_g, w_out, ln1_g, ln1_b, router_w, router_b, moe_w_gu, moe_b_gu, moe_w_down, moe_b_down, ln2_g, ln2_b):
    p = dict(w_in=w_in, q_a_norm_g=q_a_norm_g, w_q_b=w_q_b, kv_a_norm_g=kv_a_norm_g, w_kv_b=w_kv_b,
             s5_lam_re=s5_lam_re, s5_lam_im=s5_lam_im, s5_log_dt=s5_log_dt, s5_b_re=s5_b_re,
             s5_b_im=s5_b_im, s5_c_re=s5_c_re, s5_c_im=s5_c_im, s5_d=s5_d, s5_w_glu=s5_w_glu,
             s5_b_glu=s5_b_glu, gmlp_norm_g=gmlp_norm_g, gmlp_norm_b=gmlp_norm_b, gmlp_w_s=gmlp_w_s,
             gmlp_b_s=gmlp_b_s, mix_norm_g=mix_norm_g, w_out=w_out, ln1_g=ln1_g, ln1_b=ln1_b,
             router_w=router_w, router_b=router_b, moe_w_gu=moe_w_gu, moe_b_gu=moe_b_gu,
             moe_w_down=moe_w_down, moe_b_down=moe_b_down, ln2_g=ln2_g, ln2_b=ln2_b)
    depth = w_in.shape[0]
    bp, sp, _ = x_prompt.shape
    bs, ss, _ = x_sample.shape
    past = cache_mla_latent.shape[2]
    n_p, n_s = bp * sp, bs * ss
    assert sp % ATTN_BLOCK == 0 and sp % S5_BLOCK == 0 and sp % TOKEN_BLOCK == 0
    assert n_s % TOKEN_BLOCK == 0 and GMLP_CHUNK % ss == 0 and ss % SUBLANES == 0
    alpha = float((2 * depth) ** 0.25)

    pos = jnp.concatenate([jnp.arange(sp, dtype=jnp.int32),
                           jnp.tile(past + jnp.arange(ss, dtype=jnp.int32), bs)])
    cos, sin = _rope_tables(pos)
    xp, xs = x_prompt.reshape(n_p, D_MODEL), x_sample.reshape(n_s, D_MODEL)
    zero_state = jnp.zeros((bp, 1, S5_COLS), F32)

    outs = {k: [] for k in ('lat_p', 'kr_p', 'sre_p', 'sim_p', 'lat_s', 'kr_s', 'sre_s', 'sim_s', 'gv_s')}
    for l in range(depth):
        lw = _layer_weights(p, l, cos, sin, ss)
        q, k, v, lat, kr, u, gm, gv = _pre_call(xp, xs, lw)
        attn_p = _attn_call(q, k, v, bp, sp)
        attn_s = _attn_sample_call(q, k, v, cache_mla_latent, cache_mla_krope, l, lw['wkv'],
                                   n_p, bs, ss)
        ssm_p, sre_p, sim_p = _s5_call(u, zero_state, zero_state, lw, 0, bp, sp, S5_BLOCK, "s5_prompt")
        ssm_s, sre_s, sim_s = _s5_call(u, state_s5_re[l].reshape(bs, 1, S5_COLS).astype(F32),
                                       state_s5_im[l].reshape(bs, 1, S5_COLS).astype(F32),
                                       lw, n_p, bs, ss, ss, "s5_sample")
        x1, info, route, counts = _post_call(attn_p, attn_s, ssm_p, ssm_s, gm, xp, xs, lw, alpha)
        xp, xs = _moe(x1, info, route, counts, p, l, lw, alpha, n_p)

        outs['lat_p'].append(lat[:n_p].reshape(bp, sp, KV_LORA))
        outs['kr_p'].append(kr[:n_p].reshape(bp, sp, MLA_ROPE))
        outs['sre_p'].append(sre_p.reshape(bp, S5_GROUPS, S5_STATE))
        outs['sim_p'].append(sim_p.reshape(bp, S5_GROUPS, S5_STATE))
        outs['lat_s'].append(lat[n_p:].reshape(bs, ss, KV_LORA))
        outs['kr_s'].append(kr[n_p:].reshape(bs, ss, MLA_ROPE))
        outs['sre_s'].append(sre_s.reshape(bs, S5_GROUPS, S5_STATE))
        outs['sim_s'].append(sim_s.reshape(bs, S5_GROUPS, S5_STATE))
        outs['gv_s'].append(gv[n_p:].reshape(bs, ss, GMLP_WIDTH))

    st = lambda name: jnp.stack(outs[name])
    return (xp.reshape(bp, sp, D_MODEL), xs.reshape(bs, ss, D_MODEL),
            st('lat_p'), st('kr_p'), st('sre_p'), st('sim_p'),
            st('lat_s'), st('kr_s'), st('sre_s'), st('sim_s'), st('gv_s'))
```

```python
import functools
import math

import jax
import jax.numpy as jnp
import numpy as np
from jax import lax
from jax.experimental import pallas as pl
from jax.experimental.pallas import tpu as pltpu

F32 = jnp.float32
BF16 = jnp.bfloat16

D_MODEL = 1024
CHUNK = 64
MLA_HEADS = 4
MLA_NOPE = 128
MLA_ROPE = 64
MLA_V = 128
Q_LORA = 256
KV_LORA = 128
MLA_WIDTH = MLA_HEADS * MLA_V
MLA_SCALE = (MLA_NOPE + MLA_ROPE) ** -0.5
Q_SCALE = MLA_SCALE * math.log2(math.e)
ROPE_THETA = 10000.0
S5_GROUP = 16
S5_GROUPS = 16
S5_WIDTH = S5_GROUP * S5_GROUPS
S5_STATE = 64
S5_COLS = S5_GROUPS * S5_STATE
GMLP_HEADS = 4
GMLP_HEAD_DIM = 64
GMLP_WIDTH = GMLP_HEADS * GMLP_HEAD_DIM
GMLP_CHUNK = 128
N_EXPERTS = 32
TOP_K = 4
D_FF = D_MODEL
SWIGLU_LIMIT = 7.0
SWIGLU_ALPHA = 1.702
NORM_EPS = 1e-5

LANES = 128
SUBLANES = 8
TOKEN_BLOCK = 256
ATTN_BLOCK = 1024
ATTN_HEADS = 2
EXPERT_BLOCK = 256
S5_BLOCK = 1024
SORT_ROWS = TOKEN_BLOCK * TOP_K + N_EXPERTS * SUBLANES
ROUTE_ROWS = 2 * SUBLANES
POST_ROWS = 128
GROUPED_WIDTH = D_MODEL // 2 + LANES
VMEM_LIMIT = 56 * 1024 * 1024

_C_Q = 0
_C_KV = _C_Q + Q_LORA
_C_KR = _C_KV + KV_LORA
_C_KRS = _C_KR + LANES
_C_S5 = _C_KRS + LANES
_C_G = _C_S5 + S5_WIDTH
IN_EXT = _C_G + 2 * GMLP_WIDTH
HEAD_SLAB = MLA_NOPE + LANES
QK_WIDTH = MLA_HEADS * HEAD_SLAB


def _cparams(sem, vmem=VMEM_LIMIT):
    return pltpu.CompilerParams(dimension_semantics=sem, vmem_limit_bytes=vmem)


def _dot(a, b):
    return jnp.dot(a, b, preferred_element_type=F32)


def _dot_nt(a, b):
    return lax.dot_general(a, b, (((1,), (1,)), ((), ())), preferred_element_type=F32)


def _split_bf16(x):
    hi = x.astype(BF16)
    lo = (x - hi.astype(F32)).astype(BF16)
    return hi, lo


def _rms(x, g):
    return x * lax.rsqrt(jnp.mean(x * x, axis=-1, keepdims=True) + NORM_EPS) * g


def _ln(x, g, b):
    xc = x - jnp.mean(x, axis=-1, keepdims=True)
    var = jnp.mean(xc * xc, axis=-1, keepdims=True)
    return xc * lax.rsqrt(var + NORM_EPS) * g + b


def _pre_kernel(xp_ref, xs_ref, win_ref, qg_ref, wq_ref, kvg_ref, wkv_ref, cos_ref, sin_ref,
                gg_ref, gb_ref, mavg_ref, wsp_ref, bsp_ref,
                q_ref, k_ref, v_ref, lat_ref, kr_ref, u_ref, gm_ref, gv_ref, *, n_prompt_blocks):
    is_prompt = pl.program_id(0) < n_prompt_blocks
    xb = jnp.where(is_prompt, xp_ref[...], xs_ref[...]).astype(BF16)
    proj = _dot(xb, win_ref[...])
    cos = cos_ref[...]
    sin = sin_ref[...]

    qa = _rms(proj[:, _C_Q:_C_Q + Q_LORA], qg_ref[...]).astype(BF16)
    qq = _dot(qa, wq_ref[...])
    for h in range(MLA_HEADS):
        c0 = h * HEAD_SLAB
        nope = qq[:, c0:c0 + MLA_NOPE]
        rope = (qq[:, c0 + MLA_NOPE:c0 + HEAD_SLAB] * cos
                + qq[:, QK_WIDTH + h * LANES:QK_WIDTH + (h + 1) * LANES] * sin)
        q_ref[:, c0:c0 + MLA_NOPE] = (nope * Q_SCALE).astype(BF16)
        q_ref[:, c0 + MLA_NOPE:c0 + HEAD_SLAB] = (rope * Q_SCALE).astype(BF16)

    lat = _rms(proj[:, _C_KV:_C_KV + KV_LORA], kvg_ref[...])
    lat_ref[...] = lat
    kv = _dot(lat.astype(BF16), wkv_ref[...])
    kr = proj[:, _C_KR:_C_KR + LANES] * cos + proj[:, _C_KRS:_C_KRS + LANES] * sin
    kr_ref[...] = kr[:, :MLA_ROPE]
    krb = kr.astype(BF16)
    for h in range(MLA_HEADS):
        c0 = h * HEAD_SLAB
        k_ref[:, c0:c0 + MLA_NOPE] = kv[:, h * 256:h * 256 + MLA_NOPE].astype(BF16)
        k_ref[:, c0 + MLA_NOPE:c0 + HEAD_SLAB] = krb
        v_ref[:, h * MLA_V:(h + 1) * MLA_V] = kv[:, h * 256 + MLA_NOPE:(h + 1) * 256].astype(BF16)

    u_ref[...] = proj[:, _C_S5:_C_S5 + S5_WIDTH]

    z = jax.nn.gelu(proj[:, _C_G:_C_G + 2 * GMLP_WIDTH])
    ug = z[:, :GMLP_WIDTH]
    vg = z[:, GMLP_WIDTH:]
    mavg = mavg_ref[...]
    hi, lo = _split_bf16(vg)
    xc = vg - (_dot(hi, mavg) + _dot(lo, mavg))
    hi, lo = _split_bf16(xc * xc)
    var = _dot(hi, mavg) + _dot(lo, mavg)
    vn = xc * lax.rsqrt(var + NORM_EPS) * gg_ref[...] + gb_ref[...]
    gv_ref[...] = vn
    lane = lax.broadcasted_iota(jnp.int32, (GMLP_CHUNK, GMLP_WIDTH), 1)
    wsp = wsp_ref[0]
    bsp = bsp_ref[0]
    for c in range(xp_ref.shape[0] // GMLP_CHUNK):
        r0 = c * GMLP_CHUNK
        vc = vn[r0:r0 + GMLP_CHUNK, :].astype(BF16)
        stack = jnp.concatenate(
            [jnp.where(lane // GMLP_HEAD_DIM == h, vc, jnp.zeros_like(vc))
             for h in range(GMLP_HEADS)], axis=0)
        mix = _dot(wsp, stack) + bsp
        gm_ref[r0:r0 + GMLP_CHUNK, :] = ug[r0:r0 + GMLP_CHUNK, :] * mix


def _split_rows(tb, npb):
    prompt = lambda w: pl.BlockSpec((tb, w), lambda i, *_: (jnp.minimum(i, npb - 1), 0))
    sample = lambda w: pl.BlockSpec((tb, w), lambda i, *_: (jnp.maximum(i - npb, 0), 0))
    return prompt, sample


def _pre_call(xp, xs, lw):
    tb = TOKEN_BLOCK
    n = xp.shape[0] + xs.shape[0]
    nb = n // tb
    n_prompt_blocks = xp.shape[0] // tb
    row = lambda w: pl.BlockSpec((tb, w), lambda i: (i, 0))
    prompt_row, sample_row = _split_rows(tb, n_prompt_blocks)
    full = lambda a: pl.BlockSpec(a.shape, lambda i: (0,) * a.ndim)
    variant = lambda i: (jnp.where(i < n_prompt_blocks, 0, 1), 0, 0)
    seq_blocks = lw['cos'].shape[0] // tb - (nb - n_prompt_blocks)
    rope_row = pl.BlockSpec((tb, LANES), lambda i: (
        jnp.where(i < n_prompt_blocks, i % seq_blocks, i - n_prompt_blocks + seq_blocks), 0))
    in_specs = [prompt_row(D_MODEL), sample_row(D_MODEL),
                full(lw['w_in']), full(lw['q_g']), full(lw['wq']), full(lw['kv_g']),
                full(lw['wkv']), rope_row, rope_row, full(lw['g_g']), full(lw['g_b']),
                full(lw['mavg']),
                pl.BlockSpec((1, GMLP_CHUNK, GMLP_HEADS * GMLP_CHUNK), variant),
                pl.BlockSpec((1, GMLP_CHUNK, GMLP_WIDTH), variant)]
    widths = [(QK_WIDTH, BF16), (QK_WIDTH, BF16), (MLA_WIDTH, BF16), (KV_LORA, F32),
              (MLA_ROPE, F32), (S5_WIDTH, F32), (GMLP_WIDTH, F32), (GMLP_WIDTH, F32)]
    return pl.pallas_call(
        functools.partial(_pre_kernel, n_prompt_blocks=n_prompt_blocks),
        grid=(nb,),
        in_specs=in_specs,
        out_specs=[row(w) for w, _ in widths],
        out_shape=[jax.ShapeDtypeStruct((n, w), dt) for w, dt in widths],
        compiler_params=_cparams(("parallel",)),
        name="pre",
    )(xp, xs, lw['w_in'], lw['q_g'], lw['wq'], lw['kv_g'], lw['wkv'], lw['cos'], lw['sin'],
      lw['g_g'], lw['g_b'], lw['mavg'], lw['wsp'], lw['bsp'])


def _attn_kernel(q_ref, k_ref, v_ref, o_ref):
    i = pl.program_id(2)
    bq = q_ref.shape[0]
    qs = [q_ref[:, h * HEAD_SLAB:(h + 1) * HEAD_SLAB] for h in range(ATTN_HEADS)]

    def step(h, r0, carry, mask):
        m, l, acc = carry
        s = _dot_nt(qs[h], k_ref[pl.ds(r0, bq), h * HEAD_SLAB:(h + 1) * HEAD_SLAB])
        if mask is not None:
            s = jnp.where(mask, s, -jnp.inf)
        m_new = jnp.maximum(m, jnp.max(s, axis=-1, keepdims=True))
        p = jnp.exp2(s - m_new)
        alpha = jnp.exp2(m - m_new)
        l = alpha * l + jnp.sum(p, axis=-1, keepdims=True)
        acc = alpha * acc + _dot(p.astype(BF16), v_ref[pl.ds(r0, bq), h * MLA_V:(h + 1) * MLA_V])
        return m_new, l, acc

    def body(j, carries):
        r0 = pl.multiple_of(j * bq, bq)
        return tuple(step(h, r0, carries[h], None) for h in range(ATTN_HEADS))

    init = (jnp.full((bq, 1), -jnp.inf, F32), jnp.zeros((bq, 1), F32), jnp.zeros((bq, MLA_V), F32))
    carries = lax.fori_loop(0, i, body, (init,) * ATTN_HEADS)
    r0 = pl.multiple_of(i * bq, bq)
    qc = lax.broadcasted_iota(jnp.int32, (bq, bq), 0) // CHUNK
    kc = lax.broadcasted_iota(jnp.int32, (bq, bq), 1) // CHUNK
    for h in range(ATTN_HEADS):
        m, l, acc = step(h, r0, carries[h], kc <= qc)
        o_ref[:, h * MLA_V:(h + 1) * MLA_V] = acc / l


def _attn_call(q, k, v, batch, seq):
    bq = ATTN_BLOCK
    nq = seq // bq
    nh = ATTN_HEADS
    return pl.pallas_call(
        _attn_kernel,
        grid=(batch, MLA_HEADS // nh, nq),
        in_specs=[pl.BlockSpec((bq, nh * HEAD_SLAB), lambda b, h, i: (b * nq + i, h)),
                  pl.BlockSpec((seq, nh * HEAD_SLAB), lambda b, h, i: (b, h)),
                  pl.BlockSpec((seq, nh * MLA_V), lambda b, h, i: (b, h))],
        out_specs=pl.BlockSpec((bq, nh * MLA_V), lambda b, h, i: (b * nq + i, h)),
        out_shape=jax.ShapeDtypeStruct((batch * seq, MLA_WIDTH), F32),
        compiler_params=_cparams(("parallel", "parallel", "arbitrary")),
        name="attn_prompt",
    )(q, k, v)


def _attn_sample_kernel(q_ref, k_ref, v_ref, plat_ref, pkr_ref, wkv_ref, o_ref, *, past):
    t = q_ref.shape[0]
    q = q_ref[...]
    kn = k_ref[...]
    vn = v_ref[...]
    kvp = _dot(plat_ref[0].astype(BF16), wkv_ref[...]).astype(BF16)
    krp = pkr_ref[0].astype(BF16)
    q_chunk = (past + lax.broadcasted_iota(jnp.int32, (t, 1), 0)) // CHUNK
    mask_p = lax.broadcasted_iota(jnp.int32, (t, past), 1) // CHUNK <= q_chunk
    mask_n = (past + lax.broadcasted_iota(jnp.int32, (t, t), 1)) // CHUNK <= q_chunk
    for h in range(MLA_HEADS):
        c0 = h * HEAD_SLAB
        qh = q[:, c0:c0 + HEAD_SLAB]
        s_p = (_dot_nt(qh[:, :MLA_NOPE], kvp[:, h * 256:h * 256 + MLA_NOPE])
               + _dot_nt(qh[:, MLA_NOPE:MLA_NOPE + MLA_ROPE], krp))
        s_n = _dot_nt(qh, kn[:, c0:c0 + HEAD_SLAB])
        s_p = jnp.where(mask_p, s_p, -jnp.inf)
        s_n = jnp.where(mask_n, s_n, -jnp.inf)
        m = jnp.maximum(jnp.max(s_p, axis=-1, keepdims=True), jnp.max(s_n, axis=-1, keepdims=True))
        p_p = jnp.exp2(s_p - m)
        p_n = jnp.exp2(s_n - m)
        l = jnp.sum(p_p, axis=-1, keepdims=True) + jnp.sum(p_n, axis=-1, keepdims=True)
        o = (_dot(p_p.astype(BF16), kvp[:, h * 256 + MLA_NOPE:(h + 1) * 256])
             + _dot(p_n.astype(BF16), vn[:, h * MLA_V:(h + 1) * MLA_V]))
        o_ref[:, h * MLA_V:(h + 1) * MLA_V] = o / l


def _attn_sample_call(q, k, v, cache_lat, cache_kr, layer, wkv, row0, batch, t):
    past = cache_lat.shape[2]
    past_lat = cache_lat.reshape(-1, past, KV_LORA)
    past_kr = cache_kr.reshape(-1, past, MLA_ROPE)
    lb0 = layer * batch
    blk0 = row0 // t
    return pl.pallas_call(
        functools.partial(_attn_sample_kernel, past=past),
        grid=(batch,),
        in_specs=[pl.BlockSpec((t, QK_WIDTH), lambda b: (blk0 + b, 0)),
                  pl.BlockSpec((t, QK_WIDTH), lambda b: (blk0 + b, 0)),
                  pl.BlockSpec((t, MLA_WIDTH), lambda b: (blk0 + b, 0)),
                  pl.BlockSpec((1, past, KV_LORA), lambda b: (lb0 + b, 0, 0)),
                  pl.BlockSpec((1, past, MLA_ROPE), lambda b: (lb0 + b, 0, 0)),
                  pl.BlockSpec(wkv.shape, lambda b: (0, 0))],
        out_specs=pl.BlockSpec((t, MLA_WIDTH), lambda b: (b, 0)),
        out_shape=jax.ShapeDtypeStruct((batch * t, MLA_WIDTH), F32),
        compiler_params=_cparams(("parallel",)),
        name="attn_sample",
    )(q, k, v, past_lat, past_kr, wkv)


def _s5_kernel(u_ref, s0r_ref, s0i_ref, bb_ref, tab_ref, cc_ref, d_ref, wg_ref, bg_ref,
               y_ref, sr_ref, si_ref, st_ref, cr_ref, ci_ref):
    tb = u_ref.shape[0]
    nc = S5_COLS

    @pl.when(pl.program_id(1) == 0)
    def _():
        cr_ref[...] = jnp.broadcast_to(s0r_ref[0], (SUBLANES, nc))
        ci_ref[...] = jnp.broadcast_to(s0i_ref[0], (SUBLANES, nc))

    u = u_ref[...]
    st_ref[...] = _dot(u.astype(BF16), bb_ref[...])

    def tile(r, carry):
        car, cai = carry
        r0 = pl.multiple_of(r * SUBLANES, SUBLANES)
        xr = st_ref[pl.ds(r0, SUBLANES), :nc]
        xi = st_ref[pl.ds(r0, SUBLANES), nc:]
        for si, d in enumerate((1, 2, 4)):
            pr = tab_ref[si, 0]
            pi = tab_ref[si, 1]
            sr = pltpu.roll(xr, d, 0)
            sim = pltpu.roll(xi, d, 0)
            xr, xi = xr + (pr * sr - pi * sim), xi + (pr * sim + pi * sr)
        pr = tab_ref[3, 0]
        pi = tab_ref[3, 1]
        xr, xi = xr + (pr * car - pi * cai), xi + (pr * cai + pi * car)
        st_ref[pl.ds(r0, SUBLANES), :nc] = xr
        st_ref[pl.ds(r0, SUBLANES), nc:] = xi
        return (jnp.broadcast_to(xr[SUBLANES - 1:SUBLANES, :], (SUBLANES, nc)),
                jnp.broadcast_to(xi[SUBLANES - 1:SUBLANES, :], (SUBLANES, nc)))

    car, cai = lax.fori_loop(0, tb // SUBLANES, tile, (cr_ref[...], ci_ref[...]))
    cr_ref[...] = car
    ci_ref[...] = cai
    sr_ref[0] = car[0:1, :]
    si_ref[0] = cai[0:1, :]

    y = _dot(st_ref[...].astype(BF16), cc_ref[...]) + d_ref[...] * u
    z = jax.nn.gelu(y)
    y_ref[...] = z * jax.nn.sigmoid(_dot(z.astype(BF16), wg_ref[...]) + bg_ref[...])


def _s5_call(u, s0r, s0i, lw, row0, batch, seq, tblk, name):
    nt = seq // tblk
    blk0 = row0 // tblk
    full = lambda a: pl.BlockSpec(a.shape, lambda b, t: (0,) * a.ndim)
    st_spec = pl.BlockSpec((1, 1, S5_COLS), lambda b, t: (b, 0, 0))
    return pl.pallas_call(
        _s5_kernel,
        grid=(batch, nt),
        in_specs=[pl.BlockSpec((tblk, S5_WIDTH), lambda b, t: (blk0 + b * nt + t, 0)),
                  st_spec, st_spec,
                  full(lw['s5_bb']), full(lw['s5_tab']), full(lw['s5_cc']), full(lw['s5_d']),
                  full(lw['s5_wg']), full(lw['s5_bg'])],
        out_specs=[pl.BlockSpec((tblk, S5_WIDTH), lambda b, t: (b * nt + t, 0)), st_spec, st_spec],
        out_shape=[jax.ShapeDtypeStruct((batch * seq, S5_WIDTH), F32),
                   jax.ShapeDtypeStruct((batch, 1, S5_COLS), F32),
                   jax.ShapeDtypeStruct((batch, 1, S5_COLS), F32)],
        scratch_shapes=[pltpu.VMEM((tblk, 2 * S5_COLS), F32),
                        pltpu.VMEM((SUBLANES, S5_COLS), F32),
                        pltpu.VMEM((SUBLANES, S5_COLS), F32)],
        compiler_params=_cparams(("parallel", "arbitrary")),
        name=name,
    )(u, s0r, s0i, lw['s5_bb'], lw['s5_tab'], lw['s5_cc'], lw['s5_d'], lw['s5_wg'], lw['s5_bg'])


def _post_kernel(ap_ref, as_ref, sp_ref, ss_ref, g_ref, xp_ref, xs_ref, mg_ref, wo_ref, l1g_ref,
                 l1b_ref, rwh_ref, rwl_ref, rb_ref, tri_ref, lower_ref,
                 x1_ref, info_ref, rt_ref, cnt_ref, *, alpha, n_prompt_blocks):
    tb = xp_ref.shape[0]
    mg = mg_ref[...]
    is_prompt = pl.program_id(0) < n_prompt_blocks
    rwh = rwh_ref[...]
    rwl = rwl_ref[...]
    eidx = lax.broadcasted_iota(jnp.int32, (N_EXPERTS, POST_ROWS), 0).astype(F32)

    gates, hots = [], []
    for r0 in range(0, tb, POST_ROWS):
        sl = slice(r0, r0 + POST_ROWS)
        na = _rms(jnp.where(is_prompt, ap_ref[sl, :], as_ref[sl, :]), mg[:, :MLA_WIDTH]).astype(BF16)
        ns = _rms(jnp.where(is_prompt, sp_ref[sl, :], ss_ref[sl, :]),
                  mg[:, MLA_WIDTH:MLA_WIDTH + S5_WIDTH]).astype(BF16)
        ng = _rms(g_ref[sl, :], mg[:, MLA_WIDTH + S5_WIDTH:]).astype(BF16)
        mixed = (_dot(na, wo_ref[:MLA_WIDTH, :]) + _dot(ns, wo_ref[MLA_WIDTH:MLA_WIDTH + S5_WIDTH, :])
                 + _dot(ng, wo_ref[MLA_WIDTH + S5_WIDTH:, :]))
        x = jnp.where(is_prompt, xp_ref[sl, :], xs_ref[sl, :])
        x1 = _ln(alpha * x + mixed, l1g_ref[...], l1b_ref[...])
        x1_ref[sl, :] = x1

        hi, lo = _split_bf16(x1)
        work = _dot_nt(rwh, hi) + (_dot_nt(rwh, lo) + _dot_nt(rwl, hi)) + rb_ref[...]
        tops, part_hots = [], []
        for _ in range(TOP_K):
            m = jnp.max(work, axis=0, keepdims=True)
            sel = jnp.min(jnp.where(work == m, eidx, float(N_EXPERTS)), axis=0, keepdims=True)
            hot = eidx == sel
            tops.append(m)
            part_hots.append(hot)
            work = jnp.where(hot, -jnp.inf, work)
        exps = [jnp.exp(t - tops[0]) for t in tops]
        den = exps[0] + exps[1] + exps[2] + exps[3]
        gates.append([e / den for e in exps])
        hots.append(part_hots)

    cnt = jnp.concatenate(
        [sum(h.astype(F32) for h in part_hots[1:]) + part_hots[0].astype(F32) for part_hots in hots],
        axis=1)
    tot = jnp.sum(cnt, axis=1, keepdims=True)
    seg = jnp.ceil(tot * (1.0 / SUBLANES))
    seg_l = jnp.broadcast_to(seg, (N_EXPERTS, LANES)).astype(BF16)
    start = _dot(lower_ref[...], seg_l)[:, 0:1] * float(SUBLANES)
    before = _dot(cnt.astype(BF16), tri_ref[...]) + start
    row = lax.broadcasted_iota(jnp.int32, (LANES, POST_ROWS), 0)
    for part, r0 in enumerate(range(0, tb, POST_ROWS)):
        slab = jnp.zeros((LANES, POST_ROWS), F32)
        for k in range(TOP_K):
            pos = jnp.sum(jnp.where(hots[part][k], before[:, r0:r0 + POST_ROWS], 0.0),
                          axis=0, keepdims=True)
            slab = jnp.where(row == k, gates[part][k], slab)
            slab = jnp.where(row == 2 * TOP_K + k, pos, slab)
        rt_ref[0, :, r0:r0 + POST_ROWS] = slab[:ROUTE_ROWS, :]
        info_ref[r0:r0 + POST_ROWS, :] = slab.T
    cnt_ref[0] = tot


def _post_call(attn_p, attn_s, ssm_p, ssm_s, gm, xp, xs, lw, alpha):
    n = xp.shape[0] + xs.shape[0]
    tb = TOKEN_BLOCK
    npb = xp.shape[0] // tb
    row = lambda w: pl.BlockSpec((tb, w), lambda i: (i, 0))
    prompt_row, sample_row = _split_rows(tb, npb)
    full = lambda a: pl.BlockSpec(a.shape, lambda i: (0,) * a.ndim)
    names = ['mix_g', 'w_out', 'ln1_g', 'ln1_b', 'rw_hi', 'rw_lo', 'router_b', 'tri', 'lower']
    return pl.pallas_call(
        functools.partial(_post_kernel, alpha=alpha, n_prompt_blocks=npb),
        grid=(n // tb,),
        in_specs=[prompt_row(MLA_WIDTH), sample_row(MLA_WIDTH), prompt_row(S5_WIDTH), sample_row(S5_WIDTH),
                  row(GMLP_WIDTH), prompt_row(D_MODEL), sample_row(D_MODEL)]
                 + [full(lw[k]) for k in names],
        out_specs=[row(D_MODEL), row(LANES),
                   pl.BlockSpec((1, ROUTE_ROWS, tb), lambda i: (i, 0, 0)),
                   pl.BlockSpec((1, N_EXPERTS, 1), lambda i: (i, 0, 0))],
        out_shape=[jax.ShapeDtypeStruct((n, D_MODEL), F32),
                   jax.ShapeDtypeStruct((n, LANES), F32),
                   jax.ShapeDtypeStruct((n // tb, ROUTE_ROWS, tb), F32),
                   jax.ShapeDtypeStruct((n // tb, N_EXPERTS, 1), F32)],
        compiler_params=_cparams(("parallel",)),
        name="post",
    )(attn_p, attn_s, ssm_p, ssm_s, gm, xp, xs, *[lw[k] for k in names])


def _segment_copies(tab_ref, step, src, dst, src_is_local, sem):
    copies = []
    for e in range(N_EXPERTS):
        base = (step * N_EXPERTS + e) * 3
        loc = pl.multiple_of(tab_ref[base], SUBLANES)
        glo = pl.multiple_of(tab_ref[base + 1], SUBLANES)
        n = pl.multiple_of(tab_ref[base + 2], SUBLANES)
        s_at, d_at = (loc, glo) if src_is_local else (glo, loc)
        copies.append((n, pltpu.make_async_copy(src.at[pl.ds(s_at, n), :], dst.at[pl.ds(d_at, n), :], sem)))
    return copies


def _start_copies(copies):
    for n, cp in copies:
        @pl.when(n > 0)
        def _(cp=cp):
            cp.start()


def _wait_copies(copies):
    for n, cp in copies:
        @pl.when(n > 0)
        def _(cp=cp):
            cp.wait()


def _dispatch_kernel(tab_ref, x_ref, rt_ref, xb_ref, sbuf, zbuf, sem):
    step = pl.program_id(0)
    slot = step % 2
    tb = x_ref.shape[0]
    rows = lax.broadcasted_iota(jnp.int32, (SORT_ROWS, tb), 0)
    hit = None
    gates = jnp.zeros((SORT_ROWS, tb), F32)
    for k in range(TOP_K):
        eq = rows == rt_ref[0, 2 * TOP_K + k:2 * TOP_K + k + 1, :].astype(jnp.int32)
        hit = eq if hit is None else jnp.logical_or(hit, eq)
        gates = jnp.where(eq, rt_ref[0, k:k + 1, :], gates)
    perm = jnp.where(hit, 1.0, 0.0).astype(BF16)
    xs = _dot(perm, x_ref[...].astype(BF16))
    half = D_MODEL // 2
    sbuf[slot, :, :half] = pltpu.pack_elementwise([xs[:, :half], xs[:, half:]], packed_dtype=BF16)
    row_gate = jnp.broadcast_to(jnp.sum(gates, axis=-1, keepdims=True), (SORT_ROWS, LANES))
    gate_hi = row_gate.astype(BF16).astype(F32)
    sbuf[slot, :, half:] = pltpu.pack_elementwise([gate_hi, row_gate - gate_hi], packed_dtype=BF16)

    def copies(s):
        return _segment_copies(tab_ref, s, sbuf.at[s % 2], xb_ref, True, sem.at[s % 2])

    n_steps = pl.num_programs(0)
    fill_base = n_steps * N_EXPERTS * 3
    fill_sem = sem.at[2]

    def pad_fills():
        out = []
        for e in range(N_EXPERTS):
            start = pl.multiple_of(tab_ref[fill_base + 2 * e], SUBLANES)
            n = pl.multiple_of(tab_ref[fill_base + 2 * e + 1], SUBLANES)
            out.append((n, pltpu.make_async_copy(zbuf.at[pl.ds(0, n), :], xb_ref.at[pl.ds(start, n), :], fill_sem)))
        return out

    def block_fill(j):
        row = pl.multiple_of(tab_ref[fill_base + 2 * N_EXPERTS] + j * EXPERT_BLOCK, EXPERT_BLOCK)
        return pltpu.make_async_copy(zbuf, xb_ref.at[pl.ds(row, EXPERT_BLOCK), :], fill_sem)

    n_unused = tab_ref[fill_base + 2 * N_EXPERTS + 1]

    @pl.when(step == 0)
    def _():
        zbuf[...] = jnp.zeros_like(zbuf)
        _start_copies(pad_fills())
        lax.fori_loop(0, n_unused, lambda j, c: (block_fill(j).start(), c)[1], 0)

    @pl.when(step > 0)
    def _():
        _wait_copies(copies(step - 1))

    _start_copies(copies(step))

    @pl.when(step == n_steps - 1)
    def _():
        _wait_copies(copies(step))
        _wait_copies(pad_fills())
        lax.fori_loop(0, n_unused, lambda j, c: (block_fill(j).wait(), c)[1], 0)


def _dispatch_call(seg_tab, x1, route, n_rows):
    n = x1.shape[0]
    tb = TOKEN_BLOCK
    grid_spec = pltpu.PrefetchScalarGridSpec(
        num_scalar_prefetch=1,
        grid=(n // tb,),
        in_specs=[pl.BlockSpec((tb, D_MODEL), lambda i, t: (i, 0)),
                  pl.BlockSpec((1, ROUTE_ROWS, tb), lambda i, t: (i, 0, 0))],
        out_specs=pl.BlockSpec(memory_space=pl.ANY),
        scratch_shapes=[pltpu.VMEM((2, SORT_ROWS, GROUPED_WIDTH), jnp.uint32),
                        pltpu.VMEM((EXPERT_BLOCK, GROUPED_WIDTH), jnp.uint32),
                        pltpu.SemaphoreType.DMA((3,))],
    )
    return pl.pallas_call(
        _dispatch_kernel,
        grid_spec=grid_spec,
        out_shape=jax.ShapeDtypeStruct((n_rows, GROUPED_WIDTH), jnp.uint32),
        compiler_params=_cparams(("arbitrary",)),
        name="dispatch",
    )(seg_tab, x1, route)


def _expert_kernel(be_ref, nv_ref, nx_ref, x_ref, bgu_ref, bdn_ref, wgu_hbm, wdn_hbm, y_ref,
                   wgu_st, wdn_st, wgu_bf, wdn_bf, sem):
    i = pl.program_id(0)
    e = be_ref[i]
    changed = jnp.logical_or(i == 0, be_ref[jnp.maximum(i - 1, 0)] != e)

    def fetch(ex):
        return (pltpu.make_async_copy(wgu_hbm.at[ex], wgu_st, sem.at[0]),
                pltpu.make_async_copy(wdn_hbm.at[ex], wdn_st, sem.at[1]))

    @pl.when(i == 0)
    def _():
        for cp in fetch(e):
            cp.start()

    @pl.when(changed)
    def _():
        for cp in fetch(e):
            cp.wait()
        wgu_bf[...] = wgu_st[...].astype(BF16)
        wdn_bf[...] = wdn_st[...].astype(BF16)

        @pl.when(nx_ref[i] >= 0)
        def _():
            for cp in fetch(nx_ref[i]):
                cp.start()

    @pl.when(nv_ref[i] > 0)
    def _():
        live = lax.broadcasted_iota(jnp.int32, (x_ref.shape[0], 1), 0) < nv_ref[i]
        half = D_MODEL // 2
        halves = lambda words: [jnp.where(live, pltpu.unpack_elementwise(
            words, index=idx, packed_dtype=BF16, unpacked_dtype=F32), 0.0) for idx in (0, 1)]
        x_lo, x_hi = halves(x_ref[:, :half])
        g_hi, g_rest = halves(x_ref[:, half:half + LANES])
        row_gate = (g_hi + g_rest)[:, 0:1]
        gu = (_dot(x_lo.astype(BF16), wgu_bf[:half, :]) + _dot(x_hi.astype(BF16), wgu_bf[half:, :])
              + bgu_ref[0])
        gate = jnp.minimum(gu[:, :D_FF], SWIGLU_LIMIT)
        up = jnp.clip(gu[:, D_FF:], -SWIGLU_LIMIT, SWIGLU_LIMIT)
        h = (up + 1.0) * (gate * jax.nn.sigmoid(SWIGLU_ALPHA * gate))
        y_ref[...] = (_dot(h.astype(BF16), wdn_bf[...]) + bdn_ref[0]) * row_gate

    @pl.when(nv_ref[i] == 0)
    def _():
        y_ref[...] = jnp.zeros_like(y_ref)


def _expert_call(blk_e, n_valid, next_e, xb, wgu, bgu, wdn, bdn):
    n_rows = xb.shape[0]
    eb = EXPERT_BLOCK
    grid_spec = pltpu.PrefetchScalarGridSpec(
        num_scalar_prefetch=3,
        grid=(n_rows // eb,),
        in_specs=[pl.BlockSpec((eb, GROUPED_WIDTH), lambda i, be, nv, nx: (i, 0)),
                  pl.BlockSpec((1, 1, 2 * D_FF), lambda i, be, nv, nx: (be[i], 0, 0)),
                  pl.BlockSpec((1, 1, D_MODEL), lambda i, be, nv, nx: (be[i], 0, 0)),
                  pl.BlockSpec(memory_space=pl.ANY),
                  pl.BlockSpec(memory_space=pl.ANY)],
        out_specs=pl.BlockSpec((eb, D_MODEL), lambda i, be, nv, nx: (i, 0)),
        scratch_shapes=[pltpu.VMEM((D_MODEL, 2 * D_FF), F32), pltpu.VMEM((D_FF, D_MODEL), F32),
                        pltpu.VMEM((D_MODEL, 2 * D_FF), BF16), pltpu.VMEM((D_FF, D_MODEL), BF16),
                        pltpu.SemaphoreType.DMA((2,))],
    )
    return pl.pallas_call(
        _expert_kernel,
        grid_spec=grid_spec,
        out_shape=jax.ShapeDtypeStruct((n_rows, D_MODEL), F32),
        compiler_params=_cparams(("arbitrary",)),
        name="experts",
    )(blk_e, n_valid, next_e, xb, bgu, bdn, wgu, wdn)


def _combine_kernel(tab_ref, info_ref, x1_ref, g_ref, b_ref, yb_ref, op_ref, os_ref, buf, sem, *,
                    alpha, n_prompt_blocks):
    tb = x1_ref.shape[0]
    step = pl.program_id(0)
    slot = step % 2
    tail = tb * TOP_K

    def fetch(s):
        buf[s % 2, tail:, :] = jnp.zeros((SORT_ROWS - tail, D_MODEL), F32)
        _start_copies(_segment_copies(tab_ref, s, yb_ref, buf.at[s % 2], False, sem.at[s % 2]))

    @pl.when(step == 0)
    def _():
        fetch(step)

    @pl.when(step + 1 < pl.num_programs(0))
    def _():
        fetch(step + 1)

    _wait_copies(_segment_copies(tab_ref, step, yb_ref, buf.at[slot], False, sem.at[slot]))
    info = info_ref[...]
    cols = lax.broadcasted_iota(jnp.int32, (tb, SORT_ROWS), 1).astype(F32)
    hit = None
    for k in range(TOP_K):
        eq = cols == info[:, 2 * TOP_K + k:2 * TOP_K + k + 1]
        hit = eq if hit is None else jnp.logical_or(hit, eq)
    unsort = jnp.where(hit, 1.0, 0.0).astype(BF16)
    hi, lo = _split_bf16(buf[slot])
    moe = _dot(unsort, hi) + _dot(unsort, lo)
    out = _ln(alpha * x1_ref[...] + moe, g_ref[...], b_ref[...])

    @pl.when(step < n_prompt_blocks)
    def _():
        op_ref[...] = out

    @pl.when(step >= n_prompt_blocks)
    def _():
        os_ref[...] = out


def _combine_call(seg_tab, info, x1, ln_g, ln_b, yb, alpha, n_prompt):
    n = x1.shape[0]
    tb = TOKEN_BLOCK
    npb = n_prompt // tb
    prompt_row, sample_row = _split_rows(tb, npb)
    grid_spec = pltpu.PrefetchScalarGridSpec(
        num_scalar_prefetch=1,
        grid=(n // tb,),
        in_specs=[pl.BlockSpec((tb, LANES), lambda i, t: (i, 0)),
                  pl.BlockSpec((tb, D_MODEL), lambda i, t: (i, 0)),
                  pl.BlockSpec((1, D_MODEL), lambda i, t: (0, 0)),
                  pl.BlockSpec((1, D_MODEL), lambda i, t: (0, 0)),
                  pl.BlockSpec(memory_space=pl.ANY)],
        out_specs=[prompt_row(D_MODEL), sample_row(D_MODEL)],
        scratch_shapes=[pltpu.VMEM((2, SORT_ROWS, D_MODEL), F32), pltpu.SemaphoreType.DMA((2,))],
    )
    return pl.pallas_call(
        functools.partial(_combine_kernel, alpha=alpha, n_prompt_blocks=npb),
        grid_spec=grid_spec,
        out_shape=[jax.ShapeDtypeStruct((n_prompt, D_MODEL), F32),
                   jax.ShapeDtypeStruct((n - n_prompt, D_MODEL), F32)],
        compiler_params=_cparams(("arbitrary",)),
        name="combine",
    )(seg_tab, info, x1, ln_g, ln_b, yb)


def _rope_tables(pos):
    half = MLA_ROPE // 2
    inv_freq = ROPE_THETA ** (-jnp.arange(half, dtype=F32) / half)
    ang = pos.astype(F32)[:, None] * inv_freq[None, :]
    cos, sin = jnp.cos(ang), jnp.sin(ang)
    zero = jnp.zeros((pos.shape[0], LANES - MLA_ROPE), F32)
    return (jnp.concatenate([cos, cos, zero], axis=1), jnp.concatenate([-sin, sin, zero], axis=1))


def _s5_params(lam_re, lam_im, log_dt, b_re, b_im, c_re, c_im):
    dt = jnp.exp(log_dt)[:, None]
    mag = jnp.exp(lam_re * dt)
    ab_re = mag * jnp.cos(lam_im * dt)
    ab_im = mag * jnp.sin(lam_im * dt)
    den = lam_re * lam_re + lam_im * lam_im
    nr, ni = ab_re - 1.0, ab_im
    f_re = (nr * lam_re + ni * lam_im) / den
    f_im = (ni * lam_re - nr * lam_im) / den
    bb_re = f_re[..., None] * b_re - f_im[..., None] * b_im
    bb_im = f_re[..., None] * b_im + f_im[..., None] * b_re
    eye = jnp.eye(S5_GROUPS, dtype=F32)

    def in_blockdiag(w):
        return jnp.einsum('gnc,gh->gchn', w, eye).reshape(S5_WIDTH, S5_COLS)

    def out_blockdiag(w):
        return jnp.einsum('gcn,gh->gnhc', w, eye).reshape(S5_COLS, S5_WIDTH)

    bb = jnp.concatenate([in_blockdiag(bb_re), in_blockdiag(bb_im)], axis=1).astype(BF16)
    cc = jnp.concatenate([out_blockdiag(c_re), -out_blockdiag(c_im)], axis=0).astype(BF16)
    ar, ai = ab_re.reshape(1, S5_COLS), ab_im.reshape(1, S5_COLS)
    pr, pi = [ar], [ai]
    for _ in range(SUBLANES - 1):
        pr, pi = pr + [pr[-1] * ar - pi[-1] * ai], pi + [pr[-1] * ai + pi[-1] * ar]
    rows = jnp.arange(SUBLANES)[:, None]
    tabs = []
    for d in (1, 2, 4):
        keep = rows >= d
        tabs.append(jnp.stack([jnp.where(keep, pr[d - 1], 0.0), jnp.where(keep, pi[d - 1], 0.0)]))
    tabs.append(jnp.stack([jnp.concatenate(pr, axis=0), jnp.concatenate(pi, axis=0)]))
    return bb, cc, jnp.stack(tabs)


def _gmlp_spatial(w_s, b_s, chunk_len):
    reps = GMLP_CHUNK // chunk_len
    i = jnp.arange(chunk_len)
    mask = (i[None, :] // CHUNK) <= (i[:, None] // CHUNK)
    w = jnp.where(mask[None], w_s[:, :chunk_len, :chunk_len], 0.0)
    eye = jnp.eye(reps, dtype=F32)
    wblk = jnp.einsum('hij,rs->hrisj', w, eye).reshape(GMLP_HEADS, GMLP_CHUNK, GMLP_CHUNK)
    wcat = jnp.transpose(wblk, (1, 0, 2)).reshape(GMLP_CHUNK, GMLP_HEADS * GMLP_CHUNK)
    bias = jnp.tile(b_s[:, :chunk_len], (1, reps))
    bias = jnp.repeat(bias.T, GMLP_HEAD_DIM, axis=1)
    return wcat.astype(BF16), bias


def _layer_weights(p, l, cos, sin, dec_seq):
    lw = {'cos': cos, 'sin': sin}
    w_in = p['w_in'][l]
    kr = w_in[:, _C_KR:_C_KR + MLA_ROPE]
    half = MLA_ROPE // 2
    zero = jnp.zeros((D_MODEL, LANES - MLA_ROPE), F32)
    rest = w_in[:, _C_KR + MLA_ROPE:]
    lw['w_in'] = jnp.concatenate(
        [w_in[:, :_C_KR], kr, zero, kr[:, half:], kr[:, :half], zero, rest], axis=1).astype(BF16)
    wq = p['w_q_b'][l].reshape(Q_LORA, MLA_HEADS, MLA_NOPE + MLA_ROPE)
    zq = jnp.zeros((Q_LORA, MLA_HEADS, LANES - MLA_ROPE), F32)
    rope = wq[:, :, MLA_NOPE:]
    plain = jnp.concatenate([wq, zq], axis=2).reshape(Q_LORA, QK_WIDTH)
    swapped = jnp.concatenate([rope[:, :, half:], rope[:, :, :half], zq], axis=2)
    lw['wq'] = jnp.concatenate([plain, swapped.reshape(Q_LORA, MLA_HEADS * LANES)], axis=1).astype(BF16)
    lw['q_g'] = p['q_a_norm_g'][l].reshape(1, Q_LORA)
    lw['kv_g'] = p['kv_a_norm_g'][l].reshape(1, KV_LORA)
    lw['wkv'] = p['w_kv_b'][l].astype(BF16)
    lw['s5_bb'], lw['s5_cc'], lw['s5_tab'] = _s5_params(
        p['s5_lam_re'][l], p['s5_lam_im'][l], p['s5_log_dt'][l], p['s5_b_re'][l], p['s5_b_im'][l],
        p['s5_c_re'][l], p['s5_c_im'][l])
    lw['s5_d'] = p['s5_d'][l].reshape(1, S5_WIDTH)
    lw['s5_wg'] = p['s5_w_glu'][l].astype(BF16)
    lw['s5_bg'] = p['s5_b_glu'][l].reshape(1, S5_WIDTH)
    lw['g_g'] = p['gmlp_norm_g'][l].reshape(1, GMLP_WIDTH)
    lw['g_b'] = p['gmlp_norm_b'][l].reshape(1, GMLP_WIDTH)
    grp = jnp.arange(GMLP_WIDTH) // GMLP_HEAD_DIM
    lw['mavg'] = jnp.where(grp[:, None] == grp[None, :], 1.0 / GMLP_HEAD_DIM, 0.0).astype(BF16)
    wp, bp = _gmlp_spatial(p['gmlp_w_s'][l], p['gmlp_b_s'][l], GMLP_CHUNK)
    ws, bs = _gmlp_spatial(p['gmlp_w_s'][l], p['gmlp_b_s'][l], dec_seq)
    lw['wsp'] = jnp.stack([wp, ws])
    lw['bsp'] = jnp.stack([bp, bs])
    lw['mix_g'] = p['mix_norm_g'][l].reshape(1, -1)
    lw['w_out'] = p['w_out'][l].astype(BF16)
    lw['ln1_g'] = p['ln1_g'][l].reshape(1, D_MODEL)
    lw['ln1_b'] = p['ln1_b'][l].reshape(1, D_MODEL)
    rw = p['router_w'][l].T
    lw['rw_hi'] = rw.astype(BF16)
    lw['rw_lo'] = (rw - lw['rw_hi'].astype(F32)).astype(BF16)
    lw['router_b'] = p['router_b'][l].reshape(N_EXPERTS, 1)
    t = jnp.arange(TOKEN_BLOCK)
    lw['tri'] = (t[:, None] < t[None, :]).astype(BF16)
    e = jnp.arange(N_EXPERTS)
    lw['lower'] = (e[None, :] < e[:, None]).astype(BF16)
    lw['ln2_g'] = p['ln2_g'][l].reshape(1, D_MODEL)
    lw['ln2_b'] = p['ln2_b'][l].reshape(1, D_MODEL)
    return lw


def _moe(x1, info, route, counts, p, l, lw, alpha, n_prompt):
    n = x1.shape[0]
    eb = EXPERT_BLOCK
    tb = TOKEN_BLOCK
    nb = n // tb
    cnt = counts.reshape(nb, N_EXPERTS).astype(jnp.int32)
    seg = (cnt + SUBLANES - 1) // SUBLANES * SUBLANES
    local_start = jnp.cumsum(seg, axis=1) - seg
    run = jnp.cumsum(seg, axis=0) - seg
    total = jnp.sum(seg, axis=0)
    padded = (total + eb - 1) // eb * eb
    pad_end = jnp.cumsum(padded)
    pad_start = pad_end - padded
    n_blk = -(-(n * TOP_K + nb * N_EXPERTS * (SUBLANES - 1)) // eb) + N_EXPERTS
    n_used = (pad_end[-1] // eb).astype(jnp.int32)
    seg_tab = jnp.concatenate([
        jnp.stack([local_start, pad_start[None, :] + run, seg], axis=-1).reshape(-1),
        jnp.stack([pad_start + total, padded - total], axis=-1).reshape(-1),
        jnp.stack([pad_end[-1], n_blk - n_used])]).astype(jnp.int32)
    experts = jnp.arange(N_EXPERTS, dtype=jnp.int32)
    blk_start = jnp.arange(n_blk, dtype=jnp.int32)[:, None] * eb
    member = (blk_start >= pad_start[None, :]) & (blk_start < pad_end[None, :])
    used = jnp.any(member, axis=1)
    pick = lambda per_expert: jnp.sum(jnp.where(member, per_expert[None, :], 0), axis=1)
    last_e = jnp.max(jnp.where(padded > 0, experts, 0))
    blk_e = jnp.where(used, pick(experts), last_e).astype(jnp.int32)
    n_valid = jnp.sum(jnp.where(member, jnp.clip((pad_start + total)[None, :] - blk_start, 0, eb), 0),
                      axis=1).astype(jnp.int32)
    has_rows = jnp.where(padded > 0, experts, 2 * N_EXPERTS)
    later = jnp.concatenate([lax.cummin(has_rows, reverse=True)[1:],
                             jnp.full((1,), 2 * N_EXPERTS, jnp.int32)])
    next_e = jnp.where(used, pick(jnp.where(later < N_EXPERTS, later + l * N_EXPERTS, -1)), -1).astype(jnp.int32)
    xb = _dispatch_call(seg_tab, x1, route, n_blk * eb)
    depth = p['moe_w_gu'].shape[0]
    yb = _expert_call(blk_e + l * N_EXPERTS, n_valid, next_e, xb,
                      p['moe_w_gu'].reshape(depth * N_EXPERTS, D_MODEL, 2 * D_FF),
                      p['moe_b_gu'].reshape(depth * N_EXPERTS, 1, 2 * D_FF),
                      p['moe_w_down'].reshape(depth * N_EXPERTS, D_FF, D_MODEL),
                      p['moe_b_down'].reshape(depth * N_EXPERTS, 1, D_MODEL))
    return _combine_call(seg_tab, info, x1, lw['ln2_g'], lw['ln2_b'], yb, alpha, n_prompt)


def kernel(x_prompt, x_sample, cache_mla_latent, cache_mla_krope, state_s5_re, state_s5_im, w_in, q_a_norm_g, w_q_b, kv_a_norm_g, w_kv_b, s5_lam_re, s5_lam_im, s5_log_dt, s5_b_re, s5_b_im, s5_c_re, s5_c_im, s5_d, s5_w_glu, s5_b_glu, gmlp_norm_g, gmlp_norm_b, gmlp_w_s, gmlp_b_s, mix_norm_g, w_out, ln1_g, ln1_b, router_w, router_b, moe_w_gu, moe_b_gu, moe_w_down, moe_b_down, ln2_g, ln2_b):
    p = dict(w_in=w_in, q_a_norm_g=q_a_norm_g, w_q_b=w_q_b, kv_a_norm_g=kv_a_norm_g, w_kv_b=w_kv_b,
             s5_lam_re=s5_lam_re, s5_lam_im=s5_lam_im, s5_log_dt=s5_log_dt, s5_b_re=s5_b_re,
             s5_b_im=s5_b_im, s5_c_re=s5_c_re, s5_c_im=s5_c_im, s5_d=s5_d, s5_w_glu=s5_w_glu,
             s5_b_glu=s5_b_glu, gmlp_norm_g=gmlp_norm_g, gmlp_norm_b=gmlp_norm_b, gmlp_w_s=gmlp_w_s,
             gmlp_b_s=gmlp_b_s, mix_norm_g=mix_norm_g, w_out=w_out, ln1_g=ln1_g, ln1_b=ln1_b,
             router_w=router_w, router_b=router_b, moe_w_gu=moe_w_gu, moe_b_gu=moe_b_gu,
             moe_w_down=moe_w_down, moe_b_down=moe_b_down, ln2_g=ln2_g, ln2_b=ln2_b)
    depth = w_in.shape[0]
    bp, sp, _ = x_prompt.shape
    bs, ss, _ = x_sample.shape
    past = cache_mla_latent.shape[2]
    n_p, n_s = bp * sp, bs * ss
    assert sp % ATTN_BLOCK == 0 and sp % S5_BLOCK == 0 and sp % TOKEN_BLOCK == 0
    assert n_s % TOKEN_BLOCK == 0 and GMLP_CHUNK % ss == 0 and ss % SUBLANES == 0
    alpha = float((2 * depth) ** 0.25)

    pos = jnp.concatenate([jnp.arange(sp, dtype=jnp.int32),
                           jnp.tile(past + jnp.arange(ss, dtype=jnp.int32), bs)])
    cos, sin = _rope_tables(pos)
    xp, xs = x_prompt.reshape(n_p, D_MODEL), x_sample.reshape(n_s, D_MODEL)
    zero_state = jnp.zeros((bp, 1, S5_COLS), F32)

    outs = {k: [] for k in ('lat_p', 'kr_p', 'sre_p', 'sim_p', 'lat_s', 'kr_s', 'sre_s', 'sim_s', 'gv_s')}
    for l in range(depth):
        lw = _layer_weights(p, l, cos, sin, ss)
        q, k, v, lat, kr, u, gm, gv = _pre_call(xp, xs, lw)
        attn_p = _attn_call(q, k, v, bp, sp)
        attn_s = _attn_sample_call(q, k, v, cache_mla_latent, cache_mla_krope, l, lw['wkv'],
                                   n_p, bs, ss)
        ssm_p, sre_p, sim_p = _s5_call(u, zero_state, zero_state, lw, 0, bp, sp, S5_BLOCK, "s5_prompt")
        ssm_s, sre_s, sim_s = _s5_call(u, state_s5_re[l].reshape(bs, 1, S5_COLS).astype(F32),
                                       state_s5_im[l].reshape(bs, 1, S5_COLS).astype(F32),
                                       lw, n_p, bs, ss, ss, "s5_sample")
        x1, info, route, counts = _post_call(attn_p, attn_s, ssm_p, ssm_s, gm, xp, xs, lw, alpha)
        xp, xs = _moe(x1, info, route, counts, p, l, lw, alpha, n_p)

        outs['lat_p'].append(lat[:n_p].reshape(bp, sp, KV_LORA))
        outs['kr_p'].append(kr[:n_p].reshape(bp, sp, MLA_ROPE))
        outs['sre_p'].append(sre_p.reshape(bp, S5_GROUPS, S5_STATE))
        outs['sim_p'].append(sim_p.reshape(bp, S5_GROUPS, S5_STATE))
        outs['lat_s'].append(lat[n_p:].reshape(bs, ss, KV_LORA))
        outs['kr_s'].append(kr[n_p:].reshape(bs, ss, MLA_ROPE))
        outs['sre_s'].append(sre_s.reshape(bs, S5_GROUPS, S5_STATE))
        outs['sim_s'].append(sim_s.reshape(bs, S5_GROUPS, S5_STATE))
        outs['gv_s'].append(gv[n_p:].reshape(bs, ss, GMLP_WIDTH))

    st = lambda name: jnp.stack(outs[name])
    return (xp.reshape(bp, sp, D_MODEL), xs.reshape(bs, ss, D_MODEL),
            st('lat_p'), st('kr_p'), st('sre_p'), st('sim_p'),
            st('lat_s'), st('kr_s'), st('sre_s'), st('sim_s'), st('gv_s'))
```

```python
import functools
import math

import jax
import jax.numpy as jnp
import numpy as np
from jax import lax
from jax.experimental import pallas as pl
from jax.experimental.pallas import tpu as pltpu

F32 = jnp.float32
BF16 = jnp.bfloat16

D_MODEL = 1024
CHUNK = 64
MLA_HEADS = 4
MLA_NOPE = 128
MLA_ROPE = 64
MLA_V = 128
Q_LORA = 256
KV_LORA = 128
MLA_WIDTH = MLA_HEADS * MLA_V
MLA_SCALE = (MLA_NOPE + MLA_ROPE) ** -0.5
Q_SCALE = MLA_SCALE * math.log2(math.e)
ROPE_THETA = 10000.0
S5_GROUP = 16
S5_GROUPS = 16
S5_WIDTH = S5_GROUP * S5_GROUPS
S5_STATE = 64
S5_COLS = S5_GROUPS * S5_STATE
GMLP_HEADS = 4
GMLP_HEAD_DIM = 64
GMLP_WIDTH = GMLP_HEADS * GMLP_HEAD_DIM
GMLP_CHUNK = 128
N_EXPERTS = 32
TOP_K = 4
D_FF = D_MODEL
SWIGLU_LIMIT = 7.0
SWIGLU_ALPHA = 1.702
NORM_EPS = 1e-5

LANES = 128
SUBLANES = 8
TOKEN_BLOCK = 256
ATTN_BLOCK = 1024
ATTN_HEADS = 2
EXPERT_BLOCK = 512
S5_BLOCK = 1024
SORT_ROWS = TOKEN_BLOCK * TOP_K + N_EXPERTS * SUBLANES
ROUTE_ROWS = 2 * SUBLANES
POST_ROWS = 128
GROUPED_WIDTH = D_MODEL // 2 + LANES
VMEM_LIMIT = 56 * 1024 * 1024

_C_Q = 0
_C_KV = _C_Q + Q_LORA
_C_KR = _C_KV + KV_LORA
_C_KRS = _C_KR + LANES
_C_S5 = _C_KRS + LANES
_C_G = _C_S5 + S5_WIDTH
IN_EXT = _C_G + 2 * GMLP_WIDTH
HEAD_SLAB = MLA_NOPE + LANES
QK_WIDTH = MLA_HEADS * HEAD_SLAB


def _cparams(sem, vmem=VMEM_LIMIT):
    return pltpu.CompilerParams(dimension_semantics=sem, vmem_limit_bytes=vmem)


def _dot(a, b):
    return jnp.dot(a, b, preferred_element_type=F32)


def _dot_nt(a, b):
    return lax.dot_general(a, b, (((1,), (1,)), ((), ())), preferred_element_type=F32)


def _split_bf16(x):
    hi = x.astype(BF16)
    lo = (x - hi.astype(F32)).astype(BF16)
    return hi, lo


def _rms(x, g):
    return x * lax.rsqrt(jnp.mean(x * x, axis=-1, keepdims=True) + NORM_EPS) * g


def _ln(x, g, b):
    xc = x - jnp.mean(x, axis=-1, keepdims=True)
    var = jnp.mean(xc * xc, axis=-1, keepdims=True)
    return xc * lax.rsqrt(var + NORM_EPS) * g + b


def _pre_kernel(xp_ref, xs_ref, win_ref, qg_ref, wq_ref, kvg_ref, wkv_ref, cos_ref, sin_ref,
                gg_ref, gb_ref, mavg_ref, wsp_ref, bsp_ref,
                q_ref, k_ref, v_ref, lat_ref, kr_ref, u_ref, gm_ref, gv_ref, *, n_prompt_blocks):
    is_prompt = pl.program_id(0) < n_prompt_blocks
    xb = jnp.where(is_prompt, xp_ref[...], xs_ref[...]).astype(BF16)
    proj = _dot(xb, win_ref[...])
    cos = cos_ref[...]
    sin = sin_ref[...]

    qa = _rms(proj[:, _C_Q:_C_Q + Q_LORA], qg_ref[...]).astype(BF16)
    qq = _dot(qa, wq_ref[...])
    for h in range(MLA_HEADS):
        c0 = h * HEAD_SLAB
        nope = qq[:, c0:c0 + MLA_NOPE]
        rope = (qq[:, c0 + MLA_NOPE:c0 + HEAD_SLAB] * cos
                + qq[:, QK_WIDTH + h * LANES:QK_WIDTH + (h + 1) * LANES] * sin)
        q_ref[:, c0:c0 + MLA_NOPE] = (nope * Q_SCALE).astype(BF16)
        q_ref[:, c0 + MLA_NOPE:c0 + HEAD_SLAB] = (rope * Q_SCALE).astype(BF16)

    lat = _rms(proj[:, _C_KV:_C_KV + KV_LORA], kvg_ref[...])
    lat_ref[...] = lat
    kv = _dot(lat.astype(BF16), wkv_ref[...])
    kr = proj[:, _C_KR:_C_KR + LANES] * cos + proj[:, _C_KRS:_C_KRS + LANES] * sin
    kr_ref[...] = kr[:, :MLA_ROPE]
    krb = kr.astype(BF16)
    for h in range(MLA_HEADS):
        c0 = h * HEAD_SLAB
        k_ref[:, c0:c0 + MLA_NOPE] = kv[:, h * 256:h * 256 + MLA_NOPE].astype(BF16)
        k_ref[:, c0 + MLA_NOPE:c0 + HEAD_SLAB] = krb
        v_ref[:, h * MLA_V:(h + 1) * MLA_V] = kv[:, h * 256 + MLA_NOPE:(h + 1) * 256].astype(BF16)

    u_ref[...] = proj[:, _C_S5:_C_S5 + S5_WIDTH]

    z = jax.nn.gelu(proj[:, _C_G:_C_G + 2 * GMLP_WIDTH])
    ug = z[:, :GMLP_WIDTH]
    vg = z[:, GMLP_WIDTH:]
    mavg = mavg_ref[...]
    hi, lo = _split_bf16(vg)
    xc = vg - (_dot(hi, mavg) + _dot(lo, mavg))
    hi, lo = _split_bf16(xc * xc)
    var = _dot(hi, mavg) + _dot(lo, mavg)
    vn = xc * lax.rsqrt(var + NORM_EPS) * gg_ref[...] + gb_ref[...]
    gv_ref[...] = vn
    lane = lax.broadcasted_iota(jnp.int32, (GMLP_CHUNK, GMLP_WIDTH), 1)
    wsp = wsp_ref[0]
    bsp = bsp_ref[0]
    for c in range(xp_ref.shape[0] // GMLP_CHUNK):
        r0 = c * GMLP_CHUNK
        vc = vn[r0:r0 + GMLP_CHUNK, :].astype(BF16)
        stack = jnp.concatenate(
            [jnp.where(lane // GMLP_HEAD_DIM == h, vc, jnp.zeros_like(vc))
             for h in range(GMLP_HEADS)], axis=0)
        mix = _dot(wsp, stack) + bsp
        gm_ref[r0:r0 + GMLP_CHUNK, :] = ug[r0:r0 + GMLP_CHUNK, :] * mix


def _split_rows(tb, npb):
    prompt = lambda w: pl.BlockSpec((tb, w), lambda i, *_: (jnp.minimum(i, npb - 1), 0))
    sample = lambda w: pl.BlockSpec((tb, w), lambda i, *_: (jnp.maximum(i - npb, 0), 0))
    return prompt, sample


def _pre_call(xp, xs, lw):
    tb = TOKEN_BLOCK
    n = xp.shape[0] + xs.shape[0]
    nb = n // tb
    n_prompt_blocks = xp.shape[0] // tb
    row = lambda w: pl.BlockSpec((tb, w), lambda i: (i, 0))
    prompt_row, sample_row = _split_rows(tb, n_prompt_blocks)
    full = lambda a: pl.BlockSpec(a.shape, lambda i: (0,) * a.ndim)
    variant = lambda i: (jnp.where(i < n_prompt_blocks, 0, 1), 0, 0)
    seq_blocks = lw['cos'].shape[0] // tb - (nb - n_prompt_blocks)
    rope_row = pl.BlockSpec((tb, LANES), lambda i: (
        jnp.where(i < n_prompt_blocks, i % seq_blocks, i - n_prompt_blocks + seq_blocks), 0))
    in_specs = [prompt_row(D_MODEL), sample_row(D_MODEL),
                full(lw['w_in']), full(lw['q_g']), full(lw['wq']), full(lw['kv_g']),
                full(lw['wkv']), rope_row, rope_row, full(lw['g_g']), full(lw['g_b']),
                full(lw['mavg']),
                pl.BlockSpec((1, GMLP_CHUNK, GMLP_HEADS * GMLP_CHUNK), variant),
                pl.BlockSpec((1, GMLP_CHUNK, GMLP_WIDTH), variant)]
    widths = [(QK_WIDTH, BF16), (QK_WIDTH, BF16), (MLA_WIDTH, BF16), (KV_LORA, F32),
              (MLA_ROPE, F32), (S5_WIDTH, F32), (GMLP_WIDTH, F32), (GMLP_WIDTH, F32)]
    return pl.pallas_call(
        functools.partial(_pre_kernel, n_prompt_blocks=n_prompt_blocks),
        grid=(nb,),
        in_specs=in_specs,
        out_specs=[row(w) for w, _ in widths],
        out_shape=[jax.ShapeDtypeStruct((n, w), dt) for w, dt in widths],
        compiler_params=_cparams(("parallel",)),
        name="pre",
    )(xp, xs, lw['w_in'], lw['q_g'], lw['wq'], lw['kv_g'], lw['wkv'], lw['cos'], lw['sin'],
      lw['g_g'], lw['g_b'], lw['mavg'], lw['wsp'], lw['bsp'])


def _attn_kernel(q_ref, k_ref, v_ref, o_ref):
    i = pl.program_id(2)
    bq = q_ref.shape[0]
    qs = [q_ref[:, h * HEAD_SLAB:(h + 1) * HEAD_SLAB] for h in range(ATTN_HEADS)]

    def step(h, r0, carry, mask):
        m, l, acc = carry
        s = _dot_nt(qs[h], k_ref[pl.ds(r0, bq), h * HEAD_SLAB:(h + 1) * HEAD_SLAB])
        if mask is not None:
            s = jnp.where(mask, s, -jnp.inf)
        m_new = jnp.maximum(m, jnp.max(s, axis=-1, keepdims=True))
        p = jnp.exp2(s - m_new)
        alpha = jnp.exp2(m - m_new)
        l = alpha * l + jnp.sum(p, axis=-1, keepdims=True)
        acc = alpha * acc + _dot(p.astype(BF16), v_ref[pl.ds(r0, bq), h * MLA_V:(h + 1) * MLA_V])
        return m_new, l, acc

    def body(j, carries):
        r0 = pl.multiple_of(j * bq, bq)
        return tuple(step(h, r0, carries[h], None) for h in range(ATTN_HEADS))

    init = (jnp.full((bq, 1), -jnp.inf, F32), jnp.zeros((bq, 1), F32), jnp.zeros((bq, MLA_V), F32))
    carries = lax.fori_loop(0, i, body, (init,) * ATTN_HEADS)
    r0 = pl.multiple_of(i * bq, bq)
    qc = lax.broadcasted_iota(jnp.int32, (bq, bq), 0) // CHUNK
    kc = lax.broadcasted_iota(jnp.int32, (bq, bq), 1) // CHUNK
    for h in range(ATTN_HEADS):
        m, l, acc = step(h, r0, carries[h], kc <= qc)
        o_ref[:, h * MLA_V:(h + 1) * MLA_V] = acc / l


def _attn_call(q, k, v, batch, seq):
    bq = ATTN_BLOCK
    nq = seq // bq
    nh = ATTN_HEADS
    return pl.pallas_call(
        _attn_kernel,
        grid=(batch, MLA_HEADS // nh, nq),
        in_specs=[pl.BlockSpec((bq, nh * HEAD_SLAB), lambda b, h, i: (b * nq + i, h)),
                  pl.BlockSpec((seq, nh * HEAD_SLAB), lambda b, h, i: (b, h)),
                  pl.BlockSpec((seq, nh * MLA_V), lambda b, h, i: (b, h))],
        out_specs=pl.BlockSpec((bq, nh * MLA_V), lambda b, h, i: (b * nq + i, h)),
        out_shape=jax.ShapeDtypeStruct((batch * seq, MLA_WIDTH), F32),
        compiler_params=_cparams(("parallel", "parallel", "arbitrary")),
        name="attn_prompt",
    )(q, k, v)


def _attn_sample_kernel(q_ref, k_ref, v_ref, plat_ref, pkr_ref, wkv_ref, o_ref, *, past):
    t = q_ref.shape[0]
    q = q_ref[...]
    kn = k_ref[...]
    vn = v_ref[...]
    kvp = _dot(plat_ref[0].astype(BF16), wkv_ref[...]).astype(BF16)
    krp = pkr_ref[0].astype(BF16)
    q_chunk = (past + lax.broadcasted_iota(jnp.int32, (t, 1), 0)) // CHUNK
    mask_p = lax.broadcasted_iota(jnp.int32, (t, past), 1) // CHUNK <= q_chunk
    mask_n = (past + lax.broadcasted_iota(jnp.int32, (t, t), 1)) // CHUNK <= q_chunk
    for h in range(MLA_HEADS):
        c0 = h * HEAD_SLAB
        qh = q[:, c0:c0 + HEAD_SLAB]
        s_p = (_dot_nt(qh[:, :MLA_NOPE], kvp[:, h * 256:h * 256 + MLA_NOPE])
               + _dot_nt(qh[:, MLA_NOPE:MLA_NOPE + MLA_ROPE], krp))
        s_n = _dot_nt(qh, kn[:, c0:c0 + HEAD_SLAB])
        s_p = jnp.where(mask_p, s_p, -jnp.inf)
        s_n = jnp.where(mask_n, s_n, -jnp.inf)
        m = jnp.maximum(jnp.max(s_p, axis=-1, keepdims=True), jnp.max(s_n, axis=-1, keepdims=True))
        p_p = jnp.exp2(s_p - m)
        p_n = jnp.exp2(s_n - m)
        l = jnp.sum(p_p, axis=-1, keepdims=True) + jnp.sum(p_n, axis=-1, keepdims=True)
        o = (_dot(p_p.astype(BF16), kvp[:, h * 256 + MLA_NOPE:(h + 1) * 256])
             + _dot(p_n.astype(BF16), vn[:, h * MLA_V:(h + 1) * MLA_V]))
        o_ref[:, h * MLA_V:(h + 1) * MLA_V] = o / l


def _attn_sample_call(q, k, v, cache_lat, cache_kr, layer, wkv, row0, batch, t):
    past = cache_lat.shape[2]
    past_lat = cache_lat.reshape(-1, past, KV_LORA)
    past_kr = cache_kr.reshape(-1, past, MLA_ROPE)
    lb0 = layer * batch
    blk0 = row0 // t
    return pl.pallas_call(
        functools.partial(_attn_sample_kernel, past=past),
        grid=(batch,),
        in_specs=[pl.BlockSpec((t, QK_WIDTH), lambda b: (blk0 + b, 0)),
                  pl.BlockSpec((t, QK_WIDTH), lambda b: (blk0 + b, 0)),
                  pl.BlockSpec((t, MLA_WIDTH), lambda b: (blk0 + b, 0)),
                  pl.BlockSpec((1, past, KV_LORA), lambda b: (lb0 + b, 0, 0)),
                  pl.BlockSpec((1, past, MLA_ROPE), lambda b: (lb0 + b, 0, 0)),
                  pl.BlockSpec(wkv.shape, lambda b: (0, 0))],
        out_specs=pl.BlockSpec((t, MLA_WIDTH), lambda b: (b, 0)),
        out_shape=jax.ShapeDtypeStruct((batch * t, MLA_WIDTH), F32),
        compiler_params=_cparams(("parallel",)),
        name="attn_sample",
    )(q, k, v, past_lat, past_kr, wkv)


def _s5_kernel(u_ref, s0r_ref, s0i_ref, bb_ref, tab_ref, cc_ref, d_ref, wg_ref, bg_ref,
               y_ref, sr_ref, si_ref, st_ref, cr_ref, ci_ref):
    tb = u_ref.shape[0]
    nc = S5_COLS

    @pl.when(pl.program_id(1) == 0)
    def _():
        cr_ref[...] = jnp.broadcast_to(s0r_ref[0], (SUBLANES, nc))
        ci_ref[...] = jnp.broadcast_to(s0i_ref[0], (SUBLANES, nc))

    u = u_ref[...]
    st_ref[...] = _dot(u.astype(BF16), bb_ref[...])

    def tile(r, carry):
        car, cai = carry
        r0 = pl.multiple_of(r * SUBLANES, SUBLANES)
        xr = st_ref[pl.ds(r0, SUBLANES), :nc]
        xi = st_ref[pl.ds(r0, SUBLANES), nc:]
        for si, d in enumerate((1, 2, 4)):
            pr = tab_ref[si, 0]
            pi = tab_ref[si, 1]
            sr = pltpu.roll(xr, d, 0)
            sim = pltpu.roll(xi, d, 0)
            xr, xi = xr + (pr * sr - pi * sim), xi + (pr * sim + pi * sr)
        pr = tab_ref[3, 0]
        pi = tab_ref[3, 1]
        xr, xi = xr + (pr * car - pi * cai), xi + (pr * cai + pi * car)
        st_ref[pl.ds(r0, SUBLANES), :nc] = xr
        st_ref[pl.ds(r0, SUBLANES), nc:] = xi
        return (jnp.broadcast_to(xr[SUBLANES - 1:SUBLANES, :], (SUBLANES, nc)),
                jnp.broadcast_to(xi[SUBLANES - 1:SUBLANES, :], (SUBLANES, nc)))

    car, cai = lax.fori_loop(0, tb // SUBLANES, tile, (cr_ref[...], ci_ref[...]))
    cr_ref[...] = car
    ci_ref[...] = cai
    sr_ref[0] = car[0:1, :]
    si_ref[0] = cai[0:1, :]

    y = _dot(st_ref[...].astype(BF16), cc_ref[...]) + d_ref[...] * u
    z = jax.nn.gelu(y)
    y_ref[...] = z * jax.nn.sigmoid(_dot(z.astype(BF16), wg_ref[...]) + bg_ref[...])


def _s5_call(u, s0r, s0i, lw, row0, batch, seq, tblk, name):
    nt = seq // tblk
    blk0 = row0 // tblk
    full = lambda a: pl.BlockSpec(a.shape, lambda b, t: (0,) * a.ndim)
    st_spec = pl.BlockSpec((1, 1, S5_COLS), lambda b, t: (b, 0, 0))
    return pl.pallas_call(
        _s5_kernel,
        grid=(batch, nt),
        in_specs=[pl.BlockSpec((tblk, S5_WIDTH), lambda b, t: (blk0 + b * nt + t, 0)),
                  st_spec, st_spec,
                  full(lw['s5_bb']), full(lw['s5_tab']), full(lw['s5_cc']), full(lw['s5_d']),
                  full(lw['s5_wg']), full(lw['s5_bg'])],
        out_specs=[pl.BlockSpec((tblk, S5_WIDTH), lambda b, t: (b * nt + t, 0)), st_spec, st_spec],
        out_shape=[jax.ShapeDtypeStruct((batch * seq, S5_WIDTH), F32),
                   jax.ShapeDtypeStruct((batch, 1, S5_COLS), F32),
                   jax.ShapeDtypeStruct((batch, 1, S5_COLS), F32)],
        scratch_shapes=[pltpu.VMEM((tblk, 2 * S5_COLS), F32),
                        pltpu.VMEM((SUBLANES, S5_COLS), F32),
                        pltpu.VMEM((SUBLANES, S5_COLS), F32)],
        compiler_params=_cparams(("parallel", "arbitrary")),
        name=name,
    )(u, s0r, s0i, lw['s5_bb'], lw['s5_tab'], lw['s5_cc'], lw['s5_d'], lw['s5_wg'], lw['s5_bg'])


def _post_kernel(ap_ref, as_ref, sp_ref, ss_ref, g_ref, xp_ref, xs_ref, mg_ref, wo_ref, l1g_ref,
                 l1b_ref, rwh_ref, rwl_ref, rb_ref, tri_ref, lower_ref,
                 x1_ref, info_ref, rt_ref, cnt_ref, *, alpha, n_prompt_blocks):
    tb = xp_ref.shape[0]
    mg = mg_ref[...]
    is_prompt = pl.program_id(0) < n_prompt_blocks
    rwh = rwh_ref[...]
    rwl = rwl_ref[...]
    eidx = lax.broadcasted_iota(jnp.int32, (N_EXPERTS, POST_ROWS), 0).astype(F32)

    gates, hots = [], []
    for r0 in range(0, tb, POST_ROWS):
        sl = slice(r0, r0 + POST_ROWS)
        na = _rms(jnp.where(is_prompt, ap_ref[sl, :], as_ref[sl, :]), mg[:, :MLA_WIDTH]).astype(BF16)
        ns = _rms(jnp.where(is_prompt, sp_ref[sl, :], ss_ref[sl, :]),
                  mg[:, MLA_WIDTH:MLA_WIDTH + S5_WIDTH]).astype(BF16)
        ng = _rms(g_ref[sl, :], mg[:, MLA_WIDTH + S5_WIDTH:]).astype(BF16)
        mixed = (_dot(na, wo_ref[:MLA_WIDTH, :]) + _dot(ns, wo_ref[MLA_WIDTH:MLA_WIDTH + S5_WIDTH, :])
                 + _dot(ng, wo_ref[MLA_WIDTH + S5_WIDTH:, :]))
        x = jnp.where(is_prompt, xp_ref[sl, :], xs_ref[sl, :])
        x1 = _ln(alpha * x + mixed, l1g_ref[...], l1b_ref[...])
        x1_ref[sl, :] = x1

        hi, lo = _split_bf16(x1)
        work = _dot_nt(rwh, hi) + (_dot_nt(rwh, lo) + _dot_nt(rwl, hi)) + rb_ref[...]
        tops, part_hots = [], []
        for _ in range(TOP_K):
            m = jnp.max(work, axis=0, keepdims=True)
            sel = jnp.min(jnp.where(work == m, eidx, float(N_EXPERTS)), axis=0, keepdims=True)
            hot = eidx == sel
            tops.append(m)
            part_hots.append(hot)
            work = jnp.where(hot, -jnp.inf, work)
        exps = [jnp.exp(t - tops[0]) for t in tops]
        den = exps[0] + exps[1] + exps[2] + exps[3]
        gates.append([e / den for e in exps])
        hots.append(part_hots)

    cnt = jnp.concatenate(
        [sum(h.astype(F32) for h in part_hots[1:]) + part_hots[0].astype(F32) for part_hots in hots],
        axis=1)
    tot = jnp.sum(cnt, axis=1, keepdims=True)
    seg = jnp.ceil(tot * (1.0 / SUBLANES))
    seg_l = jnp.broadcast_to(seg, (N_EXPERTS, LANES)).astype(BF16)
    start = _dot(lower_ref[...], seg_l)[:, 0:1] * float(SUBLANES)
    before = _dot(cnt.astype(BF16), tri_ref[...]) + start
    row = lax.broadcasted_iota(jnp.int32, (LANES, POST_ROWS), 0)
    for part, r0 in enumerate(range(0, tb, POST_ROWS)):
        slab = jnp.zeros((LANES, POST_ROWS), F32)
        for k in range(TOP_K):
            pos = jnp.sum(jnp.where(hots[part][k], before[:, r0:r0 + POST_ROWS], 0.0),
                          axis=0, keepdims=True)
            slab = jnp.where(row == k, gates[part][k], slab)
            slab = jnp.where(row == 2 * TOP_K + k, pos, slab)
        rt_ref[0, :, r0:r0 + POST_ROWS] = slab[:ROUTE_ROWS, :]
        info_ref[r0:r0 + POST_ROWS, :] = slab.T
    cnt_ref[0] = tot


def _post_call(attn_p, attn_s, ssm_p, ssm_s, gm, xp, xs, lw, alpha):
    n = xp.shape[0] + xs.shape[0]
    tb = TOKEN_BLOCK
    npb = xp.shape[0] // tb
    row = lambda w: pl.BlockSpec((tb, w), lambda i: (i, 0))
    prompt_row, sample_row = _split_rows(tb, npb)
    full = lambda a: pl.BlockSpec(a.shape, lambda i: (0,) * a.ndim)
    names = ['mix_g', 'w_out', 'ln1_g', 'ln1_b', 'rw_hi', 'rw_lo', 'router_b', 'tri', 'lower']
    return pl.pallas_call(
        functools.partial(_post_kernel, alpha=alpha, n_prompt_blocks=npb),
        grid=(n // tb,),
        in_specs=[prompt_row(MLA_WIDTH), sample_row(MLA_WIDTH), prompt_row(S5_WIDTH), sample_row(S5_WIDTH),
                  row(GMLP_WIDTH), prompt_row(D_MODEL), sample_row(D_MODEL)]
                 + [full(lw[k]) for k in names],
        out_specs=[row(D_MODEL), row(LANES),
                   pl.BlockSpec((1, ROUTE_ROWS, tb), lambda i: (i, 0, 0)),
                   pl.BlockSpec((1, N_EXPERTS, 1), lambda i: (i, 0, 0))],
        out_shape=[jax.ShapeDtypeStruct((n, D_MODEL), F32),
                   jax.ShapeDtypeStruct((n, LANES), F32),
                   jax.ShapeDtypeStruct((n // tb, ROUTE_ROWS, tb), F32),
                   jax.ShapeDtypeStruct((n // tb, N_EXPERTS, 1), F32)],
        compiler_params=_cparams(("parallel",)),
        name="post",
    )(attn_p, attn_s, ssm_p, ssm_s, gm, xp, xs, *[lw[k] for k in names])


def _segment_copies(tab_ref, step, src, dst, src_is_local, sem):
    copies = []
    for e in range(N_EXPERTS):
        base = (step * N_EXPERTS + e) * 3
        loc = pl.multiple_of(tab_ref[base], SUBLANES)
        glo = pl.multiple_of(tab_ref[base + 1], SUBLANES)
        n = pl.multiple_of(tab_ref[base + 2], SUBLANES)
        s_at, d_at = (loc, glo) if src_is_local else (glo, loc)
        copies.append((n, pltpu.make_async_copy(src.at[pl.ds(s_at, n), :], dst.at[pl.ds(d_at, n), :], sem)))
    return copies


def _start_copies(copies):
    for n, cp in copies:
        @pl.when(n > 0)
        def _(cp=cp):
            cp.start()


def _wait_copies(copies):
    for n, cp in copies:
        @pl.when(n > 0)
        def _(cp=cp):
            cp.wait()


def _dispatch_kernel(tab_ref, x_ref, rt_ref, xb_ref, sbuf, zbuf, sem):
    step = pl.program_id(0)
    slot = step % 2
    tb = x_ref.shape[0]
    rows = lax.broadcasted_iota(jnp.int32, (SORT_ROWS, tb), 0)
    hit = None
    gates = jnp.zeros((SORT_ROWS, tb), F32)
    for k in range(TOP_K):
        eq = rows == rt_ref[0, 2 * TOP_K + k:2 * TOP_K + k + 1, :].astype(jnp.int32)
        hit = eq if hit is None else jnp.logical_or(hit, eq)
        gates = jnp.where(eq, rt_ref[0, k:k + 1, :], gates)
    perm = jnp.where(hit, 1.0, 0.0).astype(BF16)
    xs = _dot(perm, x_ref[...].astype(BF16))
    half = D_MODEL // 2
    sbuf[slot, :, :half] = pltpu.pack_elementwise([xs[:, :half], xs[:, half:]], packed_dtype=BF16)
    row_gate = jnp.broadcast_to(jnp.sum(gates, axis=-1, keepdims=True), (SORT_ROWS, LANES))
    gate_hi = row_gate.astype(BF16).astype(F32)
    sbuf[slot, :, half:] = pltpu.pack_elementwise([gate_hi, row_gate - gate_hi], packed_dtype=BF16)

    def copies(s):
        return _segment_copies(tab_ref, s, sbuf.at[s % 2], xb_ref, True, sem.at[s % 2])

    n_steps = pl.num_programs(0)
    fill_base = n_steps * N_EXPERTS * 3
    fill_sem = sem.at[2]

    def pad_fills():
        out = []
        for e in range(N_EXPERTS):
            start = pl.multiple_of(tab_ref[fill_base + 2 * e], SUBLANES)
            n = pl.multiple_of(tab_ref[fill_base + 2 * e + 1], SUBLANES)
            out.append((n, pltpu.make_async_copy(zbuf.at[pl.ds(0, n), :], xb_ref.at[pl.ds(start, n), :], fill_sem)))
        return out

    def block_fill(j):
        row = pl.multiple_of(tab_ref[fill_base + 2 * N_EXPERTS] + j * EXPERT_BLOCK, EXPERT_BLOCK)
        return pltpu.make_async_copy(zbuf, xb_ref.at[pl.ds(row, EXPERT_BLOCK), :], fill_sem)

    n_unused = tab_ref[fill_base + 2 * N_EXPERTS + 1]

    @pl.when(step == 0)
    def _():
        zbuf[...] = jnp.zeros_like(zbuf)
        _start_copies(pad_fills())
        lax.fori_loop(0, n_unused, lambda j, c: (block_fill(j).start(), c)[1], 0)

    @pl.when(step > 0)
    def _():
        _wait_copies(copies(step - 1))

    _start_copies(copies(step))

    @pl.when(step == n_steps - 1)
    def _():
        _wait_copies(copies(step))
        _wait_copies(pad_fills())
        lax.fori_loop(0, n_unused, lambda j, c: (block_fill(j).wait(), c)[1], 0)


def _dispatch_call(seg_tab, x1, route, n_rows):
    n = x1.shape[0]
    tb = TOKEN_BLOCK
    grid_spec = pltpu.PrefetchScalarGridSpec(
        num_scalar_prefetch=1,
        grid=(n // tb,),
        in_specs=[pl.BlockSpec((tb, D_MODEL), lambda i, t: (i, 0)),
                  pl.BlockSpec((1, ROUTE_ROWS, tb), lambda i, t: (i, 0, 0))],
        out_specs=pl.BlockSpec(memory_space=pl.ANY),
        scratch_shapes=[pltpu.VMEM((2, SORT_ROWS, GROUPED_WIDTH), jnp.uint32),
                        pltpu.VMEM((EXPERT_BLOCK, GROUPED_WIDTH), jnp.uint32),
                        pltpu.SemaphoreType.DMA((3,))],
    )
    return pl.pallas_call(
        _dispatch_kernel,
        grid_spec=grid_spec,
        out_shape=jax.ShapeDtypeStruct((n_rows, GROUPED_WIDTH), jnp.uint32),
        compiler_params=_cparams(("arbitrary",)),
        name="dispatch",
    )(seg_tab, x1, route)


def _expert_kernel(be_ref, nv_ref, nx_ref, x_ref, bgu_ref, bdn_ref, wgu_hbm, wdn_hbm, y_ref,
                   wgu_st, wdn_st, wgu_bf, wdn_bf, sem):
    i = pl.program_id(0)
    e = be_ref[i]
    changed = jnp.logical_or(i == 0, be_ref[jnp.maximum(i - 1, 0)] != e)

    def fetch(ex):
        return (pltpu.make_async_copy(wgu_hbm.at[ex], wgu_st, sem.at[0]),
                pltpu.make_async_copy(wdn_hbm.at[ex], wdn_st, sem.at[1]))

    @pl.when(i == 0)
    def _():
        for cp in fetch(e):
            cp.start()

    @pl.when(changed)
    def _():
        for cp in fetch(e):
            cp.wait()
        wgu_bf[...] = wgu_st[...].astype(BF16)
        wdn_bf[...] = wdn_st[...].astype(BF16)

        @pl.when(nx_ref[i] >= 0)
        def _():
            for cp in fetch(nx_ref[i]):
                cp.start()

    @pl.when(nv_ref[i] > 0)
    def _():
        live = lax.broadcasted_iota(jnp.int32, (x_ref.shape[0], 1), 0) < nv_ref[i]
        half = D_MODEL // 2
        halves = lambda words: [jnp.where(live, pltpu.unpack_elementwise(
            words, index=idx, packed_dtype=BF16, unpacked_dtype=F32), 0.0) for idx in (0, 1)]
        x_lo, x_hi = halves(x_ref[:, :half])
        g_hi, g_rest = halves(x_ref[:, half:half + LANES])
        row_gate = (g_hi + g_rest)[:, 0:1]
        gu = (_dot(x_lo.astype(BF16), wgu_bf[:half, :]) + _dot(x_hi.astype(BF16), wgu_bf[half:, :])
              + bgu_ref[0])
        gate = jnp.minimum(gu[:, :D_FF], SWIGLU_LIMIT)
        up = jnp.clip(gu[:, D_FF:], -SWIGLU_LIMIT, SWIGLU_LIMIT)
        h = (up + 1.0) * (gate * jax.nn.sigmoid(SWIGLU_ALPHA * gate))
        y_ref[...] = (_dot(h.astype(BF16), wdn_bf[...]) + bdn_ref[0]) * row_gate

    @pl.when(nv_ref[i] == 0)
    def _():
        y_ref[...] = jnp.zeros_like(y_ref)


def _expert_call(blk_e, n_valid, next_e, xb, wgu, bgu, wdn, bdn):
    n_rows = xb.shape[0]
    eb = EXPERT_BLOCK
    grid_spec = pltpu.PrefetchScalarGridSpec(
        num_scalar_prefetch=3,
        grid=(n_rows // eb,),
        in_specs=[pl.BlockSpec((eb, GROUPED_WIDTH), lambda i, be, nv, nx: (i, 0)),
                  pl.BlockSpec((1, 1, 2 * D_FF), lambda i, be, nv, nx: (be[i], 0, 0)),
                  pl.BlockSpec((1, 1, D_MODEL), lambda i, be, nv, nx: (be[i], 0, 0)),
                  pl.BlockSpec(memory_space=pl.ANY),
                  pl.BlockSpec(memory_space=pl.ANY)],
        out_specs=pl.BlockSpec((eb, D_MODEL), lambda i, be, nv, nx: (i, 0)),
        scratch_shapes=[pltpu.VMEM((D_MODEL, 2 * D_FF), F32), pltpu.VMEM((D_FF, D_MODEL), F32),
                        pltpu.VMEM((D_MODEL, 2 * D_FF), BF16), pltpu.VMEM((D_FF, D_MODEL), BF16),
                        pltpu.SemaphoreType.DMA((2,))],
    )
    return pl.pallas_call(
        _expert_kernel,
        grid_spec=grid_spec,
        out_shape=jax.ShapeDtypeStruct((n_rows, D_MODEL), F32),
        compiler_params=_cparams(("arbitrary",)),
        name="experts",
    )(blk_e, n_valid, next_e, xb, bgu, bdn, wgu, wdn)


def _combine_kernel(tab_ref, info_ref, x1_ref, g_ref, b_ref, yb_ref, op_ref, os_ref, buf, sem, *,
                    alpha, n_prompt_blocks):
    tb = x1_ref.shape[0]
    step = pl.program_id(0)
    slot = step % 2
    tail = tb * TOP_K

    def fetch(s):
        buf[s % 2, tail:, :] = jnp.zeros((SORT_ROWS - tail, D_MODEL), F32)
        _start_copies(_segment_copies(tab_ref, s, yb_ref, buf.at[s % 2], False, sem.at[s % 2]))

    @pl.when(step == 0)
    def _():
        fetch(step)

    @pl.when(step + 1 < pl.num_programs(0))
    def _():
        fetch(step + 1)

    _wait_copies(_segment_copies(tab_ref, step, yb_ref, buf.at[slot], False, sem.at[slot]))
    info = info_ref[...]
    cols = lax.broadcasted_iota(jnp.int32, (tb, SORT_ROWS), 1).astype(F32)
    hit = None
    for k in range(TOP_K):
        eq = cols == info[:, 2 * TOP_K + k:2 * TOP_K + k + 1]
        hit = eq if hit is None else jnp.logical_or(hit, eq)
    unsort = jnp.where(hit, 1.0, 0.0).astype(BF16)
    hi, lo = _split_bf16(buf[slot])
    moe = _dot(unsort, hi) + _dot(unsort, lo)
    out = _ln(alpha * x1_ref[...] + moe, g_ref[...], b_ref[...])

    @pl.when(step < n_prompt_blocks)
    def _():
        op_ref[...] = out

    @pl.when(step >= n_prompt_blocks)
    def _():
        os_ref[...] = out


def _combine_call(seg_tab, info, x1, ln_g, ln_b, yb, alpha, n_prompt):
    n = x1.shape[0]
    tb = TOKEN_BLOCK
    npb = n_prompt // tb
    prompt_row, sample_row = _split_rows(tb, npb)
    grid_spec = pltpu.PrefetchScalarGridSpec(
        num_scalar_prefetch=1,
        grid=(n // tb,),
        in_specs=[pl.BlockSpec((tb, LANES), lambda i, t: (i, 0)),
                  pl.BlockSpec((tb, D_MODEL), lambda i, t: (i, 0)),
                  pl.BlockSpec((1, D_MODEL), lambda i, t: (0, 0)),
                  pl.BlockSpec((1, D_MODEL), lambda i, t: (0, 0)),
                  pl.BlockSpec(memory_space=pl.ANY)],
        out_specs=[prompt_row(D_MODEL), sample_row(D_MODEL)],
        scratch_shapes=[pltpu.VMEM((2, SORT_ROWS, D_MODEL), F32), pltpu.SemaphoreType.DMA((2,))],
    )
    return pl.pallas_call(
        functools.partial(_combine_kernel, alpha=alpha, n_prompt_blocks=npb),
        grid_spec=grid_spec,
        out_shape=[jax.ShapeDtypeStruct((n_prompt, D_MODEL), F32),
                   jax.ShapeDtypeStruct((n - n_prompt, D_MODEL), F32)],
        compiler_params=_cparams(("arbitrary",)),
        name="combine",
    )(seg_tab, info, x1, ln_g, ln_b, yb)


def _rope_tables(pos):
    half = MLA_ROPE // 2
    inv_freq = ROPE_THETA ** (-jnp.arange(half, dtype=F32) / half)
    ang = pos.astype(F32)[:, None] * inv_freq[None, :]
    cos, sin = jnp.cos(ang), jnp.sin(ang)
    zero = jnp.zeros((pos.shape[0], LANES - MLA_ROPE), F32)
    return (jnp.concatenate([cos, cos, zero], axis=1), jnp.concatenate([-sin, sin, zero], axis=1))


def _s5_params(lam_re, lam_im, log_dt, b_re, b_im, c_re, c_im):
    dt = jnp.exp(log_dt)[:, None]
    mag = jnp.exp(lam_re * dt)
    ab_re = mag * jnp.cos(lam_im * dt)
    ab_im = mag * jnp.sin(lam_im * dt)
    den = lam_re * lam_re + lam_im * lam_im
    nr, ni = ab_re - 1.0, ab_im
    f_re = (nr * lam_re + ni * lam_im) / den
    f_im = (ni * lam_re - nr * lam_im) / den
    bb_re = f_re[..., None] * b_re - f_im[..., None] * b_im
    bb_im = f_re[..., None] * b_im + f_im[..., None] * b_re
    eye = jnp.eye(S5_GROUPS, dtype=F32)

    def in_blockdiag(w):
        return jnp.einsum('gnc,gh->gchn', w, eye).reshape(S5_WIDTH, S5_COLS)

    def out_blockdiag(w):
        return jnp.einsum('gcn,gh->gnhc', w, eye).reshape(S5_COLS, S5_WIDTH)

    bb = jnp.concatenate([in_blockdiag(bb_re), in_blockdiag(bb_im)], axis=1).astype(BF16)
    cc = jnp.concatenate([out_blockdiag(c_re), -out_blockdiag(c_im)], axis=0).astype(BF16)
    ar, ai = ab_re.reshape(1, S5_COLS), ab_im.reshape(1, S5_COLS)
    pr, pi = [ar], [ai]
    for _ in range(SUBLANES - 1):
        pr, pi = pr + [pr[-1] * ar - pi[-1] * ai], pi + [pr[-1] * ai + pi[-1] * ar]
    rows = jnp.arange(SUBLANES)[:, None]
    tabs = []
    for d in (1, 2, 4):
        keep = rows >= d
        tabs.append(jnp.stack([jnp.where(keep, pr[d - 1], 0.0), jnp.where(keep, pi[d - 1], 0.0)]))
    tabs.append(jnp.stack([jnp.concatenate(pr, axis=0), jnp.concatenate(pi, axis=0)]))
    return bb, cc, jnp.stack(tabs)


def _gmlp_spatial(w_s, b_s, chunk_len):
    reps = GMLP_CHUNK // chunk_len
    i = jnp.arange(chunk_len)
    mask = (i[None, :] // CHUNK) <= (i[:, None] // CHUNK)
    w = jnp.where(mask[None], w_s[:, :chunk_len, :chunk_len], 0.0)
    eye = jnp.eye(reps, dtype=F32)
    wblk = jnp.einsum('hij,rs->hrisj', w, eye).reshape(GMLP_HEADS, GMLP_CHUNK, GMLP_CHUNK)
    wcat = jnp.transpose(wblk, (1, 0, 2)).reshape(GMLP_CHUNK, GMLP_HEADS * GMLP_CHUNK)
    bias = jnp.tile(b_s[:, :chunk_len], (1, reps))
    bias = jnp.repeat(bias.T, GMLP_HEAD_DIM, axis=1)
    return wcat.astype(BF16), bias


def _layer_weights(p, l, cos, sin, dec_seq):
    lw = {'cos': cos, 'sin': sin}
    w_in = p['w_in'][l]
    kr = w_in[:, _C_KR:_C_KR + MLA_ROPE]
    half = MLA_ROPE // 2
    zero = jnp.zeros((D_MODEL, LANES - MLA_ROPE), F32)
    rest = w_in[:, _C_KR + MLA_ROPE:]
    lw['w_in'] = jnp.concatenate(
        [w_in[:, :_C_KR], kr, zero, kr[:, half:], kr[:, :half], zero, rest], axis=1).astype(BF16)
    wq = p['w_q_b'][l].reshape(Q_LORA, MLA_HEADS, MLA_NOPE + MLA_ROPE)
    zq = jnp.zeros((Q_LORA, MLA_HEADS, LANES - MLA_ROPE), F32)
    rope = wq[:, :, MLA_NOPE:]
    plain = jnp.concatenate([wq, zq], axis=2).reshape(Q_LORA, QK_WIDTH)
    swapped = jnp.concatenate([rope[:, :, half:], rope[:, :, :half], zq], axis=2)
    lw['wq'] = jnp.concatenate([plain, swapped.reshape(Q_LORA, MLA_HEADS * LANES)], axis=1).astype(BF16)
    lw['q_g'] = p['q_a_norm_g'][l].reshape(1, Q_LORA)
    lw['kv_g'] = p['kv_a_norm_g'][l].reshape(1, KV_LORA)
    lw['wkv'] = p['w_kv_b'][l].astype(BF16)
    lw['s5_bb'], lw['s5_cc'], lw['s5_tab'] = _s5_params(
        p['s5_lam_re'][l], p['s5_lam_im'][l], p['s5_log_dt'][l], p['s5_b_re'][l], p['s5_b_im'][l],
        p['s5_c_re'][l], p['s5_c_im'][l])
    lw['s5_d'] = p['s5_d'][l].reshape(1, S5_WIDTH)
    lw['s5_wg'] = p['s5_w_glu'][l].astype(BF16)
    lw['s5_bg'] = p['s5_b_glu'][l].reshape(1, S5_WIDTH)
    lw['g_g'] = p['gmlp_norm_g'][l].reshape(1, GMLP_WIDTH)
    lw['g_b'] = p['gmlp_norm_b'][l].reshape(1, GMLP_WIDTH)
    grp = jnp.arange(GMLP_WIDTH) // GMLP_HEAD_DIM
    lw['mavg'] = jnp.where(grp[:, None] == grp[None, :], 1.0 / GMLP_HEAD_DIM, 0.0).astype(BF16)
    wp, bp = _gmlp_spatial(p['gmlp_w_s'][l], p['gmlp_b_s'][l], GMLP_CHUNK)
    ws, bs = _gmlp_spatial(p['gmlp_w_s'][l], p['gmlp_b_s'][l], dec_seq)
    lw['wsp'] = jnp.stack([wp, ws])
    lw['bsp'] = jnp.stack([bp, bs])
    lw['mix_g'] = p['mix_norm_g'][l].reshape(1, -1)
    lw['w_out'] = p['w_out'][l].astype(BF16)
    lw['ln1_g'] = p['ln1_g'][l].reshape(1, D_MODEL)
    lw['ln1_b'] = p['ln1_b'][l].reshape(1, D_MODEL)
    rw = p['router_w'][l].T
    lw['rw_hi'] = rw.astype(BF16)
    lw['rw_lo'] = (rw - lw['rw_hi'].astype(F32)).astype(BF16)
    lw['router_b'] = p['router_b'][l].reshape(N_EXPERTS, 1)
    t = jnp.arange(TOKEN_BLOCK)
    lw['tri'] = (t[:, None] < t[None, :]).astype(BF16)
    e = jnp.arange(N_EXPERTS)
    lw['lower'] = (e[None, :] < e[:, None]).astype(BF16)
    lw['ln2_g'] = p['ln2_g'][l].reshape(1, D_MODEL)
    lw['ln2_b'] = p['ln2_b'][l].reshape(1, D_MODEL)
    return lw


def _moe(x1, info, route, counts, p, l, lw, alpha, n_prompt):
    n = x1.shape[0]
    eb = EXPERT_BLOCK
    tb = TOKEN_BLOCK
    nb = n // tb
    cnt = counts.reshape(nb, N_EXPERTS).astype(jnp.int32)
    seg = (cnt + SUBLANES - 1) // SUBLANES * SUBLANES
    local_start = jnp.cumsum(seg, axis=1) - seg
    run = jnp.cumsum(seg, axis=0) - seg
    total = jnp.sum(seg, axis=0)
    padded = (total + eb - 1) // eb * eb
    pad_end = jnp.cumsum(padded)
    pad_start = pad_end - padded
    n_blk = -(-(n * TOP_K + nb * N_EXPERTS * (SUBLANES - 1)) // eb) + N_EXPERTS
    n_used = (pad_end[-1] // eb).astype(jnp.int32)
    seg_tab = jnp.concatenate([
        jnp.stack([local_start, pad_start[None, :] + run, seg], axis=-1).reshape(-1),
        jnp.stack([pad_start + total, padded - total], axis=-1).reshape(-1),
        jnp.stack([pad_end[-1], n_blk - n_used])]).astype(jnp.int32)
    experts = jnp.arange(N_EXPERTS, dtype=jnp.int32)
    blk_start = jnp.arange(n_blk, dtype=jnp.int32)[:, None] * eb
    member = (blk_start >= pad_start[None, :]) & (blk_start < pad_end[None, :])
    used = jnp.any(member, axis=1)
    pick = lambda per_expert: jnp.sum(jnp.where(member, per_expert[None, :], 0), axis=1)
    last_e = jnp.max(jnp.where(padded > 0, experts, 0))
    blk_e = jnp.where(used, pick(experts), last_e).astype(jnp.int32)
    n_valid = jnp.sum(jnp.where(member, jnp.clip((pad_start + total)[None, :] - blk_start, 0, eb), 0),
                      axis=1).astype(jnp.int32)
    has_rows = jnp.where(padded > 0, experts, 2 * N_EXPERTS)
    later = jnp.concatenate([lax.cummin(has_rows, reverse=True)[1:],
                             jnp.full((1,), 2 * N_EXPERTS, jnp.int32)])
    next_e = jnp.where(used, pick(jnp.where(later < N_EXPERTS, later + l * N_EXPERTS, -1)), -1).astype(jnp.int32)
    xb = _dispatch_call(seg_tab, x1, route, n_blk * eb)
    depth = p['moe_w_gu'].shape[0]
    yb = _expert_call(blk_e + l * N_EXPERTS, n_valid, next_e, xb,
                      p['moe_w_gu'].reshape(depth * N_EXPERTS, D_MODEL, 2 * D_FF),
                      p['moe_b_gu'].reshape(depth * N_EXPERTS, 1, 2 * D_FF),
                      p['moe_w_down'].reshape(depth * N_EXPERTS, D_FF, D_MODEL),
                      p['moe_b_down'].reshape(depth * N_EXPERTS, 1, D_MODEL))
    return _combine_call(seg_tab, info, x1, lw['ln2_g'], lw['ln2_b'], yb, alpha, n_prompt)


def kernel(x_prompt, x_sample, cache_mla_latent, cache_mla_krope, state_s5_re, state_s5_im, w_in, q_a_norm_g, w_q_b, kv_a_norm_g, w_kv_b, s5_lam_re, s5_lam_im, s5_log_dt, s5_b_re, s5_b_im, s5_c_re, s5_c_im, s5_d, s5_w_glu, s5_b_glu, gmlp_norm_g, gmlp_norm_b, gmlp_w_s, gmlp_b_s, mix_norm_g, w_out, ln1_g, ln1_b, router_w, router_b, moe_w_gu, moe_b_gu, moe_w_down, moe_b_down, ln2_g, ln2_b):
    p = dict(w_in=w_in, q_a_norm_g=q_a_norm_g, w_q_b=w_q_b, kv_a_norm_g=kv_a_norm_g, w_kv_b=w_kv_b,
             s5_lam_re=s5_lam_re, s5_lam_im=s5_lam_im, s5_log_dt=s5_log_dt, s5_b_re=s5_b_re,
             s5_b_im=s5_b_im, s5_c_re=s5_c_re, s5_c_im=s5_c_im, s5_d=s5_d, s5_w_glu=s5_w_glu,
             s5_b_glu=s5_b_glu, gmlp_norm_g=gmlp_norm_g, gmlp_norm_b=gmlp_norm_b, gmlp_w_s=gmlp_w_s,
             gmlp_b_s=gmlp_b_s, mix_norm_g=mix_norm_g, w_out=w_out, ln1_g=ln1_g, ln1_b=ln1_b,
             router_w=router_w, router_b=router_b, moe_w_gu=moe_w_gu, moe_b_gu=moe_b_gu,
             moe_w_down=moe_w_down, moe_b_down=moe_b_down, ln2_g=ln2_g, ln2_b=ln2_b)
    depth = w_in.shape[0]
    bp, sp, _ = x_prompt.shape
    bs, ss, _ = x_sample.shape
    past = cache_mla_latent.shape[2]
    n_p, n_s = bp * sp, bs * ss
    assert sp % ATTN_BLOCK == 0 and sp % S5_BLOCK == 0 and sp % TOKEN_BLOCK == 0
    assert n_s % TOKEN_BLOCK == 0 and GMLP_CHUNK % ss == 0 and ss % SUBLANES == 0
    alpha = float((2 * depth) ** 0.25)

    pos = jnp.concatenate([jnp.arange(sp, dtype=jnp.int32),
                           jnp.tile(past + jnp.arange(ss, dtype=jnp.int32), bs)])
    cos, sin = _rope_tables(pos)
    xp, xs = x_prompt.reshape(n_p, D_MODEL), x_sample.reshape(n_s, D_MODEL)
    zero_state = jnp.zeros((bp, 1, S5_COLS), F32)

    outs = {k: [] for k in ('lat_p', 'kr_p', 'sre_p', 'sim_p', 'lat_s', 'kr_s', 'sre_s', 'sim_s', 'gv_s')}
    for l in range(depth):
        lw = _layer_weights(p, l, cos, sin, ss)
        q, k, v, lat, kr, u, gm, gv = _pre_call(xp, xs, lw)
        attn_p = _attn_call(q, k, v, bp, sp)
        attn_s = _attn_sample_call(q, k, v, cache_mla_latent, cache_mla_krope, l, lw['wkv'],
                                   n_p, bs, ss)
        ssm_p, sre_p, sim_p = _s5_call(u, zero_state, zero_state, lw, 0, bp, sp, S5_BLOCK, "s5_prompt")
        ssm_s, sre_s, sim_s = _s5_call(u, state_s5_re[l].reshape(bs, 1, S5_COLS).astype(F32),
                                       state_s5_im[l].reshape(bs, 1, S5_COLS).astype(F32),
                                       lw, n_p, bs, ss, ss, "s5_sample")
        x1, info, route, counts = _post_call(attn_p, attn_s, ssm_p, ssm_s, gm, xp, xs, lw, alpha)
        xp, xs = _moe(x1, info, route, counts, p, l, lw, alpha, n_p)

        outs['lat_p'].append(lat[:n_p].reshape(bp, sp, KV_LORA))
        outs['kr_p'].append(kr[:n_p].reshape(bp, sp, MLA_ROPE))
        outs['sre_p'].append(sre_p.reshape(bp, S5_GROUPS, S5_STATE))
        outs['sim_p'].append(sim_p.reshape(bp, S5_GROUPS, S5_STATE))
        outs['lat_s'].append(lat[n_p:].reshape(bs, ss, KV_LORA))
        outs['kr_s'].append(kr[n_p:].reshape(bs, ss, MLA_ROPE))
        outs['sre_s'].append(sre_s.reshape(bs, S5_GROUPS, S5_STATE))
        outs['sim_s'].append(sim_s.reshape(bs, S5_GROUPS, S5_STATE))
        outs['gv_s'].append(gv[n_p:].reshape(bs, ss, GMLP_WIDTH))

    st = lambda name: jnp.stack(outs[name])
    return (xp.reshape(bp, sp, D_MODEL), xs.reshape(bs, ss, D_MODEL),
            st('lat_p'), st('kr_p'), st('sre_p'), st('sim_p'),
            st('lat_s'), st('kr_s'), st('sre_s'), st('sim_s'), st('gv_s'))
```

```python
import functools
import math

import jax
import jax.numpy as jnp
import numpy as np
from jax import lax
from jax.experimental import pallas as pl
from jax.experimental.pallas import tpu as pltpu

F32 = jnp.float32
BF16 = jnp.bfloat16

D_MODEL = 1024
CHUNK = 64
MLA_HEADS = 4
MLA_NOPE = 128
MLA_ROPE = 64
MLA_V = 128
Q_LORA = 256
KV_LORA = 128
MLA_WIDTH = MLA_HEADS * MLA_V
MLA_SCALE = (MLA_NOPE + MLA_ROPE) ** -0.5
Q_SCALE = MLA_SCALE * math.log2(math.e)
ROPE_THETA = 10000.0
S5_GROUP = 16
S5_GROUPS = 16
S5_WIDTH = S5_GROUP * S5_GROUPS
S5_STATE = 64
S5_COLS = S5_GROUPS * S5_STATE
GMLP_HEADS = 4
GMLP_HEAD_DIM = 64
GMLP_WIDTH = GMLP_HEADS * GMLP_HEAD_DIM
GMLP_CHUNK = 128
N_EXPERTS = 32
TOP_K = 4
D_FF = D_MODEL
SWIGLU_LIMIT = 7.0
SWIGLU_ALPHA = 1.702
NORM_EPS = 1e-5

LANES = 128
SUBLANES = 8
TOKEN_BLOCK = 256
ATTN_BLOCK = 1024
ATTN_HEADS = 2
EXPERT_BLOCK = 256
S5_BLOCK = 1024
SORT_ROWS = TOKEN_BLOCK * TOP_K + N_EXPERTS * SUBLANES
ROUTE_ROWS = 2 * SUBLANES
POST_ROWS = 128
GROUPED_WIDTH = D_MODEL // 2 + LANES
VMEM_LIMIT = 56 * 1024 * 1024

_C_Q = 0
_C_KV = _C_Q + Q_LORA
_C_KR = _C_KV + KV_LORA
_C_KRS = _C_KR + LANES
_C_S5 = _C_KRS + LANES
_C_G = _C_S5 + S5_WIDTH
IN_EXT = _C_G + 2 * GMLP_WIDTH
HEAD_SLAB = MLA_NOPE + LANES
QK_WIDTH = MLA_HEADS * HEAD_SLAB


def _cparams(sem, vmem=VMEM_LIMIT):
    return pltpu.CompilerParams(dimension_semantics=sem, vmem_limit_bytes=vmem)


def _dot(a, b):
    return jnp.dot(a, b, preferred_element_type=F32)


def _dot_nt(a, b):
    return lax.dot_general(a, b, (((1,), (1,)), ((), ())), preferred_element_type=F32)


def _split_bf16(x):
    hi = x.astype(BF16)
    lo = (x - hi.astype(F32)).astype(BF16)
    return hi, lo


def _rms(x, g):
    return x * lax.rsqrt(jnp.mean(x * x, axis=-1, keepdims=True) + NORM_EPS) * g


def _ln(x, g, b):
    xc = x - jnp.mean(x, axis=-1, keepdims=True)
    var = jnp.mean(xc * xc, axis=-1, keepdims=True)
    return xc * lax.rsqrt(var + NORM_EPS) * g + b


def _pre_kernel(xp_ref, xs_ref, win_ref, qg_ref, wq_ref, kvg_ref, wkv_ref, cos_ref, sin_ref,
                gg_ref, gb_ref, mavg_ref, wsp_ref, bsp_ref,
                q_ref, k_ref, v_ref, lat_ref, kr_ref, u_ref, gm_ref, gv_ref, *, n_prompt_blocks):
    is_prompt = pl.program_id(0) < n_prompt_blocks
    xb = jnp.where(is_prompt, xp_ref[...], xs_ref[...]).astype(BF16)
    proj = _dot(xb, win_ref[...])
    cos = cos_ref[...]
    sin = sin_ref[...]

    qa = _rms(proj[:, _C_Q:_C_Q + Q_LORA], qg_ref[...]).astype(BF16)
    qq = _dot(qa, wq_ref[...])
    for h in range(MLA_HEADS):
        c0 = h * HEAD_SLAB
        nope = qq[:, c0:c0 + MLA_NOPE]
        rope = (qq[:, c0 + MLA_NOPE:c0 + HEAD_SLAB] * cos
                + qq[:, QK_WIDTH + h * LANES:QK_WIDTH + (h + 1) * LANES] * sin)
        q_ref[:, c0:c0 + MLA_NOPE] = (nope * Q_SCALE).astype(BF16)
        q_ref[:, c0 + MLA_NOPE:c0 + HEAD_SLAB] = (rope * Q_SCALE).astype(BF16)

    lat = _rms(proj[:, _C_KV:_C_KV + KV_LORA], kvg_ref[...])
    lat_ref[...] = lat
    kv = _dot(lat.astype(BF16), wkv_ref[...])
    kr = proj[:, _C_KR:_C_KR + LANES] * cos + proj[:, _C_KRS:_C_KRS + LANES] * sin
    kr_ref[...] = kr[:, :MLA_ROPE]
    krb = kr.astype(BF16)
    for h in range(MLA_HEADS):
        c0 = h * HEAD_SLAB
        k_ref[:, c0:c0 + MLA_NOPE] = kv[:, h * 256:h * 256 + MLA_NOPE].astype(BF16)
        k_ref[:, c0 + MLA_NOPE:c0 + HEAD_SLAB] = krb
        v_ref[:, h * MLA_V:(h + 1) * MLA_V] = kv[:, h * 256 + MLA_NOPE:(h + 1) * 256].astype(BF16)

    u_ref[...] = proj[:, _C_S5:_C_S5 + S5_WIDTH]

    z = jax.nn.gelu(proj[:, _C_G:_C_G + 2 * GMLP_WIDTH])
    ug = z[:, :GMLP_WIDTH]
    vg = z[:, GMLP_WIDTH:]
    mavg = mavg_ref[...]
    hi, lo = _split_bf16(vg)
    xc = vg - (_dot(hi, mavg) + _dot(lo, mavg))
    hi, lo = _split_bf16(xc * xc)
    var = _dot(hi, mavg) + _dot(lo, mavg)
    vn = xc * lax.rsqrt(var + NORM_EPS) * gg_ref[...] + gb_ref[...]
    gv_ref[...] = vn
    lane = lax.broadcasted_iota(jnp.int32, (GMLP_CHUNK, GMLP_WIDTH), 1)
    wsp = wsp_ref[0]
    bsp = bsp_ref[0]
    for c in range(xp_ref.shape[0] // GMLP_CHUNK):
        r0 = c * GMLP_CHUNK
        vc = vn[r0:r0 + GMLP_CHUNK, :].astype(BF16)
        stack = jnp.concatenate(
            [jnp.where(lane // GMLP_HEAD_DIM == h, vc, jnp.zeros_like(vc))
             for h in range(GMLP_HEADS)], axis=0)
        mix = _dot(wsp, stack) + bsp
        gm_ref[r0:r0 + GMLP_CHUNK, :] = ug[r0:r0 + GMLP_CHUNK, :] * mix


def _split_rows(tb, npb):
    prompt = lambda w: pl.BlockSpec((tb, w), lambda i, *_: (jnp.minimum(i, npb - 1), 0))
    sample = lambda w: pl.BlockSpec((tb, w), lambda i, *_: (jnp.maximum(i - npb, 0), 0))
    return prompt, sample


def _pre_call(xp, xs, lw):
    tb = TOKEN_BLOCK
    n = xp.shape[0] + xs.shape[0]
    nb = n // tb
    n_prompt_blocks = xp.shape[0] // tb
    row = lambda w: pl.BlockSpec((tb, w), lambda i: (i, 0))
    prompt_row, sample_row = _split_rows(tb, n_prompt_blocks)
    full = lambda a: pl.BlockSpec(a.shape, lambda i: (0,) * a.ndim)
    variant = lambda i: (jnp.where(i < n_prompt_blocks, 0, 1), 0, 0)
    seq_blocks = lw['cos'].shape[0] // tb - (nb - n_prompt_blocks)
    rope_row = pl.BlockSpec((tb, LANES), lambda i: (
        jnp.where(i < n_prompt_blocks, i % seq_blocks, i - n_prompt_blocks + seq_blocks), 0))
    in_specs = [prompt_row(D_MODEL), sample_row(D_MODEL),
                full(lw['w_in']), full(lw['q_g']), full(lw['wq']), full(lw['kv_g']),
                full(lw['wkv']), rope_row, rope_row, full(lw['g_g']), full(lw['g_b']),
                full(lw['mavg']),
                pl.BlockSpec((1, GMLP_CHUNK, GMLP_HEADS * GMLP_CHUNK), variant),
                pl.BlockSpec((1, GMLP_CHUNK, GMLP_WIDTH), variant)]
    widths = [(QK_WIDTH, BF16), (QK_WIDTH, BF16), (MLA_WIDTH, BF16), (KV_LORA, F32),
              (MLA_ROPE, F32), (S5_WIDTH, F32), (GMLP_WIDTH, F32), (GMLP_WIDTH, F32)]
    return pl.pallas_call(
        functools.partial(_pre_kernel, n_prompt_blocks=n_prompt_blocks),
        grid=(nb,),
        in_specs=in_specs,
        out_specs=[row(w) for w, _ in widths],
        out_shape=[jax.ShapeDtypeStruct((n, w), dt) for w, dt in widths],
        compiler_params=_cparams(("parallel",)),
        name="pre",
    )(xp, xs, lw['w_in'], lw['q_g'], lw['wq'], lw['kv_g'], lw['wkv'], lw['cos'], lw['sin'],
      lw['g_g'], lw['g_b'], lw['mavg'], lw['wsp'], lw['bsp'])


def _attn_kernel(q_ref, k_ref, v_ref, o_ref):
    i = pl.program_id(2)
    bq = q_ref.shape[0]
    qs = [q_ref[:, h * HEAD_SLAB:(h + 1) * HEAD_SLAB] for h in range(ATTN_HEADS)]

    def step(h, r0, carry, mask):
        m, l, acc = carry
        s = _dot_nt(qs[h], k_ref[pl.ds(r0, bq), h * HEAD_SLAB:(h + 1) * HEAD_SLAB])
        if mask is not None:
            s = jnp.where(mask, s, -jnp.inf)
        m_new = jnp.maximum(m, jnp.max(s, axis=-1, keepdims=True))
        p = jnp.exp2(s - m_new)
        alpha = jnp.exp2(m - m_new)
        l = alpha * l + jnp.sum(p, axis=-1, keepdims=True)
        acc = alpha * acc + _dot(p.astype(BF16), v_ref[pl.ds(r0, bq), h * MLA_V:(h + 1) * MLA_V])
        return m_new, l, acc

    def body(j, carries):
        r0 = pl.multiple_of(j * bq, bq)
        return tuple(step(h, r0, carries[h], None) for h in range(ATTN_HEADS))

    init = (jnp.full((bq, 1), -jnp.inf, F32), jnp.zeros((bq, 1), F32), jnp.zeros((bq, MLA_V), F32))
    carries = lax.fori_loop(0, i, body, (init,) * ATTN_HEADS)
    r0 = pl.multiple_of(i * bq, bq)
    qc = lax.broadcasted_iota(jnp.int32, (bq, bq), 0) // CHUNK
    kc = lax.broadcasted_iota(jnp.int32, (bq, bq), 1) // CHUNK
    for h in range(ATTN_HEADS):
        m, l, acc = step(h, r0, carries[h], kc <= qc)
        o_ref[:, h * MLA_V:(h + 1) * MLA_V] = acc / l


def _attn_call(q, k, v, batch, seq):
    bq = ATTN_BLOCK
    nq = seq // bq
    nh = ATTN_HEADS
    return pl.pallas_call(
        _attn_kernel,
        grid=(batch, MLA_HEADS // nh, nq),
        in_specs=[pl.BlockSpec((bq, nh * HEAD_SLAB), lambda b, h, i: (b * nq + i, h)),
                  pl.BlockSpec((seq, nh * HEAD_SLAB), lambda b, h, i: (b, h)),
                  pl.BlockSpec((seq, nh * MLA_V), lambda b, h, i: (b, h))],
        out_specs=pl.BlockSpec((bq, nh * MLA_V), lambda b, h, i: (b * nq + i, h)),
        out_shape=jax.ShapeDtypeStruct((batch * seq, MLA_WIDTH), F32),
        compiler_params=_cparams(("parallel", "parallel", "arbitrary")),
        name="attn_prompt",
    )(q, k, v)


def _attn_sample_kernel(q_ref, k_ref, v_ref, plat_ref, pkr_ref, wkv_ref, o_ref, *, past):
    t = q_ref.shape[0]
    q = q_ref[...]
    kn = k_ref[...]
    vn = v_ref[...]
    kvp = _dot(plat_ref[0].astype(BF16), wkv_ref[...]).astype(BF16)
    krp = pkr_ref[0].astype(BF16)
    q_chunk = (past + lax.broadcasted_iota(jnp.int32, (t, 1), 0)) // CHUNK
    mask_p = lax.broadcasted_iota(jnp.int32, (t, past), 1) // CHUNK <= q_chunk
    mask_n = (past + lax.broadcasted_iota(jnp.int32, (t, t), 1)) // CHUNK <= q_chunk
    for h in range(MLA_HEADS):
        c0 = h * HEAD_SLAB
        qh = q[:, c0:c0 + HEAD_SLAB]
        s_p = (_dot_nt(qh[:, :MLA_NOPE], kvp[:, h * 256:h * 256 + MLA_NOPE])
               + _dot_nt(qh[:, MLA_NOPE:MLA_NOPE + MLA_ROPE], krp))
        s_n = _dot_nt(qh, kn[:, c0:c0 + HEAD_SLAB])
        s_p = jnp.where(mask_p, s_p, -jnp.inf)
        s_n = jnp.where(mask_n, s_n, -jnp.inf)
        m = jnp.maximum(jnp.max(s_p, axis=-1, keepdims=True), jnp.max(s_n, axis=-1, keepdims=True))
        p_p = jnp.exp2(s_p - m)
        p_n = jnp.exp2(s_n - m)
        l = jnp.sum(p_p, axis=-1, keepdims=True) + jnp.sum(p_n, axis=-1, keepdims=True)
        o = (_dot(p_p.astype(BF16), kvp[:, h * 256 + MLA_NOPE:(h + 1) * 256])
             + _dot(p_n.astype(BF16), vn[:, h * MLA_V:(h + 1) * MLA_V]))
        o_ref[:, h * MLA_V:(h + 1) * MLA_V] = o / l


def _attn_sample_call(q, k, v, cache_lat, cache_kr, layer, wkv, row0, batch, t):
    past = cache_lat.shape[2]
    past_lat = cache_lat.reshape(-1, past, KV_LORA)
    past_kr = cache_kr.reshape(-1, past, MLA_ROPE)
    lb0 = layer * batch
    blk0 = row0 // t
    return pl.pallas_call(
        functools.partial(_attn_sample_kernel, past=past),
        grid=(batch,),
        in_specs=[pl.BlockSpec((t, QK_WIDTH), lambda b: (blk0 + b, 0)),
                  pl.BlockSpec((t, QK_WIDTH), lambda b: (blk0 + b, 0)),
                  pl.BlockSpec((t, MLA_WIDTH), lambda b: (blk0 + b, 0)),
                  pl.BlockSpec((1, past, KV_LORA), lambda b: (lb0 + b, 0, 0)),
                  pl.BlockSpec((1, past, MLA_ROPE), lambda b: (lb0 + b, 0, 0)),
                  pl.BlockSpec(wkv.shape, lambda b: (0, 0))],
        out_specs=pl.BlockSpec((t, MLA_WIDTH), lambda b: (b, 0)),
        out_shape=jax.ShapeDtypeStruct((batch * t, MLA_WIDTH), F32),
        compiler_params=_cparams(("parallel",)),
        name="attn_sample",
    )(q, k, v, past_lat, past_kr, wkv)


def _s5_kernel(u_ref, s0r_ref, s0i_ref, bb_ref, tab_ref, cc_ref, d_ref, wg_ref, bg_ref,
               y_ref, sr_ref, si_ref, st_ref, cr_ref, ci_ref):
    tb = u_ref.shape[0]
    nc = S5_COLS

    @pl.when(pl.program_id(1) == 0)
    def _():
        cr_ref[...] = jnp.broadcast_to(s0r_ref[0], (SUBLANES, nc))
        ci_ref[...] = jnp.broadcast_to(s0i_ref[0], (SUBLANES, nc))

    u = u_ref[...]
    st_ref[...] = _dot(u.astype(BF16), bb_ref[...])

    def tile(r, carry):
        car, cai = carry
        r0 = pl.multiple_of(r * SUBLANES, SUBLANES)
        xr = st_ref[pl.ds(r0, SUBLANES), :nc]
        xi = st_ref[pl.ds(r0, SUBLANES), nc:]
        for si, d in enumerate((1, 2, 4)):
            pr = tab_ref[si, 0]
            pi = tab_ref[si, 1]
            sr = pltpu.roll(xr, d, 0)
            sim = pltpu.roll(xi, d, 0)
            xr, xi = xr + (pr * sr - pi * sim), xi + (pr * sim + pi * sr)
        pr = tab_ref[3, 0]
        pi = tab_ref[3, 1]
        xr, xi = xr + (pr * car - pi * cai), xi + (pr * cai + pi * car)
        st_ref[pl.ds(r0, SUBLANES), :nc] = xr
        st_ref[pl.ds(r0, SUBLANES), nc:] = xi
        return (jnp.broadcast_to(xr[SUBLANES - 1:SUBLANES, :], (SUBLANES, nc)),
                jnp.broadcast_to(xi[SUBLANES - 1:SUBLANES, :], (SUBLANES, nc)))

    car, cai = lax.fori_loop(0, tb // SUBLANES, tile, (cr_ref[...], ci_ref[...]))
    cr_ref[...] = car
    ci_ref[...] = cai
    sr_ref[0] = car[0:1, :]
    si_ref[0] = cai[0:1, :]

    y = _dot(st_ref[...].astype(BF16), cc_ref[...]) + d_ref[...] * u
    z = jax.nn.gelu(y)
    y_ref[...] = z * jax.nn.sigmoid(_dot(z.astype(BF16), wg_ref[...]) + bg_ref[...])


def _s5_call(u, s0r, s0i, lw, row0, batch, seq, tblk, name):
    nt = seq // tblk
    blk0 = row0 // tblk
    full = lambda a: pl.BlockSpec(a.shape, lambda b, t: (0,) * a.ndim)
    st_spec = pl.BlockSpec((1, 1, S5_COLS), lambda b, t: (b, 0, 0))
    return pl.pallas_call(
        _s5_kernel,
        grid=(batch, nt),
        in_specs=[pl.BlockSpec((tblk, S5_WIDTH), lambda b, t: (blk0 + b * nt + t, 0)),
                  st_spec, st_spec,
                  full(lw['s5_bb']), full(lw['s5_tab']), full(lw['s5_cc']), full(lw['s5_d']),
                  full(lw['s5_wg']), full(lw['s5_bg'])],
        out_specs=[pl.BlockSpec((tblk, S5_WIDTH), lambda b, t: (b * nt + t, 0)), st_spec, st_spec],
        out_shape=[jax.ShapeDtypeStruct((batch * seq, S5_WIDTH), F32),
                   jax.ShapeDtypeStruct((batch, 1, S5_COLS), F32),
                   jax.ShapeDtypeStruct((batch, 1, S5_COLS), F32)],
        scratch_shapes=[pltpu.VMEM((tblk, 2 * S5_COLS), F32),
                        pltpu.VMEM((SUBLANES, S5_COLS), F32),
                        pltpu.VMEM((SUBLANES, S5_COLS), F32)],
        compiler_params=_cparams(("parallel", "arbitrary")),
        name=name,
    )(u, s0r, s0i, lw['s5_bb'], lw['s5_tab'], lw['s5_cc'], lw['s5_d'], lw['s5_wg'], lw['s5_bg'])


def _post_kernel(ap_ref, as_ref, sp_ref, ss_ref, g_ref, xp_ref, xs_ref, mg_ref, wo_ref, l1g_ref,
                 l1b_ref, rwh_ref, rwl_ref, rb_ref, tri_ref, lower_ref,
                 x1_ref, info_ref, rt_ref, cnt_ref, *, alpha, n_prompt_blocks):
    tb = xp_ref.shape[0]
    mg = mg_ref[...]
    is_prompt = pl.program_id(0) < n_prompt_blocks
    rwh = rwh_ref[...]
    rwl = rwl_ref[...]
    eidx = lax.broadcasted_iota(jnp.int32, (N_EXPERTS, POST_ROWS), 0).astype(F32)

    gates, hots = [], []
    for r0 in range(0, tb, POST_ROWS):
        sl = slice(r0, r0 + POST_ROWS)
        na = _rms(jnp.where(is_prompt, ap_ref[sl, :], as_ref[sl, :]), mg[:, :MLA_WIDTH]).astype(BF16)
        ns = _rms(jnp.where(is_prompt, sp_ref[sl, :], ss_ref[sl, :]),
                  mg[:, MLA_WIDTH:MLA_WIDTH + S5_WIDTH]).astype(BF16)
        ng = _rms(g_ref[sl, :], mg[:, MLA_WIDTH + S5_WIDTH:]).astype(BF16)
        mixed = (_dot(na, wo_ref[:MLA_WIDTH, :]) + _dot(ns, wo_ref[MLA_WIDTH:MLA_WIDTH + S5_WIDTH, :])
                 + _dot(ng, wo_ref[MLA_WIDTH + S5_WIDTH:, :]))
        x = jnp.where(is_prompt, xp_ref[sl, :], xs_ref[sl, :])
        x1 = _ln(alpha * x + mixed, l1g_ref[...], l1b_ref[...])
        x1_ref[sl, :] = x1

        hi, lo = _split_bf16(x1)
        work = _dot_nt(rwh, hi) + (_dot_nt(rwh, lo) + _dot_nt(rwl, hi)) + rb_ref[...]
        tops, part_hots = [], []
        for _ in range(TOP_K):
            m = jnp.max(work, axis=0, keepdims=True)
            sel = jnp.min(jnp.where(work == m, eidx, float(N_EXPERTS)), axis=0, keepdims=True)
            hot = eidx == sel
            tops.append(m)
            part_hots.append(hot)
            work = jnp.where(hot, -jnp.inf, work)
        exps = [jnp.exp(t - tops[0]) for t in tops]
        den = exps[0] + exps[1] + exps[2] + exps[3]
        gates.append([e / den for e in exps])
        hots.append(part_hots)

    cnt = jnp.concatenate(
        [sum(h.astype(F32) for h in part_hots[1:]) + part_hots[0].astype(F32) for part_hots in hots],
        axis=1)
    tot = jnp.sum(cnt, axis=1, keepdims=True)
    seg = jnp.ceil(tot * (1.0 / SUBLANES))
    seg_l = jnp.broadcast_to(seg, (N_EXPERTS, LANES)).astype(BF16)
    start = _dot(lower_ref[...], seg_l)[:, 0:1] * float(SUBLANES)
    before = _dot(cnt.astype(BF16), tri_ref[...]) + start
    row = lax.broadcasted_iota(jnp.int32, (LANES, POST_ROWS), 0)
    for part, r0 in enumerate(range(0, tb, POST_ROWS)):
        slab = jnp.zeros((LANES, POST_ROWS), F32)
        for k in range(TOP_K):
            pos = jnp.sum(jnp.where(hots[part][k], before[:, r0:r0 + POST_ROWS], 0.0),
                          axis=0, keepdims=True)
            slab = jnp.where(row == k, gates[part][k], slab)
            slab = jnp.where(row == 2 * TOP_K + k, pos, slab)
        rt_ref[0, :, r0:r0 + POST_ROWS] = slab[:ROUTE_ROWS, :]
        info_ref[r0:r0 + POST_ROWS, :] = slab.T
    cnt_ref[0] = tot


def _post_call(attn_p, attn_s, ssm_p, ssm_s, gm, xp, xs, lw, alpha):
    n = xp.shape[0] + xs.shape[0]
    tb = TOKEN_BLOCK
    npb = xp.shape[0] // tb
    row = lambda w: pl.BlockSpec((tb, w), lambda i: (i, 0))
    prompt_row, sample_row = _split_rows(tb, npb)
    full = lambda a: pl.BlockSpec(a.shape, lambda i: (0,) * a.ndim)
    names = ['mix_g', 'w_out', 'ln1_g', 'ln1_b', 'rw_hi', 'rw_lo', 'router_b', 'tri', 'lower']
    return pl.pallas_call(
        functools.partial(_post_kernel, alpha=alpha, n_prompt_blocks=npb),
        grid=(n // tb,),
        in_specs=[prompt_row(MLA_WIDTH), sample_row(MLA_WIDTH), prompt_row(S5_WIDTH), sample_row(S5_WIDTH),
                  row(GMLP_WIDTH), prompt_row(D_MODEL), sample_row(D_MODEL)]
                 + [full(lw[k]) for k in names],
        out_specs=[row(D_MODEL), row(LANES),
                   pl.BlockSpec((1, ROUTE_ROWS, tb), lambda i: (i, 0, 0)),
                   pl.BlockSpec((1, N_EXPERTS, 1), lambda i: (i, 0, 0))],
        out_shape=[jax.ShapeDtypeStruct((n, D_MODEL), F32),
                   jax.ShapeDtypeStruct((n, LANES), F32),
                   jax.ShapeDtypeStruct((n // tb, ROUTE_ROWS, tb), F32),
                   jax.ShapeDtypeStruct((n // tb, N_EXPERTS, 1), F32)],
        compiler_params=_cparams(("parallel",)),
        name="post",
    )(attn_p, attn_s, ssm_p, ssm_s, gm, xp, xs, *[lw[k] for k in names])


def _segment_copies(tab_ref, step, src, dst, src_is_local, sem):
    copies = []
    for e in range(N_EXPERTS):
        base = (step * N_EXPERTS + e) * 3
        loc = pl.multiple_of(tab_ref[base], SUBLANES)
        glo = pl.multiple_of(tab_ref[base + 1], SUBLANES)
        n = pl.multiple_of(tab_ref[base + 2], SUBLANES)
        s_at, d_at = (loc, glo) if src_is_local else (glo, loc)
        copies.append((n, pltpu.make_async_copy(src.at[pl.ds(s_at, n), :], dst.at[pl.ds(d_at, n), :], sem)))
    return copies


def _start_copies(copies):
    for idx, (n, cp) in enumerate(copies):
        @pl.when(n > 0)
        def _(cp=cp, idx=idx):
            cp.start(priority=idx % 2)


def _wait_copies(copies):
    for n, cp in copies:
        @pl.when(n > 0)
        def _(cp=cp):
            cp.wait()


def _dispatch_kernel(tab_ref, x_ref, rt_ref, xb_ref, sbuf, zbuf, sem):
    step = pl.program_id(0)
    slot = step % 2
    tb = x_ref.shape[0]
    rows = lax.broadcasted_iota(jnp.int32, (SORT_ROWS, tb), 0)
    hit = None
    gates = jnp.zeros((SORT_ROWS, tb), F32)
    for k in range(TOP_K):
        eq = rows == rt_ref[0, 2 * TOP_K + k:2 * TOP_K + k + 1, :].astype(jnp.int32)
        hit = eq if hit is None else jnp.logical_or(hit, eq)
        gates = jnp.where(eq, rt_ref[0, k:k + 1, :], gates)
    perm = jnp.where(hit, 1.0, 0.0).astype(BF16)
    xs = _dot(perm, x_ref[...].astype(BF16))
    half = D_MODEL // 2
    sbuf[slot, :, :half] = pltpu.pack_elementwise([xs[:, :half], xs[:, half:]], packed_dtype=BF16)
    row_gate = jnp.broadcast_to(jnp.sum(gates, axis=-1, keepdims=True), (SORT_ROWS, LANES))
    gate_hi = row_gate.astype(BF16).astype(F32)
    sbuf[slot, :, half:] = pltpu.pack_elementwise([gate_hi, row_gate - gate_hi], packed_dtype=BF16)

    def copies(s):
        return _segment_copies(tab_ref, s, sbuf.at[s % 2], xb_ref, True, sem.at[s % 2])

    n_steps = pl.num_programs(0)
    fill_base = n_steps * N_EXPERTS * 3
    fill_sem = sem.at[2]

    def pad_fills():
        out = []
        for e in range(N_EXPERTS):
            start = pl.multiple_of(tab_ref[fill_base + 2 * e], SUBLANES)
            n = pl.multiple_of(tab_ref[fill_base + 2 * e + 1], SUBLANES)
            out.append((n, pltpu.make_async_copy(zbuf.at[pl.ds(0, n), :], xb_ref.at[pl.ds(start, n), :], fill_sem)))
        return out

    def block_fill(j):
        row = pl.multiple_of(tab_ref[fill_base + 2 * N_EXPERTS] + j * EXPERT_BLOCK, EXPERT_BLOCK)
        return pltpu.make_async_copy(zbuf, xb_ref.at[pl.ds(row, EXPERT_BLOCK), :], fill_sem)

    n_unused = tab_ref[fill_base + 2 * N_EXPERTS + 1]

    @pl.when(step == 0)
    def _():
        zbuf[...] = jnp.zeros_like(zbuf)
        _start_copies(pad_fills())
        lax.fori_loop(0, n_unused, lambda j, c: (block_fill(j).start(), c)[1], 0)

    @pl.when(step > 0)
    def _():
        _wait_copies(copies(step - 1))

    _start_copies(copies(step))

    @pl.when(step == n_steps - 1)
    def _():
        _wait_copies(copies(step))
        _wait_copies(pad_fills())
        lax.fori_loop(0, n_unused, lambda j, c: (block_fill(j).wait(), c)[1], 0)


def _dispatch_call(seg_tab, x1, route, n_rows):
    n = x1.shape[0]
    tb = TOKEN_BLOCK
    grid_spec = pltpu.PrefetchScalarGridSpec(
        num_scalar_prefetch=1,
        grid=(n // tb,),
        in_specs=[pl.BlockSpec((tb, D_MODEL), lambda i, t: (i, 0)),
                  pl.BlockSpec((1, ROUTE_ROWS, tb), lambda i, t: (i, 0, 0))],
        out_specs=pl.BlockSpec(memory_space=pl.ANY),
        scratch_shapes=[pltpu.VMEM((2, SORT_ROWS, GROUPED_WIDTH), jnp.uint32),
                        pltpu.VMEM((EXPERT_BLOCK, GROUPED_WIDTH), jnp.uint32),
                        pltpu.SemaphoreType.DMA((3,))],
    )
    return pl.pallas_call(
        _dispatch_kernel,
        grid_spec=grid_spec,
        out_shape=jax.ShapeDtypeStruct((n_rows, GROUPED_WIDTH), jnp.uint32),
        compiler_params=_cparams(("arbitrary",)),
        name="dispatch",
    )(seg_tab, x1, route)


def _expert_kernel(be_ref, nv_ref, nx_ref, x_ref, bgu_ref, bdn_ref, wgu_hbm, wdn_hbm, y_ref,
                   wgu_st, wdn_st, wgu_bf, wdn_bf, sem):
    i = pl.program_id(0)
    e = be_ref[i]
    changed = jnp.logical_or(i == 0, be_ref[jnp.maximum(i - 1, 0)] != e)

    def fetch(ex):
        return (pltpu.make_async_copy(wgu_hbm.at[ex], wgu_st, sem.at[0]),
                pltpu.make_async_copy(wdn_hbm.at[ex], wdn_st, sem.at[1]))

    @pl.when(i == 0)
    def _():
        for cp in fetch(e):
            cp.start()

    @pl.when(changed)
    def _():
        for cp in fetch(e):
            cp.wait()
        wgu_bf[...] = wgu_st[...].astype(BF16)
        wdn_bf[...] = wdn_st[...].astype(BF16)

        @pl.when(nx_ref[i] >= 0)
        def _():
            for cp in fetch(nx_ref[i]):
                cp.start()

    @pl.when(nv_ref[i] > 0)
    def _():
        live = lax.broadcasted_iota(jnp.int32, (x_ref.shape[0], 1), 0) < nv_ref[i]
        half = D_MODEL // 2
        halves = lambda words: [jnp.where(live, pltpu.unpack_elementwise(
            words, index=idx, packed_dtype=BF16, unpacked_dtype=F32), 0.0) for idx in (0, 1)]
        x_lo, x_hi = halves(x_ref[:, :half])
        g_hi, g_rest = halves(x_ref[:, half:half + LANES])
        row_gate = (g_hi + g_rest)[:, 0:1]
        gu = (_dot(x_lo.astype(BF16), wgu_bf[:half, :]) + _dot(x_hi.astype(BF16), wgu_bf[half:, :])
              + bgu_ref[0])
        gate = jnp.minimum(gu[:, :D_FF], SWIGLU_LIMIT)
        up = jnp.clip(gu[:, D_FF:], -SWIGLU_LIMIT, SWIGLU_LIMIT)
        h = (up + 1.0) * (gate * jax.nn.sigmoid(SWIGLU_ALPHA * gate))
        y_ref[...] = (_dot(h.astype(BF16), wdn_bf[...]) + bdn_ref[0]) * row_gate

    @pl.when(nv_ref[i] == 0)
    def _():
        y_ref[...] = jnp.zeros_like(y_ref)


def _expert_call(blk_e, n_valid, next_e, xb, wgu, bgu, wdn, bdn):
    n_rows = xb.shape[0]
    eb = EXPERT_BLOCK
    grid_spec = pltpu.PrefetchScalarGridSpec(
        num_scalar_prefetch=3,
        grid=(n_rows // eb,),
        in_specs=[pl.BlockSpec((eb, GROUPED_WIDTH), lambda i, be, nv, nx: (i, 0)),
                  pl.BlockSpec((1, 1, 2 * D_FF), lambda i, be, nv, nx: (be[i], 0, 0)),
                  pl.BlockSpec((1, 1, D_MODEL), lambda i, be, nv, nx: (be[i], 0, 0)),
                  pl.BlockSpec(memory_space=pl.ANY),
                  pl.BlockSpec(memory_space=pl.ANY)],
        out_specs=pl.BlockSpec((eb, D_MODEL), lambda i, be, nv, nx: (i, 0)),
        scratch_shapes=[pltpu.VMEM((D_MODEL, 2 * D_FF), F32), pltpu.VMEM((D_FF, D_MODEL), F32),
                        pltpu.VMEM((D_MODEL, 2 * D_FF), BF16), pltpu.VMEM((D_FF, D_MODEL), BF16),
                        pltpu.SemaphoreType.DMA((2,))],
    )
    return pl.pallas_call(
        _expert_kernel,
        grid_spec=grid_spec,
        out_shape=jax.ShapeDtypeStruct((n_rows, D_MODEL), F32),
        compiler_params=_cparams(("arbitrary",)),
        name="experts",
    )(blk_e, n_valid, next_e, xb, bgu, bdn, wgu, wdn)


def _combine_kernel(tab_ref, info_ref, x1_ref, g_ref, b_ref, yb_ref, op_ref, os_ref, buf, sem, *,
                    alpha, n_prompt_blocks):
    tb = x1_ref.shape[0]
    step = pl.program_id(0)
    slot = step % 2
    tail = tb * TOP_K

    def fetch(s):
        buf[s % 2, tail:, :] = jnp.zeros((SORT_ROWS - tail, D_MODEL), F32)
        _start_copies(_segment_copies(tab_ref, s, yb_ref, buf.at[s % 2], False, sem.at[s % 2]))

    @pl.when(step == 0)
    def _():
        fetch(step)

    @pl.when(step + 1 < pl.num_programs(0))
    def _():
        fetch(step + 1)

    _wait_copies(_segment_copies(tab_ref, step, yb_ref, buf.at[slot], False, sem.at[slot]))
    info = info_ref[...]
    cols = lax.broadcasted_iota(jnp.int32, (tb, SORT_ROWS), 1).astype(F32)
    hit = None
    for k in range(TOP_K):
        eq = cols == info[:, 2 * TOP_K + k:2 * TOP_K + k + 1]
        hit = eq if hit is None else jnp.logical_or(hit, eq)
    unsort = jnp.where(hit, 1.0, 0.0).astype(BF16)
    hi, lo = _split_bf16(buf[slot])
    moe = _dot(unsort, hi) + _dot(unsort, lo)
    out = _ln(alpha * x1_ref[...] + moe, g_ref[...], b_ref[...])

    @pl.when(step < n_prompt_blocks)
    def _():
        op_ref[...] = out

    @pl.when(step >= n_prompt_blocks)
    def _():
        os_ref[...] = out


def _combine_call(seg_tab, info, x1, ln_g, ln_b, yb, alpha, n_prompt):
    n = x1.shape[0]
    tb = TOKEN_BLOCK
    npb = n_prompt // tb
    prompt_row, sample_row = _split_rows(tb, npb)
    grid_spec = pltpu.PrefetchScalarGridSpec(
        num_scalar_prefetch=1,
        grid=(n // tb,),
        in_specs=[pl.BlockSpec((tb, LANES), lambda i, t: (i, 0)),
                  pl.BlockSpec((tb, D_MODEL), lambda i, t: (i, 0)),
                  pl.BlockSpec((1, D_MODEL), lambda i, t: (0, 0)),
                  pl.BlockSpec((1, D_MODEL), lambda i, t: (0, 0)),
                  pl.BlockSpec(memory_space=pl.ANY)],
        out_specs=[prompt_row(D_MODEL), sample_row(D_MODEL)],
        scratch_shapes=[pltpu.VMEM((2, SORT_ROWS, D_MODEL), F32), pltpu.SemaphoreType.DMA((2,))],
    )
    return pl.pallas_call(
        functools.partial(_combine_kernel, alpha=alpha, n_prompt_blocks=npb),
        grid_spec=grid_spec,
        out_shape=[jax.ShapeDtypeStruct((n_prompt, D_MODEL), F32),
                   jax.ShapeDtypeStruct((n - n_prompt, D_MODEL), F32)],
        compiler_params=_cparams(("arbitrary",)),
        name="combine",
    )(seg_tab, info, x1, ln_g, ln_b, yb)


def _rope_tables(pos):
    half = MLA_ROPE // 2
    inv_freq = ROPE_THETA ** (-jnp.arange(half, dtype=F32) / half)
    ang = pos.astype(F32)[:, None] * inv_freq[None, :]
    cos, sin = jnp.cos(ang), jnp.sin(ang)
    zero = jnp.zeros((pos.shape[0], LANES - MLA_ROPE), F32)
    return (jnp.concatenate([cos, cos, zero], axis=1), jnp.concatenate([-sin, sin, zero], axis=1))


def _s5_params(lam_re, lam_im, log_dt, b_re, b_im, c_re, c_im):
    dt = jnp.exp(log_dt)[:, None]
    mag = jnp.exp(lam_re * dt)
    ab_re = mag * jnp.cos(lam_im * dt)
    ab_im = mag * jnp.sin(lam_im * dt)
    den = lam_re * lam_re + lam_im * lam_im
    nr, ni = ab_re - 1.0, ab_im
    f_re = (nr * lam_re + ni * lam_im) / den
    f_im = (ni * lam_re - nr * lam_im) / den
    bb_re = f_re[..., None] * b_re - f_im[..., None] * b_im
    bb_im = f_re[..., None] * b_im + f_im[..., None] * b_re
    eye = jnp.eye(S5_GROUPS, dtype=F32)

    def in_blockdiag(w):
        return jnp.einsum('gnc,gh->gchn', w, eye).reshape(S5_WIDTH, S5_COLS)

    def out_blockdiag(w):
        return jnp.einsum('gcn,gh->gnhc', w, eye).reshape(S5_COLS, S5_WIDTH)

    bb = jnp.concatenate([in_blockdiag(bb_re), in_blockdiag(bb_im)], axis=1).astype(BF16)
    cc = jnp.concatenate([out_blockdiag(c_re), -out_blockdiag(c_im)], axis=0).astype(BF16)
    ar, ai = ab_re.reshape(1, S5_COLS), ab_im.reshape(1, S5_COLS)
    pr, pi = [ar], [ai]
    for _ in range(SUBLANES - 1):
        pr, pi = pr + [pr[-1] * ar - pi[-1] * ai], pi + [pr[-1] * ai + pi[-1] * ar]
    rows = jnp.arange(SUBLANES)[:, None]
    tabs = []
    for d in (1, 2, 4):
        keep = rows >= d
        tabs.append(jnp.stack([jnp.where(keep, pr[d - 1], 0.0), jnp.where(keep, pi[d - 1], 0.0)]))
    tabs.append(jnp.stack([jnp.concatenate(pr, axis=0), jnp.concatenate(pi, axis=0)]))
    return bb, cc, jnp.stack(tabs)


def _gmlp_spatial(w_s, b_s, chunk_len):
    reps = GMLP_CHUNK // chunk_len
    i = jnp.arange(chunk_len)
    mask = (i[None, :] // CHUNK) <= (i[:, None] // CHUNK)
    w = jnp.where(mask[None], w_s[:, :chunk_len, :chunk_len], 0.0)
    eye = jnp.eye(reps, dtype=F32)
    wblk = jnp.einsum('hij,rs->hrisj', w, eye).reshape(GMLP_HEADS, GMLP_CHUNK, GMLP_CHUNK)
    wcat = jnp.transpose(wblk, (1, 0, 2)).reshape(GMLP_CHUNK, GMLP_HEADS * GMLP_CHUNK)
    bias = jnp.tile(b_s[:, :chunk_len], (1, reps))
    bias = jnp.repeat(bias.T, GMLP_HEAD_DIM, axis=1)
    return wcat.astype(BF16), bias


def _layer_weights(p, l, cos, sin, dec_seq):
    lw = {'cos': cos, 'sin': sin}
    w_in = p['w_in'][l]
    kr = w_in[:, _C_KR:_C_KR + MLA_ROPE]
    half = MLA_ROPE // 2
    zero = jnp.zeros((D_MODEL, LANES - MLA_ROPE), F32)
    rest = w_in[:, _C_KR + MLA_ROPE:]
    lw['w_in'] = jnp.concatenate(
        [w_in[:, :_C_KR], kr, zero, kr[:, half:], kr[:, :half], zero, rest], axis=1).astype(BF16)
    wq = p['w_q_b'][l].reshape(Q_LORA, MLA_HEADS, MLA_NOPE + MLA_ROPE)
    zq = jnp.zeros((Q_LORA, MLA_HEADS, LANES - MLA_ROPE), F32)
    rope = wq[:, :, MLA_NOPE:]
    plain = jnp.concatenate([wq, zq], axis=2).reshape(Q_LORA, QK_WIDTH)
    swapped = jnp.concatenate([rope[:, :, half:], rope[:, :, :half], zq], axis=2)
    lw['wq'] = jnp.concatenate([plain, swapped.reshape(Q_LORA, MLA_HEADS * LANES)], axis=1).astype(BF16)
    lw['q_g'] = p['q_a_norm_g'][l].reshape(1, Q_LORA)
    lw['kv_g'] = p['kv_a_norm_g'][l].reshape(1, KV_LORA)
    lw['wkv'] = p['w_kv_b'][l].astype(BF16)
    lw['s5_bb'], lw['s5_cc'], lw['s5_tab'] = _s5_params(
        p['s5_lam_re'][l], p['s5_lam_im'][l], p['s5_log_dt'][l], p['s5_b_re'][l], p['s5_b_im'][l],
        p['s5_c_re'][l], p['s5_c_im'][l])
    lw['s5_d'] = p['s5_d'][l].reshape(1, S5_WIDTH)
    lw['s5_wg'] = p['s5_w_glu'][l].astype(BF16)
    lw['s5_bg'] = p['s5_b_glu'][l].reshape(1, S5_WIDTH)
    lw['g_g'] = p['gmlp_norm_g'][l].reshape(1, GMLP_WIDTH)
    lw['g_b'] = p['gmlp_norm_b'][l].reshape(1, GMLP_WIDTH)
    grp = jnp.arange(GMLP_WIDTH) // GMLP_HEAD_DIM
    lw['mavg'] = jnp.where(grp[:, None] == grp[None, :], 1.0 / GMLP_HEAD_DIM, 0.0).astype(BF16)
    wp, bp = _gmlp_spatial(p['gmlp_w_s'][l], p['gmlp_b_s'][l], GMLP_CHUNK)
    ws, bs = _gmlp_spatial(p['gmlp_w_s'][l], p['gmlp_b_s'][l], dec_seq)
    lw['wsp'] = jnp.stack([wp, ws])
    lw['bsp'] = jnp.stack([bp, bs])
    lw['mix_g'] = p['mix_norm_g'][l].reshape(1, -1)
    lw['w_out'] = p['w_out'][l].astype(BF16)
    lw['ln1_g'] = p['ln1_g'][l].reshape(1, D_MODEL)
    lw['ln1_b'] = p['ln1_b'][l].reshape(1, D_MODEL)
    rw = p['router_w'][l].T
    lw['rw_hi'] = rw.astype(BF16)
    lw['rw_lo'] = (rw - lw['rw_hi'].astype(F32)).astype(BF16)
    lw['router_b'] = p['router_b'][l].reshape(N_EXPERTS, 1)
    t = jnp.arange(TOKEN_BLOCK)
    lw['tri'] = (t[:, None] < t[None, :]).astype(BF16)
    e = jnp.arange(N_EXPERTS)
    lw['lower'] = (e[None, :] < e[:, None]).astype(BF16)
    lw['ln2_g'] = p['ln2_g'][l].reshape(1, D_MODEL)
    lw['ln2_b'] = p['ln2_b'][l].reshape(1, D_MODEL)
    return lw


def _moe(x1, info, route, counts, p, l, lw, alpha, n_prompt):
    n = x1.shape[0]
    eb = EXPERT_BLOCK
    tb = TOKEN_BLOCK
    nb = n // tb
    cnt = counts.reshape(nb, N_EXPERTS).astype(jnp.int32)
    seg = (cnt + SUBLANES - 1) // SUBLANES * SUBLANES
    local_start = jnp.cumsum(seg, axis=1) - seg
    run = jnp.cumsum(seg, axis=0) - seg
    total = jnp.sum(seg, axis=0)
    padded = (total + eb - 1) // eb * eb
    pad_end = jnp.cumsum(padded)
    pad_start = pad_end - padded
    n_blk = -(-(n * TOP_K + nb * N_EXPERTS * (SUBLANES - 1)) // eb) + N_EXPERTS
    n_used = (pad_end[-1] // eb).astype(jnp.int32)
    seg_tab = jnp.concatenate([
        jnp.stack([local_start, pad_start[None, :] + run, seg], axis=-1).reshape(-1),
        jnp.stack([pad_start + total, padded - total], axis=-1).reshape(-1),
        jnp.stack([pad_end[-1], n_blk - n_used])]).astype(jnp.int32)
    experts = jnp.arange(N_EXPERTS, dtype=jnp.int32)
    blk_start = jnp.arange(n_blk, dtype=jnp.int32)[:, None] * eb
    member = (blk_start >= pad_start[None, :]) & (blk_start < pad_end[None, :])
    used = jnp.any(member, axis=1)
    pick = lambda per_expert: jnp.sum(jnp.where(member, per_expert[None, :], 0), axis=1)
    last_e = jnp.max(jnp.where(padded > 0, experts, 0))
    blk_e = jnp.where(used, pick(experts), last_e).astype(jnp.int32)
    n_valid = jnp.sum(jnp.where(member, jnp.clip((pad_start + total)[None, :] - blk_start, 0, eb), 0),
                      axis=1).astype(jnp.int32)
    has_rows = jnp.where(padded > 0, experts, 2 * N_EXPERTS)
    later = jnp.concatenate([lax.cummin(has_rows, reverse=True)[1:],
                             jnp.full((1,), 2 * N_EXPERTS, jnp.int32)])
    next_e = jnp.where(used, pick(jnp.where(later < N_EXPERTS, later + l * N_EXPERTS, -1)), -1).astype(jnp.int32)
    xb = _dispatch_call(seg_tab, x1, route, n_blk * eb)
    depth = p['moe_w_gu'].shape[0]
    yb = _expert_call(blk_e + l * N_EXPERTS, n_valid, next_e, xb,
                      p['moe_w_gu'].reshape(depth * N_EXPERTS, D_MODEL, 2 * D_FF),
                      p['moe_b_gu'].reshape(depth * N_EXPERTS, 1, 2 * D_FF),
                      p['moe_w_down'].reshape(depth * N_EXPERTS, D_FF, D_MODEL),
                      p['moe_b_down'].reshape(depth * N_EXPERTS, 1, D_MODEL))
    return _combine_call(seg_tab, info, x1, lw['ln2_g'], lw['ln2_b'], yb, alpha, n_prompt)


def kernel(x_prompt, x_sample, cache_mla_latent, cache_mla_krope, state_s5_re, state_s5_im, w_in, q_a_norm_g, w_q_b, kv_a_norm_g, w_kv_b, s5_lam_re, s5_lam_im, s5_log_dt, s5_b_re, s5_b_im, s5_c_re, s5_c_im, s5_d, s5_w_glu, s5_b_glu, gmlp_norm_g, gmlp_norm_b, gmlp_w_s, gmlp_b_s, mix_norm_g, w_out, ln1_g, ln1_b, router_w, router_b, moe_w_gu, moe_b_gu, moe_w_down, moe_b_down, ln2_g, ln2_b):
    p = dict(w_in=w_in, q_a_norm_g=q_a_norm_g, w_q_b=w_q_b, kv_a_norm_g=kv_a_norm_g, w_kv_b=w_kv_b,
             s5_lam_re=s5_lam_re, s5_lam_im=s5_lam_im, s5_log_dt=s5_log_dt, s5_b_re=s5_b_re,
             s5_b_im=s5_b_im, s5_c_re=s5_c_re, s5_c_im=s5_c_im, s5_d=s5_d, s5_w_glu=s5_w_glu,
             s5_b_glu=s5_b_glu, gmlp_norm_g=gmlp_norm_g, gmlp_norm_b=gmlp_norm_b, gmlp_w_s=gmlp_w_s,
             gmlp_b_s=gmlp_b_s, mix_norm_g=mix_norm_g, w_out=w_out, ln1_g=ln1_g, ln1_b=ln1_b,
             router_w=router_w, router_b=router_b, moe_w_gu=moe_w_gu, moe_b_gu=moe_b_gu,
             moe_w_down=moe_w_down, moe_b_down=moe_b_down, ln2_g=ln2_g, ln2_b=ln2_b)
    depth = w_in.shape[0]
    bp, sp, _ = x_prompt.shape
    bs, ss, _ = x_sample.shape
    past = cache_mla_latent.shape[2]
    n_p, n_s = bp * sp, bs * ss
    assert sp % ATTN_BLOCK == 0 and sp % S5_BLOCK == 0 and sp % TOKEN_BLOCK == 0
    assert n_s % TOKEN_BLOCK == 0 and GMLP_CHUNK % ss == 0 and ss % SUBLANES == 0
    alpha = float((2 * depth) ** 0.25)

    pos = jnp.concatenate([jnp.arange(sp, dtype=jnp.int32),
                           jnp.tile(past + jnp.arange(ss, dtype=jnp.int32), bs)])
    cos, sin = _rope_tables(pos)
    xp, xs = x_prompt.reshape(n_p, D_MODEL), x_sample.reshape(n_s, D_MODEL)
    zero_state = jnp.zeros((bp, 1, S5_COLS), F32)

    outs = {k: [] for k in ('lat_p', 'kr_p', 'sre_p', 'sim_p', 'lat_s', 'kr_s', 'sre_s', 'sim_s', 'gv_s')}
    for l in range(depth):
        lw = _layer_weights(p, l, cos, sin, ss)
        q, k, v, lat, kr, u, gm, gv = _pre_call(xp, xs, lw)
        attn_p = _attn_call(q, k, v, bp, sp)
        attn_s = _attn_sample_call(q, k, v, cache_mla_latent, cache_mla_krope, l, lw['wkv'],
                                   n_p, bs, ss)
        ssm_p, sre_p, sim_p = _s5_call(u, zero_state, zero_state, lw, 0, bp, sp, S5_BLOCK, "s5_prompt")
        ssm_s, sre_s, sim_s = _s5_call(u, state_s5_re[l].reshape(bs, 1, S5_COLS).astype(F32),
                                       state_s5_im[l].reshape(bs, 1, S5_COLS).astype(F32),
                                       lw, n_p, bs, ss, ss, "s5_sample")
        x1, info, route, counts = _post_call(attn_p, attn_s, ssm_p, ssm_s, gm, xp, xs, lw, alpha)
        xp, xs = _moe(x1, info, route, counts, p, l, lw, alpha, n_p)

        outs['lat_p'].append(lat[:n_p].reshape(bp, sp, KV_LORA))
        outs['kr_p'].append(kr[:n_p].reshape(bp, sp, MLA_ROPE))
        outs['sre_p'].append(sre_p.reshape(bp, S5_GROUPS, S5_STATE))
        outs['sim_p'].append(sim_p.reshape(bp, S5_GROUPS, S5_STATE))
        outs['lat_s'].append(lat[n_p:].reshape(bs, ss, KV_LORA))
        outs['kr_s'].append(kr[n_p:].reshape(bs, ss, MLA_ROPE))
        outs['sre_s'].append(sre_s.reshape(bs, S5_GROUPS, S5_STATE))
        outs['sim_s'].append(sim_s.reshape(bs, S5_GROUPS, S5_STATE))
        outs['gv_s'].append(gv[n_p:].reshape(bs, ss, GMLP_WIDTH))

    st = lambda name: jnp.stack(outs[name])
    return (xp.reshape(bp, sp, D_MODEL), xs.reshape(bs, ss, D_MODEL),
            st('lat_p'), st('kr_p'), st('sre_p'), st('sim_p'),
            st('lat_s'), st('kr_s'), st('sre_s'), st('sim_s'), st('gv_s'))
```
